```python
import math
import jax
import jax.numpy as jnp
from jax import lax
import numpy as np

D_MODEL = 1024
BATCH = 32
SEQ = 2048
DEPTH = 2

D_MIX = D_MODEL
S5_WIDTH = D_MIX // 2
S5_GROUP_CH = 16
S5_GROUPS = S5_WIDTH // S5_GROUP_CH
S5_STATE = 64
S5_CHUNK = 128
GLA_WIDTH = D_MIX - S5_WIDTH
GLA_HEADS = 4
GLA_DK = GLA_WIDTH // 2
GLA_HEAD_K = GLA_DK // GLA_HEADS
GLA_HEAD_V = GLA_WIDTH // GLA_HEADS
GLA_GATE_RANK = 16
GLA_GATE_TAU = 16.0
GLA_CHUNK = 64
IN_SPLITS = (S5_WIDTH, S5_WIDTH + GLA_DK, S5_WIDTH + 2 * GLA_DK, S5_WIDTH + 2 * GLA_DK + GLA_WIDTH, S5_WIDTH + 2 * GLA_DK + 2 * GLA_WIDTH)
IN_COLS = S5_WIDTH + 2 * GLA_DK + 2 * GLA_WIDTH + GLA_GATE_RANK
D_FF = 7 * D_MODEL // 2
N_EXPERTS = 8
TOP_K = 2
N_DENSE = (DEPTH + 1) // 2
N_MOE = DEPTH // 2
DN_ALPHA = (2.0 * DEPTH) ** 0.25
DN_BETA = (8.0 * DEPTH) ** -0.25
N_MOD = 6
MOD_STD = 0.1
LN_EPS = 1e-5
RMS_EPS = 1e-6

kernel_name = 'hymba_s5_gla_deepnorm_moe_trunk'


def _layer_norm(x, gain, bias):
    xf = x.astype(jnp.float32)
    mu = jnp.mean(xf, axis=-1, keepdims=True)
    var = jnp.mean(jnp.square(xf - mu), axis=-1, keepdims=True)
    return ((xf - mu) * lax.rsqrt(var + LN_EPS)).astype(x.dtype) * gain + bias


def _rms_norm(x, gain):
    xf = x.astype(jnp.float32)
    return xf * lax.rsqrt(jnp.mean(xf * xf, axis=-1, keepdims=True) + RMS_EPS) * gain.astype(jnp.float32)


def _linear_recurrence_op(e1, e2):
    a1, b1 = e1
    a2, b2 = e2
    return a1 * a2, a2 * b1 + b2


def _s5_group(u, lam_re, lam_im, log_dt, b_re, b_im, c_re, c_im, d_skip, w_glu, b_glu):
    bsz, seq, _ = u.shape
    n_chunks = seq // S5_CHUNK
    f32 = jnp.float32
    uf = u.astype(f32)
    lam = lax.complex(lam_re.astype(f32), lam_im.astype(f32))
    dt = jnp.exp(log_dt.astype(f32))[:, None]
    lam_bar = jnp.exp(lam * dt)
    b_mat = lax.complex(b_re.astype(f32), b_im.astype(f32))
    b_bar = ((lam_bar - 1.0) / lam)[:, :, None] * b_mat
    c_mat = lax.complex(c_re.astype(f32), c_im.astype(f32))
    u_chunks = uf.reshape(bsz, n_chunks, S5_CHUNK, S5_GROUPS, S5_GROUP_CH).transpose(1, 0, 2, 3, 4)
    a_elems = jnp.broadcast_to(lam_bar, (bsz, S5_CHUNK, S5_GROUPS, S5_STATE))

    def chunk_step(h, u_c):
        bu = jnp.einsum('gph,blgh->blgp', b_bar, u_c.astype(jnp.complex64))
        a_cum, s = lax.associative_scan(_linear_recurrence_op, (a_elems, bu), axis=1)
        states = s + a_cum * h[:, None]
        y = jnp.einsum('ghp,blgp->blgh', c_mat, states).real
        return states[:, -1], y

    h0 = jnp.zeros((bsz, S5_GROUPS, S5_STATE), jnp.complex64)
    _, y = lax.scan(chunk_step, h0, u_chunks)
    y = y.transpose(1, 0, 2, 3, 4).reshape(bsz, seq, S5_WIDTH) + d_skip.astype(f32) * uf
    y = jax.nn.gelu(y).astype(u.dtype)
    return y * jax.nn.sigmoid(y @ w_glu + b_glu)


def _gla_group(q, k, v, g_out, a_low, w_alpha_up, b_alpha, head_gain):
    bsz, seq, _ = q.shape
    nc = seq // GLA_CHUNK
    f32 = jnp.float32

    def heads(t, dh):
        return t.astype(f32).reshape(bsz, nc, GLA_CHUNK, GLA_HEADS, dh).transpose(1, 0, 3, 2, 4)

    log_alpha = jax.nn.log_sigmoid((a_low @ w_alpha_up + b_alpha).astype(f32)) / GLA_GATE_TAU
    qc = heads(q, GLA_HEAD_K) * (GLA_HEAD_K ** -0.5)
    kc = heads(k, GLA_HEAD_K)
    vc = heads(v, GLA_HEAD_V)
    bc = jnp.cumsum(heads(log_alpha, GLA_HEAD_K), axis=3)
    causal = jnp.tril(jnp.ones((GLA_CHUNK, GLA_CHUNK), dtype=bool))

    def chunk_step(state, inp):
        q_c, k_c, v_c, b_c = inp
        o_inter = jnp.einsum('bhik,bhkv->bhiv', q_c * jnp.exp(b_c), state)
        diff = b_c[:, :, :, None, :] - b_c[:, :, None, :, :]
        decay = jnp.exp(jnp.where(causal[:, :, None], diff, -jnp.inf))
        scores = jnp.einsum('bhik,bhjk,bhijk->bhij', q_c, k_c, decay)
        o_intra = jnp.einsum('bhij,bhjv->bhiv', scores, v_c)
        b_last = b_c[:, :, -1:, :]
        k_dec = k_c * jnp.exp(b_last - b_c)
        new_state = jnp.exp(b_last[:, :, 0, :])[..., None] * state + jnp.einsum('bhjk,bhjv->bhkv', k_dec, v_c)
        return new_state, o_inter + o_intra

    s0 = jnp.zeros((bsz, GLA_HEADS, GLA_HEAD_K, GLA_HEAD_V), f32)
    _, o = lax.scan(chunk_step, s0, (qc, kc, vc, bc))
    o = o.transpose(1, 0, 3, 2, 4).reshape(bsz, seq, GLA_HEADS, GLA_HEAD_V)
    o = _rms_norm(o, head_gain.reshape(GLA_HEADS, GLA_HEAD_V)).reshape(bsz, seq, GLA_WIDTH)
    return (o * jax.nn.silu(g_out.astype(f32))).astype(q.dtype)


def _swiglu(x, w_gate, w_up, w_down):
    return (jax.nn.silu(x @ w_gate) * (x @ w_up)) @ w_down


def _moe_swiglu(x, w_router, b_router, w_gate, w_up, w_down):
    logits = (x @ w_router).astype(jnp.float32) + b_router.astype(jnp.float32)
    top_vals, top_idx = lax.top_k(logits, TOP_K)
    top_w = jax.nn.softmax(top_vals, axis=-1)
    combine = jnp.sum(jax.nn.one_hot(top_idx, N_EXPERTS, dtype=jnp.float32) * top_w[..., None], axis=-2)
    out = jnp.zeros_like(x)
    for e in range(N_EXPERTS):
        out = out + combine[..., e:e + 1].astype(x.dtype) * _swiglu(x, w_gate[e], w_up[e], w_down[e])
    return out


def setup_inputs(seed: int = 0) -> dict:
    key = jax.random.key(seed)
    ks = jax.random.split(key, 32)
    f32 = jnp.float32

    def nrm(k, shape, std):
        return std * jax.random.normal(k, shape, f32)

    G, P, H = S5_GROUPS, S5_STATE, S5_GROUP_CH
    return {
        'x': jax.random.normal(ks[0], (BATCH, SEQ, D_MODEL), f32),
        'c': jax.random.normal(ks[1], (BATCH, D_MODEL), f32),
        'mod_w': nrm(ks[2], (DEPTH, D_MODEL, N_MOD * D_MODEL), MOD_STD * D_MODEL ** -0.5),
        'mod_b': nrm(ks[3], (DEPTH, N_MOD * D_MODEL), 0.01),
        'w_in': nrm(ks[4], (DEPTH, D_MODEL, IN_COLS), D_MODEL ** -0.5),
        'w_out': nrm(ks[5], (DEPTH, D_MIX, D_MODEL), DN_BETA * D_MIX ** -0.5),
        's5_lam_re': -0.5 * jnp.exp(nrm(ks[6], (DEPTH, G, P), 0.05)),
        's5_lam_im': math.pi * jnp.arange(P, dtype=f32)[None, None, :] + nrm(ks[7], (DEPTH, G, P), 0.01),
        's5_log_dt': jax.random.uniform(ks[8], (DEPTH, G), f32, math.log(1e-3), math.log(1e-1)),
        's5_b_re': nrm(ks[9], (DEPTH, G, P, H), (2.0 * H) ** -0.5),
        's5_b_im': nrm(ks[10], (DEPTH, G, P, H), (2.0 * H) ** -0.5),
        's5_c_re': nrm(ks[11], (DEPTH, G, H, P), (2.0 * P) ** -0.5),
        's5_c_im': nrm(ks[12], (DEPTH, G, H, P), (2.0 * P) ** -0.5),
        's5_d': nrm(ks[13], (DEPTH, S5_WIDTH), 1.0),
        's5_w_glu': nrm(ks[14], (DEPTH, S5_WIDTH, S5_WIDTH), S5_WIDTH ** -0.5),
        's5_b_glu': nrm(ks[15], (DEPTH, S5_WIDTH), 0.01),
        'gla_w_alpha_up': nrm(ks[16], (DEPTH, GLA_GATE_RANK, GLA_DK), GLA_GATE_RANK ** -0.5),
        'gla_b_alpha': nrm(ks[17], (DEPTH, GLA_DK), 0.01),
        'gla_head_gain': 1.0 + nrm(ks[18], (DEPTH, GLA_WIDTH), 0.02),
        'ln_mix_g': 1.0 + nrm(ks[19], (DEPTH, D_MODEL), 0.02),
        'ln_mix_b': nrm(ks[20], (DEPTH, D_MODEL), 0.01),
        'ffn_w_gate': nrm(ks[21], (N_DENSE, D_MODEL, D_FF), D_MODEL ** -0.5),
        'ffn_w_up': nrm(ks[22], (N_DENSE, D_MODEL, D_FF), D_MODEL ** -0.5),
        'ffn_w_down': nrm(ks[23], (N_DENSE, D_FF, D_MODEL), DN_BETA * D_FF ** -0.5),
        'moe_w_router': nrm(ks[24], (N_MOE, D_MODEL, N_EXPERTS), D_MODEL ** -0.5),
        'moe_b_router': nrm(ks[25], (N_MOE, N_EXPERTS), 0.01),
        'moe_w_gate': nrm(ks[26], (N_MOE, N_EXPERTS, D_MODEL, D_FF), D_MODEL ** -0.5),
        'moe_w_up': nrm(ks[27], (N_MOE, N_EXPERTS, D_MODEL, D_FF), D_MODEL ** -0.5),
        'moe_w_down': nrm(ks[28], (N_MOE, N_EXPERTS, D_FF, D_MODEL), DN_BETA * D_FF ** -0.5),
        'ln_ffn_g': 1.0 + nrm(ks[29], (DEPTH, D_MODEL), 0.02),
        'ln_ffn_b': nrm(ks[30], (DEPTH, D_MODEL), 0.01),
    }


def reference(x, c, mod_w, mod_b, w_in, w_out, s5_lam_re, s5_lam_im, s5_log_dt, s5_b_re, s5_b_im, s5_c_re, s5_c_im, s5_d, s5_w_glu, s5_b_glu, gla_w_alpha_up, gla_b_alpha, gla_head_gain, ln_mix_g, ln_mix_b, ffn_w_gate, ffn_w_up, ffn_w_down, moe_w_router, moe_b_router, moe_w_gate, moe_w_up, moe_w_down, ln_ffn_g, ln_ffn_b):
    c_act = jax.nn.silu(c)
    for layer in range(DEPTH):
        mod = c_act @ mod_w[layer] + mod_b[layer]
        sh_m, sc_m, gt_m, sh_f, sc_f, gt_f = jnp.split(mod[:, None, :], N_MOD, axis=-1)
        h = x * (1 + sc_m) + sh_m
        u, q, k, v, g_out, a_low = jnp.split(h @ w_in[layer], IN_SPLITS, axis=-1)
        y_s5 = _s5_group(u, s5_lam_re[layer], s5_lam_im[layer], s5_log_dt[layer], s5_b_re[layer], s5_b_im[layer], s5_c_re[layer], s5_c_im[layer], s5_d[layer], s5_w_glu[layer], s5_b_glu[layer])
        y_gla = _gla_group(q, k, v, g_out, a_low, gla_w_alpha_up[layer], gla_b_alpha[layer], gla_head_gain[layer])
        y = jnp.concatenate([y_s5, y_gla], axis=-1) @ w_out[layer]
        x = _layer_norm(DN_ALPHA * x + (1 + gt_m) * y, ln_mix_g[layer], ln_mix_b[layer])
        h = x * (1 + sc_f) + sh_f
        i = layer // 2
        if layer % 2 == 0:
            f = _swiglu(h, ffn_w_gate[i], ffn_w_up[i], ffn_w_down[i])
        else:
            f = _moe_swiglu(h, moe_w_router[i], moe_b_router[i], moe_w_gate[i], moe_w_up[i], moe_w_down[i])
        x = _layer_norm(DN_ALPHA * x + (1 + gt_f) * f, ln_ffn_g[layer], ln_ffn_b[layer])
    return x
```

```python
import functools
import math

import jax
import jax.numpy as jnp
from jax import lax
from jax.experimental import pallas as pl
from jax.experimental.pallas import tpu as pltpu

F32 = jnp.float32
BF16 = jnp.bfloat16

D_MODEL = 1024
S5_WIDTH = 512
S5_GROUP_CH = 16
S5_GROUPS = 32
S5_STATE = 64
S5_LANES = S5_GROUPS * S5_STATE
GLA_WIDTH = 512
GLA_HEADS = 4
GLA_DK = 256
GLA_HEAD_K = 64
GLA_HEAD_V = 128
GLA_GATE_RANK = 16
GLA_GATE_TAU = 16.0
GLA_CHUNK = 64
D_FF = 3584
N_EXPERTS = 8
N_MOD = 6
LN_EPS = 1e-5
RMS_EPS = 1e-6
ROUTER_LANES = 128
NEG_BIG = -1e30

VMEM_LIMIT = 56 * 1024 * 1024


def _cparams(*sem):
    return pltpu.CompilerParams(dimension_semantics=sem, vmem_limit_bytes=VMEM_LIMIT)


def _dot(a, b):
    return jnp.dot(a, b, preferred_element_type=F32)


def _split2(a):
    a1 = a.astype(BF16)
    a2 = (a - a1.astype(F32)).astype(BF16)
    return a1, a2


def _split3(a):
    a1 = a.astype(BF16)
    r1 = a - a1.astype(F32)
    a2 = r1.astype(BF16)
    a3 = (r1 - a2.astype(F32)).astype(BF16)
    return a1, a2, a3


def _dot_f32(a, b):
    a1, a2, a3 = _split3(a)
    b1, b2, b3 = _split3(b)
    lo = _dot(a1, b3) + _dot(a2, b2) + _dot(a3, b1)
    mid = _dot(a1, b2) + _dot(a2, b1)
    return lo + mid + _dot(a1, b1)


def _dot_f32x3(a, b):
    a1, a2 = _split2(a)
    b1, b2 = _split2(b)
    return (_dot(a1, b2) + _dot(a2, b1)) + _dot(a1, b1)


def _silu(x):
    return x * jax.nn.sigmoid(x)


def _layer_norm(r, gain, bias):
    mu = jnp.mean(r, axis=-1, keepdims=True)
    d = r - mu
    var = jnp.mean(d * d, axis=-1, keepdims=True)
    return d * lax.rsqrt(var + LN_EPS) * gain + bias


def _mod_kernel(c_ref, w_ref, b_ref, o_ref):
    o_ref[0] = _dot_f32(_silu(c_ref[...]), w_ref[0]) + b_ref[0]


def _modulation(c, mod_w, mod_b):
    depth, d, nd = mod_w.shape
    bsz = c.shape[0]
    return pl.pallas_call(
        _mod_kernel,
        grid=(depth, nd // d),
        in_specs=[
            pl.BlockSpec((bsz, d), lambda l, n: (0, 0)),
            pl.BlockSpec((1, d, d), lambda l, n: (l, 0, n)),
            pl.BlockSpec((1, 1, d), lambda l, n: (l, 0, n)),
        ],
        out_specs=pl.BlockSpec((1, bsz, d), lambda l, n: (l, 0, n)),
        out_shape=jax.ShapeDtypeStruct((depth, bsz, nd), F32),
        compiler_params=_cparams("parallel", "parallel"),
        name="modulation",
    )(c, mod_w, mod_b.reshape(depth, 1, nd))


def _gate_fold_kernel(wl_ref, wu_ref, o_ref):
    o_ref[...] = _dot_f32(wl_ref[...], wu_ref[...])


def _gate_fold(w_low, w_up):
    d = w_low.shape[0]
    pad = 128 - GLA_GATE_RANK
    wl = jnp.pad(w_low, ((0, 0), (0, pad)))
    wu = jnp.pad(w_up, ((0, pad), (0, 0)))
    return pl.pallas_call(
        _gate_fold_kernel,
        out_shape=jax.ShapeDtypeStruct((d, GLA_DK), F32),
        name="gate_fold",
    )(wl, wu)


def _s5_disc_kernel(lre_ref, lim_ref, ldt_ref, bre_ref, bim_ref, lam_ref, bbre_ref, bbim_ref):
    lre = lre_ref[...]
    lim = lim_ref[...]
    dt = jnp.exp(ldt_ref[...])
    mag = jnp.exp(lre * dt)
    ang = lim * dt
    are = mag * jnp.cos(ang)
    aim = mag * jnp.sin(ang)
    lam_ref[0] = are
    lam_ref[1] = aim
    den = lre * lre + lim * lim
    nre = are - 1.0
    cre = (nre * lre + aim * lim) / den
    cim = (aim * lre - nre * lim) / den
    for h in range(S5_GROUP_CH):
        bre = bre_ref[h]
        bim = bim_ref[h]
        bbre_ref[h] = cre * bre - cim * bim
        bbim_ref[h] = cre * bim + cim * bre


def _s5_discretise(lam_re, lam_im, log_dt, b_re, b_im):
    g, p = lam_re.shape
    hch = b_re.shape[-1]
    bre_t = jnp.transpose(b_re, (2, 0, 1))
    bim_t = jnp.transpose(b_im, (2, 0, 1))
    lam, bbre, bbim = pl.pallas_call(
        _s5_disc_kernel,
        out_shape=(
            jax.ShapeDtypeStruct((2, g, p), F32),
            jax.ShapeDtypeStruct((hch, g, p), F32),
            jax.ShapeDtypeStruct((hch, g, p), F32),
        ),
        name="s5_discretise",
    )(lam_re, lam_im, log_dt.reshape(g, 1), bre_t, bim_t)
    return lam, bbre, bbim


def _s5_weights(lam_re, lam_im, log_dt, b_re, b_im, c_re, c_im):
    g, p, hch = S5_GROUPS, S5_STATE, S5_GROUP_CH
    lam, bbre, bbim = _s5_discretise(lam_re, lam_im, log_dt, b_re, b_im)
    lam_rows = lam.reshape(2, g * p)
    eye = jnp.eye(g, dtype=F32)

    def in_blocks(bb):
        dense = jnp.einsum('hgp,gk->ghkp', bb, eye).reshape(g * hch, g * p)
        blocks = [dense[128 * (n // 2):128 * (n // 2) + 128, 256 * n:256 * (n + 1)] for n in range(8)]
        return jnp.stack(blocks).astype(BF16)

    def out_blocks(cc, sign):
        dense = jnp.einsum('ghp,gk->gpkh', cc, eye).reshape(g * p, g * hch) * sign
        blocks = [dense[1024 * m:1024 * (m + 1), 256 * m:256 * (m + 1)] for m in range(2)]
        return jnp.stack(blocks).astype(BF16)

    return lam_rows, in_blocks(bbre), in_blocks(bbim), out_blocks(c_re, 1.0), out_blocks(c_im, -1.0)


IN_U = (0, 512)
IN_Q = (512, 768)
IN_K = (768, 1024)
IN_V = (1024, 1536)
IN_G = (1536, 2048)
IN_P = (2048, 2304)


def _inproj_kernel(x_ref, mod_ref, w_ref, u_ref, q_ref, k_ref, v_ref, g_ref, p_ref):
    sh = mod_ref[0, 0:1, :]
    sc = mod_ref[0, 1:2, :]
    h = (x_ref[0] * (1.0 + sc) + sh).astype(BF16)
    u_ref[...] = _dot(h, w_ref[:, IN_U[0]:IN_U[1]])
    q_ref[0] = _dot(h, w_ref[:, IN_Q[0]:IN_Q[1]])
    k_ref[0] = _dot(h, w_ref[:, IN_K[0]:IN_K[1]])
    v_ref[0] = _dot(h, w_ref[:, IN_V[0]:IN_V[1]])
    g_ref[0] = _dot(h, w_ref[:, IN_G[0]:IN_G[1]])
    p_ref[0] = _dot(h, w_ref[:, IN_P[0]:IN_P[1]])


def _in_projection(x, mod_l, w_cat, tm):
    bsz, seq, d = x.shape
    ncol = w_cat.shape[1]
    bspec = lambda w: pl.BlockSpec((1, tm, w), lambda b, t: (b, t, 0))
    return pl.pallas_call(
        _inproj_kernel,
        grid=(bsz, seq // tm),
        in_specs=[
            pl.BlockSpec((1, tm, d), lambda b, t: (b, t, 0)),
            pl.BlockSpec((1, N_MOD, d), lambda b, t: (b, 0, 0)),
            pl.BlockSpec((d, ncol), lambda b, t: (0, 0)),
        ],
        out_specs=[
            pl.BlockSpec((tm, S5_WIDTH), lambda b, t: (t, b)),
            bspec(GLA_DK), bspec(GLA_DK), bspec(GLA_WIDTH), bspec(GLA_WIDTH), bspec(GLA_DK),
        ],
        out_shape=[
            jax.ShapeDtypeStruct((seq, bsz * S5_WIDTH), F32),
            jax.ShapeDtypeStruct((bsz, seq, GLA_DK), F32),
            jax.ShapeDtypeStruct((bsz, seq, GLA_DK), F32),
            jax.ShapeDtypeStruct((bsz, seq, GLA_WIDTH), F32),
            jax.ShapeDtypeStruct((bsz, seq, GLA_WIDTH), F32),
            jax.ShapeDtypeStruct((bsz, seq, GLA_DK), F32),
        ],
        compiler_params=_cparams("parallel", "parallel"),
        name="in_projection",
    )(x, mod_l, w_cat)


S5_SCAN_LANES = 256


def _s5_kernel(u_ref, lam_ref, wbre_ref, wbim_ref, wcre_ref, wcim_ref, d_ref, wglu_ref, bglu_ref,
               o_ref, hre, him, st_re, st_im, *, tt, nb):
    @pl.when(pl.program_id(0) == 0)
    def _():
        st_re[...] = jnp.zeros_like(st_re)
        st_im[...] = jnp.zeros_like(st_im)

    rows = tt * nb
    u = u_ref[...].reshape(rows, S5_WIDTH)
    ub = u.astype(BF16)
    for n in range(8):
        lhs = ub[:, 128 * (n // 2):128 * (n // 2) + 128]
        hre[:, 256 * n:256 * (n + 1)] = _dot(lhs, wbre_ref[n])
        him[:, 256 * n:256 * (n + 1)] = _dot(lhs, wbim_ref[n])

    for c in range(S5_LANES // S5_SCAN_LANES):
        ls = slice(c * S5_SCAN_LANES, (c + 1) * S5_SCAN_LANES)
        lr = jnp.broadcast_to(lam_ref[0:1, ls], (nb, S5_SCAN_LANES))
        li = jnp.broadcast_to(lam_ref[1:2, ls], (nb, S5_SCAN_LANES))
        hr = st_re[:, ls]
        hi = st_im[:, ls]
        for t in range(tt):
            rs = slice(t * nb, (t + 1) * nb)
            nr = lr * hr - li * hi + hre[rs, ls]
            ni = lr * hi + li * hr + him[rs, ls]
            hre[rs, ls] = nr
            him[rs, ls] = ni
            hr, hi = nr, ni
        st_re[:, ls] = hr
        st_im[:, ls] = hi

    ys = []
    for m in range(2):
        ks = slice(1024 * m, 1024 * (m + 1))
        ys.append(_dot(hre[:, ks].astype(BF16), wcre_ref[m]) + _dot(him[:, ks].astype(BF16), wcim_ref[m]))
    y = jnp.concatenate(ys, axis=-1) + d_ref[...] * u
    y = 0.5 * y * (1.0 + jnp.tanh(math.sqrt(2.0 / math.pi) * (y + 0.044715 * (y * y * y))))
    z = y * jax.nn.sigmoid(_dot(y.astype(BF16), wglu_ref[...]) + bglu_ref[...])
    o_ref[...] = z.astype(BF16).reshape(tt, nb, S5_WIDTH)


def _s5_group(u_tm, lam_rows, wbre, wbim, wcre, wcim, d_skip, w_glu, b_glu, tt):
    seq, nb, width = u_tm.shape
    rows = tt * nb
    full = lambda a: pl.BlockSpec(a.shape, lambda t: (0,) * a.ndim)
    d2 = d_skip.reshape(1, width)
    b2 = b_glu.reshape(1, width)
    wg = w_glu.astype(BF16)
    return pl.pallas_call(
        functools.partial(_s5_kernel, tt=tt, nb=nb),
        grid=(seq // tt,),
        in_specs=[
            pl.BlockSpec((tt, nb, width), lambda t: (t, 0, 0)),
            full(lam_rows), full(wbre), full(wbim), full(wcre), full(wcim), full(d2), full(wg), full(b2),
        ],
        out_specs=pl.BlockSpec((tt, nb, width), lambda t: (t, 0, 0)),
        out_shape=jax.ShapeDtypeStruct((seq, nb, width), BF16),
        scratch_shapes=[
            pltpu.VMEM((rows, S5_LANES), F32),
            pltpu.VMEM((rows, S5_LANES), F32),
            pltpu.VMEM((nb, S5_LANES), F32),
            pltpu.VMEM((nb, S5_LANES), F32),
        ],
        compiler_params=_cparams("arbitrary"),
        name="s5_group",
    )(u_tm, lam_rows, wbre, wbim, wcre, wcim, d2, wg, b2)


def _gla_kernel(q_ref, k_ref, v_ref, g_ref, p_ref, ba_ref, gain_ref, o_ref, s_ref, *, nchunk):
    @pl.when(pl.program_id(1) == 0)
    def _():
        s_ref[...] = jnp.zeros_like(s_ref)

    c = GLA_CHUNK
    row = lax.broadcasted_iota(jnp.int32, (c, c), 0)
    col = lax.broadcasted_iota(jnp.int32, (c, c), 1)
    causal = row >= col
    tri = jnp.where(causal, 1.0, 0.0).astype(BF16)
    lane = lax.broadcasted_iota(jnp.int32, (1, 2 * GLA_HEAD_K), 1)
    head_lanes = (lane < GLA_HEAD_K, lane >= GLA_HEAD_K)
    srow = lax.broadcasted_iota(jnp.int32, (2 * GLA_HEAD_V, 2 * GLA_HEAD_K), 0)
    scol = lax.broadcasted_iota(jnp.int32, (2 * GLA_HEAD_V, 2 * GLA_HEAD_K), 1)
    same_head = (srow < GLA_HEAD_V) == (scol < GLA_HEAD_K)
    scale = GLA_HEAD_K ** -0.5
    mid = c // 2 - 1

    for ci in range(nchunk):
        sl = slice(ci * c, (ci + 1) * c)
        pre = p_ref[0, sl, :] + ba_ref[...]
        log_alpha = (jnp.minimum(pre, 0.0) - jnp.log1p(jnp.exp(-jnp.abs(pre)))) * (1.0 / GLA_GATE_TAU)
        a1, a2, a3 = _split3(log_alpha)
        b = _dot(tri, a3) + _dot(tri, a2) + _dot(tri, a1)
        b_mid = b[mid:mid + 1, :]
        b_last = b[c - 1:c, :]
        q = q_ref[0, sl, :] * scale
        k = k_ref[0, sl, :]
        q_in = (q * jnp.exp(b)).astype(BF16)
        q_e = q * jnp.exp(b - b_mid)
        k_e = (k * jnp.exp(b_mid - b)).astype(BF16)
        k_d = (k * jnp.exp(b_last - b)).astype(BF16)
        decay = jnp.exp(b_last)
        for pr in range(GLA_HEADS // 2):
            ks = slice(2 * GLA_HEAD_K * pr, 2 * GLA_HEAD_K * (pr + 1))
            vs = slice(2 * GLA_HEAD_V * pr, 2 * GLA_HEAD_V * (pr + 1))
            s_t = s_ref[pr]
            o_inter = lax.dot_general(q_in[:, ks], s_t.astype(BF16), (((1,), (1,)), ((), ())),
                                      preferred_element_type=F32)
            v_pair = v_ref[0, sl, vs].astype(BF16)
            q_pair = q_e[:, ks]
            for hh in range(2):
                q_h = jnp.where(head_lanes[hh], q_pair, 0.0).astype(BF16)
                scores = lax.dot_general(q_h, k_e[:, ks], (((1,), (1,)), ((), ())),
                                         preferred_element_type=F32)
                scores = jnp.where(causal, scores, 0.0).astype(BF16)
                hv = slice(GLA_HEAD_V * hh, GLA_HEAD_V * (hh + 1))
                o_h = o_inter[:, hv] + _dot(scores, v_pair[:, hv])
                os_ = slice(2 * GLA_HEAD_V * pr + GLA_HEAD_V * hh, 2 * GLA_HEAD_V * pr + GLA_HEAD_V * (hh + 1))
                ms = jnp.mean(o_h * o_h, axis=-1, keepdims=True)
                o_n = o_h * lax.rsqrt(ms + RMS_EPS) * gain_ref[:, os_]
                o_ref[0, sl, os_] = (o_n * _silu(g_ref[0, sl, os_])).astype(BF16)
            upd = lax.dot_general(v_pair, k_d[:, ks], (((0,), (0,)), ((), ())),
                                  preferred_element_type=F32)
            s_ref[pr] = decay[:, ks] * s_t + jnp.where(same_head, upd, 0.0)


def _gla_group(q, k, v, g_out, pre, b_alpha, head_gain, tg):
    bsz, seq, _ = q.shape
    blk = lambda w: pl.BlockSpec((1, tg, w), lambda b, t: (b, t, 0))
    ba = b_alpha.reshape(1, GLA_DK)
    gain = head_gain.reshape(1, GLA_WIDTH)
    return pl.pallas_call(
        functools.partial(_gla_kernel, nchunk=tg // GLA_CHUNK),
        grid=(bsz, seq // tg),
        in_specs=[
            blk(GLA_DK), blk(GLA_DK), blk(GLA_WIDTH), blk(GLA_WIDTH), blk(GLA_DK),
            pl.BlockSpec((1, GLA_DK), lambda b, t: (0, 0)),
            pl.BlockSpec((1, GLA_WIDTH), lambda b, t: (0, 0)),
        ],
        out_specs=blk(GLA_WIDTH),
        out_shape=jax.ShapeDtypeStruct((bsz, seq, GLA_WIDTH), BF16),
        scratch_shapes=[pltpu.VMEM((GLA_HEADS // 2, 2 * GLA_HEAD_V, 2 * GLA_HEAD_K), F32)],
        compiler_params=_cparams("parallel", "arbitrary"),
        name="gla_group",
    )(q, k, v, g_out, pre, ba, gain)


def _mix_out(ys_ref, yg_ref, x_ref, mod_ref, w_ref, lg_ref, lb_ref, alpha):
    y = _dot(ys_ref[...], w_ref[0:S5_WIDTH, :]) + _dot(yg_ref[0], w_ref[S5_WIDTH:, :])
    gt = mod_ref[0, 2:3, :]
    x1 = _layer_norm(alpha * x_ref[0] + (1.0 + gt) * y, lg_ref[...], lb_ref[...])
    h2 = x1 * (1.0 + mod_ref[0, 4:5, :]) + mod_ref[0, 3:4, :]
    return x1, h2


def _outproj_kernel(ys_ref, yg_ref, x_ref, mod_ref, w_ref, lg_ref, lb_ref, x1_ref, h2_ref, *, alpha):
    x1, h2 = _mix_out(ys_ref, yg_ref, x_ref, mod_ref, w_ref, lg_ref, lb_ref, alpha)
    x1_ref[0] = x1
    h2_ref[0] = h2.astype(BF16)


def _top2_combine(logits):
    lane = lax.broadcasted_iota(jnp.int32, logits.shape, 1).astype(F32)
    l0 = jnp.where(lane < N_EXPERTS, logits, NEG_BIG)
    m1 = jnp.max(l0, axis=-1, keepdims=True)
    i1 = jnp.min(jnp.where(l0 == m1, lane, float(ROUTER_LANES)), axis=-1, keepdims=True)
    sel1 = lane == i1
    l1 = jnp.where(sel1, NEG_BIG, l0)
    m2 = jnp.max(l1, axis=-1, keepdims=True)
    i2 = jnp.min(jnp.where(l1 == m2, lane, float(ROUTER_LANES)), axis=-1, keepdims=True)
    sel2 = lane == i2
    e2 = jnp.exp(m2 - m1)
    w1 = 1.0 / (1.0 + e2)
    w2 = e2 / (1.0 + e2)
    return jnp.where(sel1, w1, 0.0) + jnp.where(sel2, w2, 0.0)


def _outproj_router_kernel(ys_ref, yg_ref, x_ref, mod_ref, w_ref, lg_ref, lb_ref, wr_ref, br_ref,
                           x1_ref, h2_ref, comb_ref, *, alpha):
    x1, h2 = _mix_out(ys_ref, yg_ref, x_ref, mod_ref, w_ref, lg_ref, lb_ref, alpha)
    x1_ref[0] = x1
    h2_ref[0] = h2.astype(BF16)
    logits = _dot_f32x3(h2, wr_ref[...]) + br_ref[...]
    comb_ref[0] = _top2_combine(logits)


def _out_projection(y_s5_tm, y_gla, x, mod_l, w_out, ln_g, ln_b, alpha, tm, router=None):
    bsz, seq, d = x.shape
    row = lambda a: a.reshape(1, -1)
    in_specs = [
        pl.BlockSpec((tm, S5_WIDTH), lambda b, t: (t, b)),
        pl.BlockSpec((1, tm, GLA_WIDTH), lambda b, t: (b, t, 0)),
        pl.BlockSpec((1, tm, d), lambda b, t: (b, t, 0)),
        pl.BlockSpec((1, N_MOD, d), lambda b, t: (b, 0, 0)),
        pl.BlockSpec((d, d), lambda b, t: (0, 0)),
        pl.BlockSpec((1, d), lambda b, t: (0, 0)),
        pl.BlockSpec((1, d), lambda b, t: (0, 0)),
    ]
    out_specs = [
        pl.BlockSpec((1, tm, d), lambda b, t: (b, t, 0)),
        pl.BlockSpec((1, tm, d), lambda b, t: (b, t, 0)),
    ]
    out_shape = [jax.ShapeDtypeStruct((bsz, seq, d), F32), jax.ShapeDtypeStruct((bsz, seq, d), BF16)]
    args = [y_s5_tm, y_gla, x, mod_l, w_out.astype(BF16), row(ln_g), row(ln_b)]
    if router is None:
        body = functools.partial(_outproj_kernel, alpha=alpha)
    else:
        w_router, b_router = router
        pad = ROUTER_LANES - N_EXPERTS
        in_specs += [pl.BlockSpec((d, ROUTER_LANES), lambda b, t: (0, 0)),
                     pl.BlockSpec((1, ROUTER_LANES), lambda b, t: (0, 0))]
        out_specs.append(pl.BlockSpec((1, tm, ROUTER_LANES), lambda b, t: (b, t, 0)))
        out_shape.append(jax.ShapeDtypeStruct((bsz, seq, ROUTER_LANES), F32))
        args += [jnp.pad(w_router, ((0, 0), (0, pad))), jnp.pad(row(b_router), ((0, 0), (0, pad)))]
        body = functools.partial(_outproj_router_kernel, alpha=alpha)
    return pl.pallas_call(
        body,
        grid=(bsz, seq // tm),
        in_specs=in_specs,
        out_specs=out_specs,
        out_shape=out_shape,
        compiler_params=_cparams("parallel", "parallel"),
        name="out_projection",
    )(*args)


def _ffn_kernel(h_ref, x_ref, mod_ref, wg_ref, wu_ref, wd_ref, lg_ref, lb_ref, o_ref, acc_ref, *, alpha):
    j = pl.program_id(1)

    @pl.when(j == 0)
    def _():
        acc_ref[...] = jnp.zeros_like(acc_ref)

    h = h_ref[...]
    a = _dot(h, wg_ref[...])
    mid = (_silu(a) * _dot(h, wu_ref[...])).astype(BF16)
    acc_ref[...] += _dot(mid, wd_ref[...])

    @pl.when(j == pl.num_programs(1) - 1)
    def _():
        gt = mod_ref[0, 5:6, :]
        o_ref[...] = _layer_norm(alpha * x_ref[...] + (1.0 + gt) * acc_ref[...], lg_ref[...], lb_ref[...])


def _dense_ffn(h2, x1, mod_l, w_gate, w_up, w_down, ln_g, ln_b, alpha, tm, tf):
    bsz, seq, d = x1.shape
    n = bsz * seq
    dff = w_gate.shape[1]
    per_b = seq // tm
    return pl.pallas_call(
        functools.partial(_ffn_kernel, alpha=alpha),
        grid=(n // tm, dff // tf),
        in_specs=[
            pl.BlockSpec((tm, d), lambda i, j: (i, 0)),
            pl.BlockSpec((tm, d), lambda i, j: (i, 0)),
            pl.BlockSpec((1, N_MOD, d), lambda i, j: (i // per_b, 0, 0)),
            pl.BlockSpec((d, tf), lambda i, j: (0, j)),
            pl.BlockSpec((d, tf), lambda i, j: (0, j)),
            pl.BlockSpec((tf, d), lambda i, j: (j, 0)),
            pl.BlockSpec((1, d), lambda i, j: (0, 0)),
            pl.BlockSpec((1, d), lambda i, j: (0, 0)),
        ],
        out_specs=pl.BlockSpec((tm, d), lambda i, j: (i, 0)),
        out_shape=jax.ShapeDtypeStruct((n, d), F32),
        scratch_shapes=[pltpu.VMEM((tm, d), F32)],
        compiler_params=_cparams("parallel", "arbitrary"),
        name="dense_ffn",
    )(h2.reshape(n, d), x1.reshape(n, d), mod_l, w_gate.astype(BF16), w_up.astype(BF16), w_down.astype(BF16),
      ln_g.reshape(1, d), ln_b.reshape(1, d)).reshape(bsz, seq, d)


def _moe_kernel(h_ref, x_ref, comb_ref, mod_ref, wg_ref, wu_ref, wd_ref, lg_ref, lb_ref, o_ref, acc_ref, cw_ref,
                *, alpha):
    e = pl.program_id(1)
    j = pl.program_id(2)

    @pl.when((e == 0) & (j == 0))
    def _():
        acc_ref[...] = jnp.zeros_like(acc_ref)

    @pl.when(j == 0)
    def _():
        lane = lax.broadcasted_iota(jnp.int32, comb_ref.shape, 1)
        cw_ref[...] = jnp.sum(jnp.where(lane == e, comb_ref[...], 0.0), axis=-1, keepdims=True)

    h = h_ref[...]
    a = _dot(h, wg_ref[0])
    mid = (_silu(a) * _dot(h, wu_ref[0])).astype(BF16)
    acc_ref[...] += cw_ref[...] * _dot(mid, wd_ref[0])

    @pl.when((e == pl.num_programs(1) - 1) & (j == pl.num_programs(2) - 1))
    def _():
        gt = mod_ref[0, 5:6, :]
        o_ref[...] = _layer_norm(alpha * x_ref[...] + (1.0 + gt) * acc_ref[...], lg_ref[...], lb_ref[...])


def _moe_ffn(h2, x1, comb, mod_l, w_gate, w_up, w_down, ln_g, ln_b, alpha, tm, tf):
    bsz, seq, d = x1.shape
    n = bsz * seq
    ne, _, dff = w_gate.shape
    per_b = seq // tm
    return pl.pallas_call(
        functools.partial(_moe_kernel, alpha=alpha),
        grid=(n // tm, ne, dff // tf),
        in_specs=[
            pl.BlockSpec((tm, d), lambda i, e, j: (i, 0)),
            pl.BlockSpec((tm, d), lambda i, e, j: (i, 0)),
            pl.BlockSpec((tm, ROUTER_LANES), lambda i, e, j: (i, 0)),
            pl.BlockSpec((1, N_MOD, d), lambda i, e, j: (i // per_b, 0, 0)),
            pl.BlockSpec((1, d, tf), lambda i, e, j: (e, 0, j)),
            pl.BlockSpec((1, d, tf), lambda i, e, j: (e, 0, j)),
            pl.BlockSpec((1, tf, d), lambda i, e, j: (e, j, 0)),
            pl.BlockSpec((1, d), lambda i, e, j: (0, 0)),
            pl.BlockSpec((1, d), lambda i, e, j: (0, 0)),
        ],
        out_specs=pl.BlockSpec((tm, d), lambda i, e, j: (i, 0)),
        out_shape=jax.ShapeDtypeStruct((n, d), F32),
        scratch_shapes=[pltpu.VMEM((tm, d), F32), pltpu.VMEM((tm, 1), F32)],
        compiler_params=_cparams("parallel", "arbitrary", "arbitrary"),
        name="moe_ffn",
    )(h2.reshape(n, d), x1.reshape(n, d), comb.reshape(n, ROUTER_LANES), mod_l,
      w_gate.astype(BF16), w_up.astype(BF16), w_down.astype(BF16),
      ln_g.reshape(1, d), ln_b.reshape(1, d)).reshape(bsz, seq, d)


def _tile(n, want):
    t = min(n, want)
    assert n % t == 0, (n, want)
    return t


def kernel(x, c, mod_w, mod_b, w_in, w_out, s5_lam_re, s5_lam_im, s5_log_dt, s5_b_re, s5_b_im, s5_c_re, s5_c_im, s5_d, s5_w_glu, s5_b_glu, gla_w_alpha_up, gla_b_alpha, gla_head_gain, ln_mix_g, ln_mix_b, ffn_w_gate, ffn_w_up, ffn_w_down, moe_w_router, moe_b_router, moe_w_gate, moe_w_up, moe_w_down, ln_ffn_g, ln_ffn_b):
    bsz, seq, d = x.shape
    depth = mod_w.shape[0]
    alpha = (2.0 * depth) ** 0.25
    tm = _tile(seq, 512)
    tt = _tile(seq, 16)
    tg = _tile(seq, 256)
    tm_ffn = _tile(seq, 1024)
    tf = _tile(D_FF, 512)

    mod = _modulation(c, mod_w, mod_b).reshape(depth, bsz, N_MOD, d)
    for layer in range(depth):
        mod_l = mod[layer]
        w_gate_fold = _gate_fold(w_in[layer][:, IN_P[0]:], gla_w_alpha_up[layer])
        w_cat = jnp.concatenate([w_in[layer][:, :IN_P[0]], w_gate_fold], axis=1).astype(BF16)
        u_tm, q, k, v, g_out, pre = _in_projection(x, mod_l, w_cat, tm)

        s5w = _s5_weights(s5_lam_re[layer], s5_lam_im[layer], s5_log_dt[layer], s5_b_re[layer], s5_b_im[layer],
                          s5_c_re[layer], s5_c_im[layer])
        y_s5 = _s5_group(u_tm.reshape(seq, bsz, S5_WIDTH), *s5w, s5_d[layer], s5_w_glu[layer], s5_b_glu[layer], tt)
        y_s5 = y_s5.reshape(seq, bsz * S5_WIDTH)
        y_gla = _gla_group(q, k, v, g_out, pre, gla_b_alpha[layer], gla_head_gain[layer], tg)

        i = layer // 2
        if layer % 2 == 0:
            x1, h2 = _out_projection(y_s5, y_gla, x, mod_l, w_out[layer], ln_mix_g[layer], ln_mix_b[layer], alpha, tm)
            x = _dense_ffn(h2, x1, mod_l, ffn_w_gate[i], ffn_w_up[i], ffn_w_down[i],
                           ln_ffn_g[layer], ln_ffn_b[layer], alpha, tm_ffn, tf)
        else:
            x1, h2, comb = _out_projection(y_s5, y_gla, x, mod_l, w_out[layer], ln_mix_g[layer], ln_mix_b[layer],
                                           alpha, tm, router=(moe_w_router[i], moe_b_router[i]))
            x = _moe_ffn(h2, x1, comb, mod_l, moe_w_gate[i], moe_w_up[i], moe_w_down[i],
                         ln_ffn_g[layer], ln_ffn_b[layer], alpha, tm_ffn, tf)
    return x
```

```python
import functools
import math

import jax
import jax.numpy as jnp
from jax import lax
from jax.experimental import pallas as pl
from jax.experimental.pallas import tpu as pltpu

F32 = jnp.float32
BF16 = jnp.bfloat16

D_MODEL = 1024
S5_WIDTH = 512
S5_GROUP_CH = 16
S5_GROUPS = 32
S5_STATE = 64
S5_LANES = S5_GROUPS * S5_STATE
GLA_WIDTH = 512
GLA_HEADS = 4
GLA_DK = 256
GLA_HEAD_K = 64
GLA_HEAD_V = 128
GLA_GATE_RANK = 16
GLA_GATE_TAU = 16.0
GLA_CHUNK = 64
D_FF = 3584
N_EXPERTS = 8
N_MOD = 6
LN_EPS = 1e-5
RMS_EPS = 1e-6
ROUTER_LANES = 128
NEG_BIG = -1e30

VMEM_LIMIT = 56 * 1024 * 1024


def _cparams(*sem):
    return pltpu.CompilerParams(dimension_semantics=sem, vmem_limit_bytes=VMEM_LIMIT)


def _dot(a, b):
    return jnp.dot(a, b, preferred_element_type=F32)


def _split2(a):
    a1 = a.astype(BF16)
    a2 = (a - a1.astype(F32)).astype(BF16)
    return a1, a2


def _split3(a):
    a1 = a.astype(BF16)
    r1 = a - a1.astype(F32)
    a2 = r1.astype(BF16)
    a3 = (r1 - a2.astype(F32)).astype(BF16)
    return a1, a2, a3


def _dot_f32(a, b):
    a1, a2, a3 = _split3(a)
    b1, b2, b3 = _split3(b)
    lo = _dot(a1, b3) + _dot(a2, b2) + _dot(a3, b1)
    mid = _dot(a1, b2) + _dot(a2, b1)
    return lo + mid + _dot(a1, b1)


def _dot_f32x3(a, b):
    a1, a2 = _split2(a)
    b1, b2 = _split2(b)
    return (_dot(a1, b2) + _dot(a2, b1)) + _dot(a1, b1)


def _silu(x):
    return x * jax.nn.sigmoid(x)


def _layer_norm(r, gain, bias):
    mu = jnp.mean(r, axis=-1, keepdims=True)
    d = r - mu
    var = jnp.mean(d * d, axis=-1, keepdims=True)
    return d * lax.rsqrt(var + LN_EPS) * gain + bias


def _mod_kernel(c_ref, w_ref, b_ref, o_ref):
    o_ref[0] = _dot_f32(_silu(c_ref[...]), w_ref[0]) + b_ref[0]


def _modulation(c, mod_w, mod_b):
    depth, d, nd = mod_w.shape
    bsz = c.shape[0]
    return pl.pallas_call(
        _mod_kernel,
        grid=(depth, nd // d),
        in_specs=[
            pl.BlockSpec((bsz, d), lambda l, n: (0, 0)),
            pl.BlockSpec((1, d, d), lambda l, n: (l, 0, n)),
            pl.BlockSpec((1, 1, d), lambda l, n: (l, 0, n)),
        ],
        out_specs=pl.BlockSpec((1, bsz, d), lambda l, n: (l, 0, n)),
        out_shape=jax.ShapeDtypeStruct((depth, bsz, nd), F32),
        compiler_params=_cparams("parallel", "parallel"),
        name="modulation",
    )(c, mod_w, mod_b.reshape(depth, 1, nd))


def _gate_fold_kernel(wl_ref, wu_ref, o_ref):
    o_ref[...] = _dot_f32(wl_ref[...], wu_ref[...])


def _gate_fold(w_low, w_up):
    d = w_low.shape[0]
    pad = 128 - GLA_GATE_RANK
    wl = jnp.pad(w_low, ((0, 0), (0, pad)))
    wu = jnp.pad(w_up, ((0, pad), (0, 0)))
    return pl.pallas_call(
        _gate_fold_kernel,
        out_shape=jax.ShapeDtypeStruct((d, GLA_DK), F32),
        name="gate_fold",
    )(wl, wu)


def _s5_disc_kernel(lre_ref, lim_ref, ldt_ref, bre_ref, bim_ref, lam_ref, bbre_ref, bbim_ref):
    lre = lre_ref[...]
    lim = lim_ref[...]
    dt = jnp.exp(ldt_ref[...])
    mag = jnp.exp(lre * dt)
    ang = lim * dt
    are = mag * jnp.cos(ang)
    aim = mag * jnp.sin(ang)
    lam_ref[0] = are
    lam_ref[1] = aim
    den = lre * lre + lim * lim
    nre = are - 1.0
    cre = (nre * lre + aim * lim) / den
    cim = (aim * lre - nre * lim) / den
    for h in range(S5_GROUP_CH):
        bre = bre_ref[h]
        bim = bim_ref[h]
        bbre_ref[h] = cre * bre - cim * bim
        bbim_ref[h] = cre * bim + cim * bre


def _s5_discretise(lam_re, lam_im, log_dt, b_re, b_im):
    g, p = lam_re.shape
    hch = b_re.shape[-1]
    bre_t = jnp.transpose(b_re, (2, 0, 1))
    bim_t = jnp.transpose(b_im, (2, 0, 1))
    lam, bbre, bbim = pl.pallas_call(
        _s5_disc_kernel,
        out_shape=(
            jax.ShapeDtypeStruct((2, g, p), F32),
            jax.ShapeDtypeStruct((hch, g, p), F32),
            jax.ShapeDtypeStruct((hch, g, p), F32),
        ),
        name="s5_discretise",
    )(lam_re, lam_im, log_dt.reshape(g, 1), bre_t, bim_t)
    return lam, bbre, bbim


def _s5_weights(lam_re, lam_im, log_dt, b_re, b_im, c_re, c_im):
    g, p, hch = S5_GROUPS, S5_STATE, S5_GROUP_CH
    lam, bbre, bbim = _s5_discretise(lam_re, lam_im, log_dt, b_re, b_im)
    lam_rows = lam.reshape(2, g * p)
    eye = jnp.eye(g, dtype=F32)

    def in_blocks(bb):
        dense = jnp.einsum('hgp,gk->ghkp', bb, eye).reshape(g * hch, g * p)
        blocks = [dense[128 * (n // 2):128 * (n // 2) + 128, 256 * n:256 * (n + 1)] for n in range(8)]
        return jnp.stack(blocks).astype(BF16)

    def out_blocks(cc, sign):
        dense = jnp.einsum('ghp,gk->gpkh', cc, eye).reshape(g * p, g * hch) * sign
        blocks = [dense[1024 * m:1024 * (m + 1), 256 * m:256 * (m + 1)] for m in range(2)]
        return jnp.stack(blocks).astype(BF16)

    return lam_rows, in_blocks(bbre), in_blocks(bbim), out_blocks(c_re, 1.0), out_blocks(c_im, -1.0)


IN_U = (0, 512)
IN_Q = (512, 768)
IN_K = (768, 1024)
IN_V = (1024, 1536)
IN_G = (1536, 2048)
IN_P = (2048, 2304)


def _inproj_kernel(x_ref, mod_ref, w_ref, u_ref, q_ref, k_ref, v_ref, g_ref, p_ref):
    sh = mod_ref[0, 0:1, :]
    sc = mod_ref[0, 1:2, :]
    h = (x_ref[0] * (1.0 + sc) + sh).astype(BF16)
    u_ref[...] = _dot(h, w_ref[:, IN_U[0]:IN_U[1]])
    q_ref[0] = _dot(h, w_ref[:, IN_Q[0]:IN_Q[1]])
    k_ref[0] = _dot(h, w_ref[:, IN_K[0]:IN_K[1]])
    v_ref[0] = _dot(h, w_ref[:, IN_V[0]:IN_V[1]])
    g_ref[0] = _dot(h, w_ref[:, IN_G[0]:IN_G[1]])
    p_ref[0] = _dot(h, w_ref[:, IN_P[0]:IN_P[1]])


def _in_projection(x, mod_l, w_cat, tm):
    bsz, seq, d = x.shape
    ncol = w_cat.shape[1]
    bspec = lambda w: pl.BlockSpec((1, tm, w), lambda b, t: (b, t, 0))
    return pl.pallas_call(
        _inproj_kernel,
        grid=(bsz, seq // tm),
        in_specs=[
            pl.BlockSpec((1, tm, d), lambda b, t: (b, t, 0)),
            pl.BlockSpec((1, N_MOD, d), lambda b, t: (b, 0, 0)),
            pl.BlockSpec((d, ncol), lambda b, t: (0, 0)),
        ],
        out_specs=[
            pl.BlockSpec((tm, S5_WIDTH), lambda b, t: (t, b)),
            bspec(GLA_DK), bspec(GLA_DK), bspec(GLA_WIDTH), bspec(GLA_WIDTH), bspec(GLA_DK),
        ],
        out_shape=[
            jax.ShapeDtypeStruct((seq, bsz * S5_WIDTH), F32),
            jax.ShapeDtypeStruct((bsz, seq, GLA_DK), F32),
            jax.ShapeDtypeStruct((bsz, seq, GLA_DK), F32),
            jax.ShapeDtypeStruct((bsz, seq, GLA_WIDTH), F32),
            jax.ShapeDtypeStruct((bsz, seq, GLA_WIDTH), F32),
            jax.ShapeDtypeStruct((bsz, seq, GLA_DK), F32),
        ],
        compiler_params=_cparams("parallel", "parallel"),
        name="in_projection",
    )(x, mod_l, w_cat)


S5_SCAN_LANES = 256


def _s5_kernel(u_ref, lam_ref, wbre_ref, wbim_ref, wcre_ref, wcim_ref, d_ref, wglu_ref, bglu_ref,
               o_ref, hre, him, st_re, st_im, *, tt, nb):
    @pl.when(pl.program_id(0) == 0)
    def _():
        st_re[...] = jnp.zeros_like(st_re)
        st_im[...] = jnp.zeros_like(st_im)

    rows = tt * nb
    u = u_ref[...].reshape(rows, S5_WIDTH)
    ub = u.astype(BF16)
    for n in range(8):
        lhs = ub[:, 128 * (n // 2):128 * (n // 2) + 128]
        hre[:, 256 * n:256 * (n + 1)] = _dot(lhs, wbre_ref[n])
        him[:, 256 * n:256 * (n + 1)] = _dot(lhs, wbim_ref[n])

    for c in range(S5_LANES // S5_SCAN_LANES):
        ls = slice(c * S5_SCAN_LANES, (c + 1) * S5_SCAN_LANES)
        lr = jnp.broadcast_to(lam_ref[0:1, ls], (nb, S5_SCAN_LANES))
        li = jnp.broadcast_to(lam_ref[1:2, ls], (nb, S5_SCAN_LANES))
        hr = st_re[:, ls]
        hi = st_im[:, ls]
        for t in range(tt):
            rs = slice(t * nb, (t + 1) * nb)
            nr = lr * hr - li * hi + hre[rs, ls]
            ni = lr * hi + li * hr + him[rs, ls]
            hre[rs, ls] = nr
            him[rs, ls] = ni
            hr, hi = nr, ni
        st_re[:, ls] = hr
        st_im[:, ls] = hi

    ys = []
    for m in range(2):
        ks = slice(1024 * m, 1024 * (m + 1))
        ys.append(_dot(hre[:, ks].astype(BF16), wcre_ref[m]) + _dot(him[:, ks].astype(BF16), wcim_ref[m]))
    y = jnp.concatenate(ys, axis=-1) + d_ref[...] * u
    y = 0.5 * y * (1.0 + jnp.tanh(math.sqrt(2.0 / math.pi) * (y + 0.044715 * (y * y * y))))
    z = y * jax.nn.sigmoid(_dot(y.astype(BF16), wglu_ref[...]) + bglu_ref[...])
    o_ref[...] = z.astype(BF16).reshape(tt, nb, S5_WIDTH)


def _s5_group(u_tm, lam_rows, wbre, wbim, wcre, wcim, d_skip, w_glu, b_glu, tt):
    seq, nb, width = u_tm.shape
    rows = tt * nb
    full = lambda a: pl.BlockSpec(a.shape, lambda t: (0,) * a.ndim)
    d2 = d_skip.reshape(1, width)
    b2 = b_glu.reshape(1, width)
    wg = w_glu.astype(BF16)
    return pl.pallas_call(
        functools.partial(_s5_kernel, tt=tt, nb=nb),
        grid=(seq // tt,),
        in_specs=[
            pl.BlockSpec((tt, nb, width), lambda t: (t, 0, 0)),
            full(lam_rows), full(wbre), full(wbim), full(wcre), full(wcim), full(d2), full(wg), full(b2),
        ],
        out_specs=pl.BlockSpec((tt, nb, width), lambda t: (t, 0, 0)),
        out_shape=jax.ShapeDtypeStruct((seq, nb, width), BF16),
        scratch_shapes=[
            pltpu.VMEM((rows, S5_LANES), F32),
            pltpu.VMEM((rows, S5_LANES), F32),
            pltpu.VMEM((nb, S5_LANES), F32),
            pltpu.VMEM((nb, S5_LANES), F32),
        ],
        compiler_params=_cparams("arbitrary"),
        name="s5_group",
    )(u_tm, lam_rows, wbre, wbim, wcre, wcim, d2, wg, b2)


def _gla_kernel(q_ref, k_ref, v_ref, g_ref, p_ref, ba_ref, gain_ref, o_ref, s_ref, *, nchunk):
    @pl.when(pl.program_id(1) == 0)
    def _():
        s_ref[...] = jnp.zeros_like(s_ref)

    c = GLA_CHUNK
    row = lax.broadcasted_iota(jnp.int32, (c, c), 0)
    col = lax.broadcasted_iota(jnp.int32, (c, c), 1)
    causal = row >= col
    tri = jnp.where(causal, 1.0, 0.0).astype(BF16)
    lane = lax.broadcasted_iota(jnp.int32, (1, 2 * GLA_HEAD_K), 1)
    head_lanes = (lane < GLA_HEAD_K, lane >= GLA_HEAD_K)
    srow = lax.broadcasted_iota(jnp.int32, (2 * GLA_HEAD_V, 2 * GLA_HEAD_K), 0)
    scol = lax.broadcasted_iota(jnp.int32, (2 * GLA_HEAD_V, 2 * GLA_HEAD_K), 1)
    same_head = (srow < GLA_HEAD_V) == (scol < GLA_HEAD_K)
    scale = GLA_HEAD_K ** -0.5
    mid = c // 2 - 1

    for ci in range(nchunk):
        sl = slice(ci * c, (ci + 1) * c)
        pre = p_ref[0, sl, :] + ba_ref[...]
        log_alpha = (jnp.minimum(pre, 0.0) - jnp.log1p(jnp.exp(-jnp.abs(pre)))) * (1.0 / GLA_GATE_TAU)
        a1, a2, a3 = _split3(log_alpha)
        b = _dot(tri, a3) + _dot(tri, a2) + _dot(tri, a1)
        b_mid = b[mid:mid + 1, :]
        b_last = b[c - 1:c, :]
        q = q_ref[0, sl, :] * scale
        k = k_ref[0, sl, :]
        q_in = (q * jnp.exp(b)).astype(BF16)
        q_e = q * jnp.exp(b - b_mid)
        k_e = (k * jnp.exp(b_mid - b)).astype(BF16)
        k_d = (k * jnp.exp(b_last - b)).astype(BF16)
        decay = jnp.exp(b_last)
        for pr in range(GLA_HEADS // 2):
            ks = slice(2 * GLA_HEAD_K * pr, 2 * GLA_HEAD_K * (pr + 1))
            vs = slice(2 * GLA_HEAD_V * pr, 2 * GLA_HEAD_V * (pr + 1))
            s_t = s_ref[pr]
            o_inter = lax.dot_general(q_in[:, ks], s_t.astype(BF16), (((1,), (1,)), ((), ())),
                                      preferred_element_type=F32)
            v_pair = v_ref[0, sl, vs].astype(BF16)
            q_pair = q_e[:, ks]
            for hh in range(2):
                q_h = jnp.where(head_lanes[hh], q_pair, 0.0).astype(BF16)
                scores = lax.dot_general(q_h, k_e[:, ks], (((1,), (1,)), ((), ())),
                                         preferred_element_type=F32)
                scores = jnp.where(causal, scores, 0.0).astype(BF16)
                hv = slice(GLA_HEAD_V * hh, GLA_HEAD_V * (hh + 1))
                o_h = o_inter[:, hv] + _dot(scores, v_pair[:, hv])
                os_ = slice(2 * GLA_HEAD_V * pr + GLA_HEAD_V * hh, 2 * GLA_HEAD_V * pr + GLA_HEAD_V * (hh + 1))
                ms = jnp.mean(o_h * o_h, axis=-1, keepdims=True)
                o_n = o_h * lax.rsqrt(ms + RMS_EPS) * gain_ref[:, os_]
                o_ref[0, sl, os_] = (o_n * _silu(g_ref[0, sl, os_])).astype(BF16)
            upd = lax.dot_general(v_pair, k_d[:, ks], (((0,), (0,)), ((), ())),
                                  preferred_element_type=F32)
            s_ref[pr] = decay[:, ks] * s_t + jnp.where(same_head, upd, 0.0)


def _gla_group(q, k, v, g_out, pre, b_alpha, head_gain, tg):
    bsz, seq, _ = q.shape
    blk = lambda w: pl.BlockSpec((1, tg, w), lambda b, t: (b, t, 0))
    ba = b_alpha.reshape(1, GLA_DK)
    gain = head_gain.reshape(1, GLA_WIDTH)
    return pl.pallas_call(
        functools.partial(_gla_kernel, nchunk=tg // GLA_CHUNK),
        grid=(bsz, seq // tg),
        in_specs=[
            blk(GLA_DK), blk(GLA_DK), blk(GLA_WIDTH), blk(GLA_WIDTH), blk(GLA_DK),
            pl.BlockSpec((1, GLA_DK), lambda b, t: (0, 0)),
            pl.BlockSpec((1, GLA_WIDTH), lambda b, t: (0, 0)),
        ],
        out_specs=blk(GLA_WIDTH),
        out_shape=jax.ShapeDtypeStruct((bsz, seq, GLA_WIDTH), BF16),
        scratch_shapes=[pltpu.VMEM((GLA_HEADS // 2, 2 * GLA_HEAD_V, 2 * GLA_HEAD_K), F32)],
        compiler_params=_cparams("parallel", "arbitrary"),
        name="gla_group",
    )(q, k, v, g_out, pre, ba, gain)


def _mix_out(ys_ref, yg_ref, x_ref, mod_ref, w_ref, lg_ref, lb_ref, alpha):
    y = _dot(ys_ref[...], w_ref[0:S5_WIDTH, :]) + _dot(yg_ref[0], w_ref[S5_WIDTH:, :])
    gt = mod_ref[0, 2:3, :]
    x1 = _layer_norm(alpha * x_ref[0] + (1.0 + gt) * y, lg_ref[...], lb_ref[...])
    h2 = x1 * (1.0 + mod_ref[0, 4:5, :]) + mod_ref[0, 3:4, :]
    return x1, h2


def _outproj_kernel(ys_ref, yg_ref, x_ref, mod_ref, w_ref, lg_ref, lb_ref, x1_ref, h2_ref, *, alpha):
    x1, h2 = _mix_out(ys_ref, yg_ref, x_ref, mod_ref, w_ref, lg_ref, lb_ref, alpha)
    x1_ref[0] = x1
    h2_ref[0] = h2.astype(BF16)


RT_E1, RT_E2, RT_R1, RT_R2, RT_W1, RT_W2 = range(6)


def _top2_route(logits, count):
    tm = logits.shape[0]
    lane = lax.broadcasted_iota(jnp.int32, logits.shape, 1).astype(F32)
    l0 = jnp.where(lane < N_EXPERTS, logits, NEG_BIG)
    m1 = jnp.max(l0, axis=-1, keepdims=True)
    i1 = jnp.min(jnp.where(l0 == m1, lane, float(ROUTER_LANES)), axis=-1, keepdims=True)
    sel1 = lane == i1
    l1 = jnp.where(sel1, NEG_BIG, l0)
    m2 = jnp.max(l1, axis=-1, keepdims=True)
    i2 = jnp.min(jnp.where(l1 == m2, lane, float(ROUTER_LANES)), axis=-1, keepdims=True)
    sel2 = lane == i2
    e2 = jnp.exp(m2 - m1)
    w1 = 1.0 / (1.0 + e2)
    w2 = e2 / (1.0 + e2)
    chosen = jnp.where(sel1, 1.0, 0.0) + jnp.where(sel2, 1.0, 0.0)
    row = lax.broadcasted_iota(jnp.int32, (tm, tm), 0)
    col = lax.broadcasted_iota(jnp.int32, (tm, tm), 1)
    earlier = jnp.where(row > col, 1.0, 0.0).astype(BF16)
    before = _dot(earlier, chosen.astype(BF16)) + count
    r1 = jnp.sum(jnp.where(sel1, before, 0.0), axis=-1, keepdims=True)
    r2 = jnp.sum(jnp.where(sel2, before, 0.0), axis=-1, keepdims=True)
    rec = jnp.zeros_like(logits)
    for ln, val in ((RT_E1, i1), (RT_E2, i2), (RT_R1, r1), (RT_R2, r2), (RT_W1, w1), (RT_W2, w2)):
        rec = jnp.where(lane == float(ln), val, rec)
    return rec, count + jnp.sum(chosen, axis=0, keepdims=True)


def _outproj_router_kernel(ys_ref, yg_ref, x_ref, mod_ref, w_ref, lg_ref, lb_ref, wr_ref, br_ref,
                           x1_ref, h2_ref, route_ref, count_ref, cnt, *, alpha):
    @pl.when((pl.program_id(0) == 0) & (pl.program_id(1) == 0))
    def _():
        cnt[...] = jnp.zeros_like(cnt)

    x1, h2 = _mix_out(ys_ref, yg_ref, x_ref, mod_ref, w_ref, lg_ref, lb_ref, alpha)
    x1_ref[0] = x1
    h2_ref[0] = h2
    logits = _dot_f32x3(h2, wr_ref[...]) + br_ref[...]
    rec, new_count = _top2_route(logits, cnt[...])
    route_ref[0] = rec
    cnt[...] = new_count
    count_ref[...] = jnp.broadcast_to(new_count, count_ref.shape)


def _out_projection(y_s5_tm, y_gla, x, mod_l, w_out, ln_g, ln_b, alpha, tm, router=None):
    bsz, seq, d = x.shape
    row = lambda a: a.reshape(1, -1)
    in_specs = [
        pl.BlockSpec((tm, S5_WIDTH), lambda b, t: (t, b)),
        pl.BlockSpec((1, tm, GLA_WIDTH), lambda b, t: (b, t, 0)),
        pl.BlockSpec((1, tm, d), lambda b, t: (b, t, 0)),
        pl.BlockSpec((1, N_MOD, d), lambda b, t: (b, 0, 0)),
        pl.BlockSpec((d, d), lambda b, t: (0, 0)),
        pl.BlockSpec((1, d), lambda b, t: (0, 0)),
        pl.BlockSpec((1, d), lambda b, t: (0, 0)),
    ]
    out_specs = [
        pl.BlockSpec((1, tm, d), lambda b, t: (b, t, 0)),
        pl.BlockSpec((1, tm, d), lambda b, t: (b, t, 0)),
    ]
    out_shape = [jax.ShapeDtypeStruct((bsz, seq, d), F32), jax.ShapeDtypeStruct((bsz, seq, d), BF16)]
    args = [y_s5_tm, y_gla, x, mod_l, w_out.astype(BF16), row(ln_g), row(ln_b)]
    scratch = []
    sem = ("parallel", "parallel")
    if router is None:
        body = functools.partial(_outproj_kernel, alpha=alpha)
    else:
        w_router, b_router = router
        pad = ROUTER_LANES - N_EXPERTS
        in_specs += [pl.BlockSpec((d, ROUTER_LANES), lambda b, t: (0, 0)),
                     pl.BlockSpec((1, ROUTER_LANES), lambda b, t: (0, 0))]
        out_specs += [pl.BlockSpec((1, tm, ROUTER_LANES), lambda b, t: (b, t, 0)),
                      pl.BlockSpec((8, ROUTER_LANES), lambda b, t: (0, 0))]
        out_shape[1] = jax.ShapeDtypeStruct((bsz, seq, d), F32)
        out_shape += [jax.ShapeDtypeStruct((bsz, seq, ROUTER_LANES), F32),
                      jax.ShapeDtypeStruct((8, ROUTER_LANES), F32)]
        args += [jnp.pad(w_router, ((0, 0), (0, pad))), jnp.pad(row(b_router), ((0, 0), (0, pad)))]
        body = functools.partial(_outproj_router_kernel, alpha=alpha)
        scratch = [pltpu.VMEM((1, ROUTER_LANES), F32)]
        sem = ("arbitrary", "arbitrary")
    return pl.pallas_call(
        body,
        grid=(bsz, seq // tm),
        in_specs=in_specs,
        out_specs=out_specs,
        out_shape=out_shape,
        scratch_shapes=scratch,
        compiler_params=_cparams(*sem),
        name="out_projection",
    )(*args)


def _ffn_kernel(h_ref, x_ref, mod_ref, wg_ref, wu_ref, wd_ref, lg_ref, lb_ref, o_ref, acc_ref, *, alpha):
    j = pl.program_id(1)

    @pl.when(j == 0)
    def _():
        acc_ref[...] = jnp.zeros_like(acc_ref)

    h = h_ref[...]
    a = _dot(h, wg_ref[...])
    mid = (_silu(a) * _dot(h, wu_ref[...])).astype(BF16)
    acc_ref[...] += _dot(mid, wd_ref[...])

    @pl.when(j == pl.num_programs(1) - 1)
    def _():
        gt = mod_ref[0, 5:6, :]
        o_ref[...] = _layer_norm(alpha * x_ref[...] + (1.0 + gt) * acc_ref[...], lg_ref[...], lb_ref[...])


def _dense_ffn(h2, x1, mod_l, w_gate, w_up, w_down, ln_g, ln_b, alpha, tm, tf):
    bsz, seq, d = x1.shape
    n = bsz * seq
    dff = w_gate.shape[1]
    per_b = seq // tm
    return pl.pallas_call(
        functools.partial(_ffn_kernel, alpha=alpha),
        grid=(n // tm, dff // tf),
        in_specs=[
            pl.BlockSpec((tm, d), lambda i, j: (i, 0)),
            pl.BlockSpec((tm, d), lambda i, j: (i, 0)),
            pl.BlockSpec((1, N_MOD, d), lambda i, j: (i // per_b, 0, 0)),
            pl.BlockSpec((d, tf), lambda i, j: (0, j)),
            pl.BlockSpec((d, tf), lambda i, j: (0, j)),
            pl.BlockSpec((tf, d), lambda i, j: (j, 0)),
            pl.BlockSpec((1, d), lambda i, j: (0, 0)),
            pl.BlockSpec((1, d), lambda i, j: (0, 0)),
        ],
        out_specs=pl.BlockSpec((tm, d), lambda i, j: (i, 0)),
        out_shape=jax.ShapeDtypeStruct((n, d), F32),
        scratch_shapes=[pltpu.VMEM((tm, d), F32)],
        compiler_params=_cparams("parallel", "arbitrary"),
        name="dense_ffn",
    )(h2.reshape(n, d), x1.reshape(n, d), mod_l, w_gate.astype(BF16), w_up.astype(BF16), w_down.astype(BF16),
      ln_g.reshape(1, d), ln_b.reshape(1, d)).reshape(bsz, seq, d)


def _route_tables(route, count, rb, nblk):
    e = route[:, RT_E1:RT_E2 + 1].astype(jnp.int32)
    rank = route[:, RT_R1:RT_R2 + 1].astype(jnp.int32)
    cnt = count[0, :N_EXPERTS].astype(jnp.int32)
    padded = ((cnt + rb - 1) // rb) * rb
    ends = jnp.cumsum(padded)
    starts = ends - padded
    pos = jnp.sum(jnp.where(e[..., None] == jnp.arange(N_EXPERTS), starts, 0), axis=-1) + rank
    n_valid = ends[-1] // rb
    blk = jnp.minimum(jnp.arange(nblk, dtype=jnp.int32), n_valid - 1)
    blk_expert = jnp.sum((blk[:, None] * rb >= ends[None, :]).astype(jnp.int32), axis=-1)
    blk_expert = jnp.minimum(blk_expert, N_EXPERTS - 1)
    return pos.astype(jnp.int32), blk_expert.astype(jnp.int32), n_valid.astype(jnp.int32).reshape(1)


def _row_copy(src, src_row, dst, dst_row, sem):
    return pltpu.make_async_copy(src.at[pl.ds(src_row, 1), :], dst.at[pl.ds(dst_row, 1), :], sem)


def _dispatch_kernel(pos_ref, h_hbm, xs_in_hbm, xs_hbm, sem, *, tb):
    del xs_in_hbm
    i = pl.program_id(0)
    slot = i % 2
    base = i * tb

    def issue(r, carry):
        for kk in range(2):
            _row_copy(h_hbm, base + r, xs_hbm, pos_ref[0, 0, 2 * r + kk], sem.at[slot]).start()
        return carry

    lax.fori_loop(0, tb, issue, 0, unroll=8)

    def wait_all(s):
        pltpu.make_async_copy(h_hbm.at[pl.ds(0, 2 * tb), :], xs_hbm.at[pl.ds(0, 2 * tb), :], sem.at[s]).wait()

    @pl.when(i > 0)
    def _():
        wait_all(1 - slot)

    @pl.when(i == pl.num_programs(0) - 1)
    def _():
        wait_all(slot)


def _dispatch(h2, pos, n_rows, tb):
    n, d = h2.shape
    xs0 = jnp.zeros((n_rows, d), F32)
    return pl.pallas_call(
        functools.partial(_dispatch_kernel, tb=tb),
        grid=(n // tb,),
        in_specs=[
            pl.BlockSpec((1, 1, 2 * tb), lambda i: (i, 0, 0), memory_space=pltpu.SMEM),
            pl.BlockSpec(memory_space=pl.ANY),
            pl.BlockSpec(memory_space=pl.ANY),
        ],
        out_specs=pl.BlockSpec(memory_space=pl.ANY),
        out_shape=jax.ShapeDtypeStruct((n_rows, d), F32),
        scratch_shapes=[pltpu.SemaphoreType.DMA((2,))],
        input_output_aliases={2: 0},
        compiler_params=_cparams("arbitrary"),
        name="moe_dispatch",
    )(pos.reshape(n // tb, 1, 2 * tb), h2, xs0)


def _expert_kernel(be_ref, nv_ref, xs_ref, wg_ref, wu_ref, wd_ref, y_ref, acc_ref, xb_ref):
    del be_ref
    i = pl.program_id(0)
    j = pl.program_id(1)

    @pl.when(i < nv_ref[0])
    def _():
        @pl.when(j == 0)
        def _():
            acc_ref[...] = jnp.zeros_like(acc_ref)
            xb_ref[...] = xs_ref[...].astype(BF16)

        h = xb_ref[...]
        a = _dot(h, wg_ref[0])
        mid = (_silu(a) * _dot(h, wu_ref[0])).astype(BF16)
        acc_ref[...] += _dot(mid, wd_ref[0])

        @pl.when(j == pl.num_programs(1) - 1)
        def _():
            y_ref[...] = acc_ref[...]

    @pl.when((i >= nv_ref[0]) & (j == 0))
    def _():
        y_ref[...] = jnp.zeros_like(y_ref)


def _experts(xs, blk_expert, n_valid, w_gate, w_up, w_down, rb, tf):
    n_rows, d = xs.shape
    dff = w_gate.shape[2]
    nj = dff // tf
    row_blk = lambda i, j, be, nv: (jnp.minimum(i, nv[0] - 1), 0)
    jj = lambda i, j, nv: jnp.where(i < nv[0], j, nj - 1)
    grid_spec = pltpu.PrefetchScalarGridSpec(
        num_scalar_prefetch=2,
        grid=(n_rows // rb, nj),
        in_specs=[
            pl.BlockSpec((rb, d), row_blk),
            pl.BlockSpec((1, d, tf), lambda i, j, be, nv: (be[i], 0, jj(i, j, nv))),
            pl.BlockSpec((1, d, tf), lambda i, j, be, nv: (be[i], 0, jj(i, j, nv))),
            pl.BlockSpec((1, tf, d), lambda i, j, be, nv: (be[i], jj(i, j, nv), 0)),
        ],
        out_specs=pl.BlockSpec((rb, d), lambda i, j, be, nv: (i, 0)),
        scratch_shapes=[pltpu.VMEM((rb, d), F32), pltpu.VMEM((rb, d), BF16)],
    )
    return pl.pallas_call(
        _expert_kernel,
        grid_spec=grid_spec,
        out_shape=jax.ShapeDtypeStruct((n_rows, d), F32),
        compiler_params=_cparams("arbitrary", "arbitrary"),
        name="moe_experts",
    )(blk_expert, n_valid, xs, w_gate.astype(BF16), w_up.astype(BF16), w_down.astype(BF16))


def _combine_kernel(pos_ref, posn_ref, route_ref, x_ref, mod_ref, lg_ref, lb_ref, y_hbm, o_ref, buf, sem,
                    *, tb, alpha):
    i = pl.program_id(0)
    n = pl.num_programs(0)
    slot = i % 2

    def issue(p_ref, s):
        def body(r, carry):
            for kk in range(2):
                _row_copy(y_hbm, p_ref[0, 0, 2 * r + kk], buf.at[s, kk], r, sem.at[s]).start()
            return carry
        lax.fori_loop(0, tb, body, 0, unroll=8)

    @pl.when(i == 0)
    def _():
        issue(pos_ref, 0)

    @pl.when(i + 1 < n)
    def _():
        issue(posn_ref, 1 - slot)

    for kk in range(2):
        pltpu.make_async_copy(y_hbm.at[pl.ds(0, tb), :], buf.at[slot, kk], sem.at[slot]).wait()

    rec = route_ref[...]
    f = rec[:, RT_W1:RT_W1 + 1] * buf[slot, 0] + rec[:, RT_W2:RT_W2 + 1] * buf[slot, 1]
    gt = mod_ref[0, 5:6, :]
    o_ref[...] = _layer_norm(alpha * x_ref[...] + (1.0 + gt) * f, lg_ref[...], lb_ref[...])


def _combine(ys, pos, route, x1, mod_l, ln_g, ln_b, alpha, tb, seq):
    n, d = x1.shape
    nb = n // tb
    per_b = seq // tb
    pos3 = pos.reshape(nb, 1, 2 * tb)
    return pl.pallas_call(
        functools.partial(_combine_kernel, tb=tb, alpha=alpha),
        grid=(nb,),
        in_specs=[
            pl.BlockSpec((1, 1, 2 * tb), lambda i: (i, 0, 0), memory_space=pltpu.SMEM),
            pl.BlockSpec((1, 1, 2 * tb), lambda i: (jnp.minimum(i + 1, nb - 1), 0, 0), memory_space=pltpu.SMEM),
            pl.BlockSpec((tb, ROUTER_LANES), lambda i: (i, 0)),
            pl.BlockSpec((tb, d), lambda i: (i, 0)),
            pl.BlockSpec((1, N_MOD, d), lambda i: (i // per_b, 0, 0)),
            pl.BlockSpec((1, d), lambda i: (0, 0)),
            pl.BlockSpec((1, d), lambda i: (0, 0)),
            pl.BlockSpec(memory_space=pl.ANY),
        ],
        out_specs=pl.BlockSpec((tb, d), lambda i: (i, 0)),
        out_shape=jax.ShapeDtypeStruct((n, d), F32),
        scratch_shapes=[pltpu.VMEM((2, 2, tb, d), F32), pltpu.SemaphoreType.DMA((2,))],
        compiler_params=_cparams("arbitrary"),
        name="moe_combine",
    )(pos3, pos3, route, x1, mod_l, ln_g.reshape(1, d), ln_b.reshape(1, d), ys)


def _moe_ffn(h2, x1, route, count, mod_l, w_gate, w_up, w_down, ln_g, ln_b, alpha, rb, tf, tb):
    bsz, seq, d = x1.shape
    n = bsz * seq
    n_rows = 2 * n + N_EXPERTS * rb
    route2 = route.reshape(n, ROUTER_LANES)
    pos, blk_expert, n_valid = _route_tables(route2, count, rb, n_rows // rb)
    xs = _dispatch(h2.reshape(n, d), pos, n_rows, tb)
    ys = _experts(xs, blk_expert, n_valid, w_gate, w_up, w_down, rb, tf)
    out = _combine(ys, pos, route2, x1.reshape(n, d), mod_l, ln_g, ln_b, alpha, tb, seq)
    return out.reshape(bsz, seq, d)


def _tile(n, want):
    t = min(n, want)
    assert n % t == 0, (n, want)
    return t


def kernel(x, c, mod_w, mod_b, w_in, w_out, s5_lam_re, s5_lam_im, s5_log_dt, s5_b_re, s5_b_im, s5_c_re, s5_c_im, s5_d, s5_w_glu, s5_b_glu, gla_w_alpha_up, gla_b_alpha, gla_head_gain, ln_mix_g, ln_mix_b, ffn_w_gate, ffn_w_up, ffn_w_down, moe_w_router, moe_b_router, moe_w_gate, moe_w_up, moe_w_down, ln_ffn_g, ln_ffn_b):
    bsz, seq, d = x.shape
    depth = mod_w.shape[0]
    alpha = (2.0 * depth) ** 0.25
    tm = _tile(seq, 512)
    tt = _tile(seq, 16)
    tg = _tile(seq, 256)
    tm_ffn = _tile(seq, 1024)
    tf = _tile(D_FF, 512)
    tb = _tile(seq, 256)

    mod =_modulation(c, mod_w, mod_b).reshape(depth, bsz, N_MOD, d)
    for layer in range(depth):
        mod_l = mod[layer]
        w_gate_fold = _gate_fold(w_in[layer][:, IN_P[0]:], gla_w_alpha_up[layer])
        w_cat = jnp.concatenate([w_in[layer][:, :IN_P[0]], w_gate_fold], axis=1).astype(BF16)
        u_tm, q, k, v, g_out, pre = _in_projection(x, mod_l, w_cat, tm)

        s5w = _s5_weights(s5_lam_re[layer], s5_lam_im[layer], s5_log_dt[layer], s5_b_re[layer], s5_b_im[layer],
                          s5_c_re[layer], s5_c_im[layer])
        y_s5 = _s5_group(u_tm.reshape(seq, bsz, S5_WIDTH), *s5w, s5_d[layer], s5_w_glu[layer], s5_b_glu[layer], tt)
        y_s5 = y_s5.reshape(seq, bsz * S5_WIDTH)
        y_gla = _gla_group(q, k, v, g_out, pre, gla_b_alpha[layer], gla_head_gain[layer], tg)

        i = layer // 2
        if layer % 2 == 0:
            x1, h2 = _out_projection(y_s5, y_gla, x, mod_l, w_out[layer], ln_mix_g[layer], ln_mix_b[layer], alpha, tm)
            x = _dense_ffn(h2, x1, mod_l, ffn_w_gate[i], ffn_w_up[i], ffn_w_down[i],
                           ln_ffn_g[layer], ln_ffn_b[layer], alpha, tm_ffn, tf)
        else:
            x1, h2, route, count = _out_projection(y_s5, y_gla, x, mod_l, w_out[layer], ln_mix_g[layer],
                                                   ln_mix_b[layer], alpha, tm,
                                                   router=(moe_w_router[i], moe_b_router[i]))
            x = _moe_ffn(h2, x1, route, count, mod_l, moe_w_gate[i], moe_w_up[i], moe_w_down[i],
                         ln_ffn_g[layer], ln_ffn_b[layer], alpha, tm_ffn, tf, tb)
    return x
```

```python
import functools
import math

import jax
import jax.numpy as jnp
from jax import lax
from jax.experimental import pallas as pl
from jax.experimental.pallas import tpu as pltpu

F32 = jnp.float32
BF16 = jnp.bfloat16

D_MODEL = 1024
S5_WIDTH = 512
S5_GROUP_CH = 16
S5_GROUPS = 32
S5_STATE = 64
S5_LANES = S5_GROUPS * S5_STATE
GLA_WIDTH = 512
GLA_HEADS = 4
GLA_DK = 256
GLA_HEAD_K = 64
GLA_HEAD_V = 128
GLA_GATE_RANK = 16
GLA_GATE_TAU = 16.0
GLA_CHUNK = 64
D_FF = 3584
N_EXPERTS = 8
N_MOD = 6
LN_EPS = 1e-5
RMS_EPS = 1e-6
ROUTER_LANES = 128
NEG_BIG = -1e30

VMEM_LIMIT = 56 * 1024 * 1024


def _cparams(*sem):
    return pltpu.CompilerParams(dimension_semantics=sem, vmem_limit_bytes=VMEM_LIMIT)


def _dot(a, b):
    return jnp.dot(a, b, preferred_element_type=F32)


def _split2(a):
    a1 = a.astype(BF16)
    a2 = (a - a1.astype(F32)).astype(BF16)
    return a1, a2


def _split3(a):
    a1 = a.astype(BF16)
    r1 = a - a1.astype(F32)
    a2 = r1.astype(BF16)
    a3 = (r1 - a2.astype(F32)).astype(BF16)
    return a1, a2, a3


def _dot_f32(a, b):
    a1, a2, a3 = _split3(a)
    b1, b2, b3 = _split3(b)
    lo = _dot(a1, b3) + _dot(a2, b2) + _dot(a3, b1)
    mid = _dot(a1, b2) + _dot(a2, b1)
    return lo + mid + _dot(a1, b1)


def _dot_f32x3(a, b):
    a1, a2 = _split2(a)
    b1, b2 = _split2(b)
    return (_dot(a1, b2) + _dot(a2, b1)) + _dot(a1, b1)


def _silu(x):
    return x * jax.nn.sigmoid(x)


def _layer_norm(r, gain, bias):
    mu = jnp.mean(r, axis=-1, keepdims=True)
    d = r - mu
    var = jnp.mean(d * d, axis=-1, keepdims=True)
    return d * lax.rsqrt(var + LN_EPS) * gain + bias


def _mod_kernel(c_ref, w_ref, b_ref, o_ref):
    o_ref[0] = _dot_f32(_silu(c_ref[...]), w_ref[0]) + b_ref[0]


def _modulation(c, mod_w, mod_b):
    depth, d, nd = mod_w.shape
    bsz = c.shape[0]
    return pl.pallas_call(
        _mod_kernel,
        grid=(depth, nd // d),
        in_specs=[
            pl.BlockSpec((bsz, d), lambda l, n: (0, 0)),
            pl.BlockSpec((1, d, d), lambda l, n: (l, 0, n)),
            pl.BlockSpec((1, 1, d), lambda l, n: (l, 0, n)),
        ],
        out_specs=pl.BlockSpec((1, bsz, d), lambda l, n: (l, 0, n)),
        out_shape=jax.ShapeDtypeStruct((depth, bsz, nd), F32),
        compiler_params=_cparams("parallel", "parallel"),
        name="modulation",
    )(c, mod_w, mod_b.reshape(depth, 1, nd))


def _gate_fold_kernel(wl_ref, wu_ref, o_ref):
    o_ref[...] = _dot_f32(wl_ref[...], wu_ref[...])


def _gate_fold(w_low, w_up):
    d = w_low.shape[0]
    pad = 128 - GLA_GATE_RANK
    wl = jnp.pad(w_low, ((0, 0), (0, pad)))
    wu = jnp.pad(w_up, ((0, pad), (0, 0)))
    return pl.pallas_call(
        _gate_fold_kernel,
        out_shape=jax.ShapeDtypeStruct((d, GLA_DK), F32),
        name="gate_fold",
    )(wl, wu)


def _s5_disc_kernel(lre_ref, lim_ref, ldt_ref, bre_ref, bim_ref, lam_ref, bbre_ref, bbim_ref):
    lre = lre_ref[...]
    lim = lim_ref[...]
    dt = jnp.exp(ldt_ref[...])
    mag = jnp.exp(lre * dt)
    ang = lim * dt
    are = mag * jnp.cos(ang)
    aim = mag * jnp.sin(ang)
    lam_ref[0] = are
    lam_ref[1] = aim
    den = lre * lre + lim * lim
    nre = are - 1.0
    cre = (nre * lre + aim * lim) / den
    cim = (aim * lre - nre * lim) / den
    for h in range(S5_GROUP_CH):
        bre = bre_ref[h]
        bim = bim_ref[h]
        bbre_ref[h] = cre * bre - cim * bim
        bbim_ref[h] = cre * bim + cim * bre


def _s5_discretise(lam_re, lam_im, log_dt, b_re, b_im):
    g, p = lam_re.shape
    hch = b_re.shape[-1]
    bre_t = jnp.transpose(b_re, (2, 0, 1))
    bim_t = jnp.transpose(b_im, (2, 0, 1))
    lam, bbre, bbim = pl.pallas_call(
        _s5_disc_kernel,
        out_shape=(
            jax.ShapeDtypeStruct((2, g, p), F32),
            jax.ShapeDtypeStruct((hch, g, p), F32),
            jax.ShapeDtypeStruct((hch, g, p), F32),
        ),
        name="s5_discretise",
    )(lam_re, lam_im, log_dt.reshape(g, 1), bre_t, bim_t)
    return lam, bbre, bbim


def _s5_weights(lam_re, lam_im, log_dt, b_re, b_im, c_re, c_im):
    g, p, hch = S5_GROUPS, S5_STATE, S5_GROUP_CH
    lam, bbre, bbim = _s5_discretise(lam_re, lam_im, log_dt, b_re, b_im)
    lam_rows = lam.reshape(2, g * p)
    eye = jnp.eye(g, dtype=F32)

    def in_blocks(bb):
        dense = jnp.einsum('hgp,gk->ghkp', bb, eye).reshape(g * hch, g * p)
        blocks = [dense[128 * (n // 2):128 * (n // 2) + 128, 256 * n:256 * (n + 1)] for n in range(8)]
        return jnp.stack(blocks).astype(BF16)

    def out_blocks(cc, sign):
        dense = jnp.einsum('ghp,gk->gpkh', cc, eye).reshape(g * p, g * hch) * sign
        blocks = [dense[1024 * m:1024 * (m + 1), 256 * m:256 * (m + 1)] for m in range(2)]
        return jnp.stack(blocks).astype(BF16)

    return lam_rows, in_blocks(bbre), in_blocks(bbim), out_blocks(c_re, 1.0), out_blocks(c_im, -1.0)


IN_U = (0, 512)
IN_Q = (512, 768)
IN_K = (768, 1024)
IN_V = (1024, 1536)
IN_G = (1536, 2048)
IN_P = (2048, 2304)


def _inproj_kernel(x_ref, mod_ref, w_ref, u_ref, q_ref, k_ref, v_ref, g_ref, p_ref):
    sh = mod_ref[0, 0:1, :]
    sc = mod_ref[0, 1:2, :]
    h = (x_ref[0] * (1.0 + sc) + sh).astype(BF16)
    u_ref[...] = _dot(h, w_ref[:, IN_U[0]:IN_U[1]]).astype(BF16)
    q_ref[0] = _dot(h, w_ref[:, IN_Q[0]:IN_Q[1]]).astype(BF16)
    k_ref[0] = _dot(h, w_ref[:, IN_K[0]:IN_K[1]]).astype(BF16)
    v_ref[0] = _dot(h, w_ref[:, IN_V[0]:IN_V[1]]).astype(BF16)
    g_ref[0] = _dot(h, w_ref[:, IN_G[0]:IN_G[1]]).astype(BF16)
    p_ref[0] = _dot(h, w_ref[:, IN_P[0]:IN_P[1]])


def _in_projection(x, mod_l, w_cat, tm):
    bsz, seq, d = x.shape
    ncol = w_cat.shape[1]
    bspec = lambda w: pl.BlockSpec((1, tm, w), lambda b, t: (b, t, 0))
    return pl.pallas_call(
        _inproj_kernel,
        grid=(bsz, seq // tm),
        in_specs=[
            pl.BlockSpec((1, tm, d), lambda b, t: (b, t, 0)),
            pl.BlockSpec((1, N_MOD, d), lambda b, t: (b, 0, 0)),
            pl.BlockSpec((d, ncol), lambda b, t: (0, 0)),
        ],
        out_specs=[
            pl.BlockSpec((tm, S5_WIDTH), lambda b, t: (t, b)),
            bspec(GLA_DK), bspec(GLA_DK), bspec(GLA_WIDTH), bspec(GLA_WIDTH), bspec(GLA_DK),
        ],
        out_shape=[
            jax.ShapeDtypeStruct((seq, bsz * S5_WIDTH), BF16),
            jax.ShapeDtypeStruct((bsz, seq, GLA_DK), BF16),
            jax.ShapeDtypeStruct((bsz, seq, GLA_DK), BF16),
            jax.ShapeDtypeStruct((bsz, seq, GLA_WIDTH), BF16),
            jax.ShapeDtypeStruct((bsz, seq, GLA_WIDTH), BF16),
            jax.ShapeDtypeStruct((bsz, seq, GLA_DK), F32),
        ],
        compiler_params=_cparams("parallel", "parallel"),
        name="in_projection",
    )(x, mod_l, w_cat)


S5_SCAN_LANES = 256


def _s5_kernel(u_ref, lam_ref, wbre_ref, wbim_ref, wcre_ref, wcim_ref, d_ref, wglu_ref, bglu_ref,
               o_ref, hre, him, st_re, st_im, *, tt, nb):
    @pl.when(pl.program_id(0) == 0)
    def _():
        st_re[...] = jnp.zeros_like(st_re)
        st_im[...] = jnp.zeros_like(st_im)

    rows = tt * nb
    ub = u_ref[...].reshape(rows, S5_WIDTH)
    for n in range(8):
        lhs = ub[:, 128 * (n // 2):128 * (n // 2) + 128]
        hre[:, 256 * n:256 * (n + 1)] = _dot(lhs, wbre_ref[n])
        him[:, 256 * n:256 * (n + 1)] = _dot(lhs, wbim_ref[n])

    for c in range(S5_LANES // S5_SCAN_LANES):
        ls = slice(c * S5_SCAN_LANES, (c + 1) * S5_SCAN_LANES)
        lr = jnp.broadcast_to(lam_ref[0:1, ls], (nb, S5_SCAN_LANES))
        li = jnp.broadcast_to(lam_ref[1:2, ls], (nb, S5_SCAN_LANES))
        hr = st_re[:, ls]
        hi = st_im[:, ls]
        for t in range(tt):
            rs = slice(t * nb, (t + 1) * nb)
            nr = lr * hr - li * hi + hre[rs, ls]
            ni = lr * hi + li * hr + him[rs, ls]
            hre[rs, ls] = nr
            him[rs, ls] = ni
            hr, hi = nr, ni
        st_re[:, ls] = hr
        st_im[:, ls] = hi

    ys = []
    for m in range(2):
        ks = slice(1024 * m, 1024 * (m + 1))
        ys.append(_dot(hre[:, ks].astype(BF16), wcre_ref[m]) + _dot(him[:, ks].astype(BF16), wcim_ref[m]))
    y = jnp.concatenate(ys, axis=-1) + d_ref[...] * ub.astype(F32)
    y = 0.5 * y * (1.0 + jnp.tanh(math.sqrt(2.0 / math.pi) * (y + 0.044715 * (y * y * y))))
    z = y * jax.nn.sigmoid(_dot(y.astype(BF16), wglu_ref[...]) + bglu_ref[...])
    o_ref[...] = z.astype(BF16).reshape(tt, nb, S5_WIDTH)


def _s5_group(u_tm, lam_rows, wbre, wbim, wcre, wcim, d_skip, w_glu, b_glu, tt):
    seq, nb, width = u_tm.shape
    rows = tt * nb
    full = lambda a: pl.BlockSpec(a.shape, lambda t: (0,) * a.ndim)
    d2 = d_skip.reshape(1, width)
    b2 = b_glu.reshape(1, width)
    wg = w_glu.astype(BF16)
    return pl.pallas_call(
        functools.partial(_s5_kernel, tt=tt, nb=nb),
        grid=(seq // tt,),
        in_specs=[
            pl.BlockSpec((tt, nb, width), lambda t: (t, 0, 0)),
            full(lam_rows), full(wbre), full(wbim), full(wcre), full(wcim), full(d2), full(wg), full(b2),
        ],
        out_specs=pl.BlockSpec((tt, nb, width), lambda t: (t, 0, 0)),
        out_shape=jax.ShapeDtypeStruct((seq, nb, width), BF16),
        scratch_shapes=[
            pltpu.VMEM((rows, S5_LANES), F32),
            pltpu.VMEM((rows, S5_LANES), F32),
            pltpu.VMEM((nb, S5_LANES), F32),
            pltpu.VMEM((nb, S5_LANES), F32),
        ],
        compiler_params=_cparams("arbitrary"),
        name="s5_group",
    )(u_tm, lam_rows, wbre, wbim, wcre, wcim, d2, wg, b2)


def _gla_kernel(q_ref, k_ref, v_ref, g_ref, p_ref, ba_ref, gain_ref, o_ref, s_ref, *, nchunk):
    @pl.when(pl.program_id(1) == 0)
    def _():
        s_ref[...] = jnp.zeros_like(s_ref)

    c = GLA_CHUNK
    row = lax.broadcasted_iota(jnp.int32, (c, c), 0)
    col = lax.broadcasted_iota(jnp.int32, (c, c), 1)
    causal = row >= col
    tri = jnp.where(causal, 1.0, 0.0).astype(BF16)
    lane = lax.broadcasted_iota(jnp.int32, (1, 2 * GLA_HEAD_K), 1)
    head_lanes = (lane < GLA_HEAD_K, lane >= GLA_HEAD_K)
    srow = lax.broadcasted_iota(jnp.int32, (2 * GLA_HEAD_V, 2 * GLA_HEAD_K), 0)
    scol = lax.broadcasted_iota(jnp.int32, (2 * GLA_HEAD_V, 2 * GLA_HEAD_K), 1)
    same_head = (srow < GLA_HEAD_V) == (scol < GLA_HEAD_K)
    scale = GLA_HEAD_K ** -0.5
    mid = c // 2 - 1

    for ci in range(nchunk):
        sl = slice(ci * c, (ci + 1) * c)
        pre = p_ref[0, sl, :] + ba_ref[...]
        log_alpha = (jnp.minimum(pre, 0.0) - jnp.log1p(jnp.exp(-jnp.abs(pre)))) * (1.0 / GLA_GATE_TAU)
        a1, a2, a3 = _split3(log_alpha)
        b = _dot(tri, a3) + _dot(tri, a2) + _dot(tri, a1)
        b_mid = b[mid:mid + 1, :]
        b_last = b[c - 1:c, :]
        q = q_ref[0, sl, :].astype(F32) * scale
        k = k_ref[0, sl, :].astype(F32)
        q_in = (q * jnp.exp(b)).astype(BF16)
        q_e = q * jnp.exp(b - b_mid)
        k_e = (k * jnp.exp(b_mid - b)).astype(BF16)
        k_d = (k * jnp.exp(b_last - b)).astype(BF16)
        decay = jnp.exp(b_last)
        for pr in range(GLA_HEADS // 2):
            ks = slice(2 * GLA_HEAD_K * pr, 2 * GLA_HEAD_K * (pr + 1))
            vs = slice(2 * GLA_HEAD_V * pr, 2 * GLA_HEAD_V * (pr + 1))
            s_t = s_ref[pr]
            o_inter = lax.dot_general(q_in[:, ks], s_t.astype(BF16), (((1,), (1,)), ((), ())),
                                      preferred_element_type=F32)
            v_pair = v_ref[0, sl, vs]
            q_pair = q_e[:, ks]
            for hh in range(2):
                q_h = jnp.where(head_lanes[hh], q_pair, 0.0).astype(BF16)
                scores = lax.dot_general(q_h, k_e[:, ks], (((1,), (1,)), ((), ())),
                                         preferred_element_type=F32)
                scores = jnp.where(causal, scores, 0.0).astype(BF16)
                hv = slice(GLA_HEAD_V * hh, GLA_HEAD_V * (hh + 1))
                o_h = o_inter[:, hv] + _dot(scores, v_pair[:, hv])
                os_ = slice(2 * GLA_HEAD_V * pr + GLA_HEAD_V * hh, 2 * GLA_HEAD_V * pr + GLA_HEAD_V * (hh + 1))
                ms = jnp.mean(o_h * o_h, axis=-1, keepdims=True)
                o_n = o_h * lax.rsqrt(ms + RMS_EPS) * gain_ref[:, os_]
                o_ref[0, sl, os_] = (o_n * _silu(g_ref[0, sl, os_].astype(F32))).astype(BF16)
            upd = lax.dot_general(v_pair, k_d[:, ks], (((0,), (0,)), ((), ())),
                                  preferred_element_type=F32)
            s_ref[pr] = decay[:, ks] * s_t + jnp.where(same_head, upd, 0.0)


def _gla_group(q, k, v, g_out, pre, b_alpha, head_gain, tg):
    bsz, seq, _ = q.shape
    blk = lambda w: pl.BlockSpec((1, tg, w), lambda b, t: (b, t, 0))
    ba = b_alpha.reshape(1, GLA_DK)
    gain = head_gain.reshape(1, GLA_WIDTH)
    return pl.pallas_call(
        functools.partial(_gla_kernel, nchunk=tg // GLA_CHUNK),
        grid=(bsz, seq // tg),
        in_specs=[
            blk(GLA_DK), blk(GLA_DK), blk(GLA_WIDTH), blk(GLA_WIDTH), blk(GLA_DK),
            pl.BlockSpec((1, GLA_DK), lambda b, t: (0, 0)),
            pl.BlockSpec((1, GLA_WIDTH), lambda b, t: (0, 0)),
        ],
        out_specs=blk(GLA_WIDTH),
        out_shape=jax.ShapeDtypeStruct((bsz, seq, GLA_WIDTH), BF16),
        scratch_shapes=[pltpu.VMEM((GLA_HEADS // 2, 2 * GLA_HEAD_V, 2 * GLA_HEAD_K), F32)],
        compiler_params=_cparams("parallel", "arbitrary"),
        name="gla_group",
    )(q, k, v, g_out, pre, ba, gain)


def _mix_out(ys_ref, yg_ref, x_ref, mod_ref, w_ref, lg_ref, lb_ref, alpha):
    y = _dot(ys_ref[...], w_ref[0:S5_WIDTH, :]) + _dot(yg_ref[0], w_ref[S5_WIDTH:, :])
    gt = mod_ref[0, 2:3, :]
    x1 = _layer_norm(alpha * x_ref[0] + (1.0 + gt) * y, lg_ref[...], lb_ref[...])
    h2 = x1 * (1.0 + mod_ref[0, 4:5, :]) + mod_ref[0, 3:4, :]
    return x1, h2


def _outproj_kernel(ys_ref, yg_ref, x_ref, mod_ref, w_ref, lg_ref, lb_ref, x1_ref, h2_ref, *, alpha):
    x1, h2 = _mix_out(ys_ref, yg_ref, x_ref, mod_ref, w_ref, lg_ref, lb_ref, alpha)
    x1_ref[0] = x1
    h2_ref[0] = h2.astype(BF16)


RT_E1, RT_E2, RT_R1, RT_R2, RT_W1, RT_W2 = range(6)


def _top2_route(logits, count):
    tm = logits.shape[0]
    lane = lax.broadcasted_iota(jnp.int32, logits.shape, 1).astype(F32)
    l0 = jnp.where(lane < N_EXPERTS, logits, NEG_BIG)
    m1 = jnp.max(l0, axis=-1, keepdims=True)
    i1 = jnp.min(jnp.where(l0 == m1, lane, float(ROUTER_LANES)), axis=-1, keepdims=True)
    sel1 = lane == i1
    l1 = jnp.where(sel1, NEG_BIG, l0)
    m2 = jnp.max(l1, axis=-1, keepdims=True)
    i2 = jnp.min(jnp.where(l1 == m2, lane, float(ROUTER_LANES)), axis=-1, keepdims=True)
    sel2 = lane == i2
    e2 = jnp.exp(m2 - m1)
    w1 = 1.0 / (1.0 + e2)
    w2 = e2 / (1.0 + e2)
    chosen = jnp.where(sel1, 1.0, 0.0) + jnp.where(sel2, 1.0, 0.0)
    row = lax.broadcasted_iota(jnp.int32, (tm, tm), 0)
    col = lax.broadcasted_iota(jnp.int32, (tm, tm), 1)
    earlier = jnp.where(row > col, 1.0, 0.0).astype(BF16)
    before = _dot(earlier, chosen.astype(BF16)) + count
    r1 = jnp.sum(jnp.where(sel1, before, 0.0), axis=-1, keepdims=True)
    r2 = jnp.sum(jnp.where(sel2, before, 0.0), axis=-1, keepdims=True)
    rec = jnp.zeros_like(logits)
    for ln, val in ((RT_E1, i1), (RT_E2, i2), (RT_R1, r1), (RT_R2, r2), (RT_W1, w1), (RT_W2, w2)):
        rec = jnp.where(lane == float(ln), val, rec)
    return rec, count + jnp.sum(chosen, axis=0, keepdims=True)


def _outproj_router_kernel(ys_ref, yg_ref, x_ref, mod_ref, w_ref, lg_ref, lb_ref, wr_ref, br_ref,
                           x1_ref, h2_ref, route_ref, count_ref, cnt, *, alpha):
    @pl.when((pl.program_id(0) == 0) & (pl.program_id(1) == 0))
    def _():
        cnt[...] = jnp.zeros_like(cnt)

    x1, h2 = _mix_out(ys_ref, yg_ref, x_ref, mod_ref, w_ref, lg_ref, lb_ref, alpha)
    x1_ref[0] = x1
    h2_ref[0] = h2
    logits = _dot_f32x3(h2, wr_ref[...]) + br_ref[...]
    rec, new_count = _top2_route(logits, cnt[...])
    route_ref[0] = rec
    cnt[...] = new_count
    count_ref[...] = jnp.broadcast_to(new_count, count_ref.shape)


def _out_projection(y_s5_tm, y_gla, x, mod_l, w_out, ln_g, ln_b, alpha, tm, router=None):
    bsz, seq, d = x.shape
    row = lambda a: a.reshape(1, -1)
    in_specs = [
        pl.BlockSpec((tm, S5_WIDTH), lambda b, t: (t, b)),
        pl.BlockSpec((1, tm, GLA_WIDTH), lambda b, t: (b, t, 0)),
        pl.BlockSpec((1, tm, d), lambda b, t: (b, t, 0)),
        pl.BlockSpec((1, N_MOD, d), lambda b, t: (b, 0, 0)),
        pl.BlockSpec((d, d), lambda b, t: (0, 0)),
        pl.BlockSpec((1, d), lambda b, t: (0, 0)),
        pl.BlockSpec((1, d), lambda b, t: (0, 0)),
    ]
    out_specs = [
        pl.BlockSpec((1, tm, d), lambda b, t: (b, t, 0)),
        pl.BlockSpec((1, tm, d), lambda b, t: (b, t, 0)),
    ]
    out_shape = [jax.ShapeDtypeStruct((bsz, seq, d), F32), jax.ShapeDtypeStruct((bsz, seq, d), BF16)]
    args = [y_s5_tm, y_gla, x, mod_l, w_out.astype(BF16), row(ln_g), row(ln_b)]
    scratch = []
    sem = ("parallel", "parallel")
    if router is None:
        body = functools.partial(_outproj_kernel, alpha=alpha)
    else:
        w_router, b_router = router
        pad = ROUTER_LANES - N_EXPERTS
        in_specs += [pl.BlockSpec((d, ROUTER_LANES), lambda b, t: (0, 0)),
                     pl.BlockSpec((1, ROUTER_LANES), lambda b, t: (0, 0))]
        out_specs += [pl.BlockSpec((1, tm, ROUTER_LANES), lambda b, t: (b, t, 0)),
                      pl.BlockSpec((8, ROUTER_LANES), lambda b, t: (0, 0))]
        out_shape[1] = jax.ShapeDtypeStruct((bsz, seq, d), F32)
        out_shape += [jax.ShapeDtypeStruct((bsz, seq, ROUTER_LANES), F32),
                      jax.ShapeDtypeStruct((8, ROUTER_LANES), F32)]
        args += [jnp.pad(w_router, ((0, 0), (0, pad))), jnp.pad(row(b_router), ((0, 0), (0, pad)))]
        body = functools.partial(_outproj_router_kernel, alpha=alpha)
        scratch = [pltpu.VMEM((1, ROUTER_LANES), F32)]
        sem = ("arbitrary", "arbitrary")
    return pl.pallas_call(
        body,
        grid=(bsz, seq // tm),
        in_specs=in_specs,
        out_specs=out_specs,
        out_shape=out_shape,
        scratch_shapes=scratch,
        compiler_params=_cparams(*sem),
        name="out_projection",
    )(*args)


def _ffn_kernel(h_ref, x_ref, mod_ref, wg_ref, wu_ref, wd_ref, lg_ref, lb_ref, o_ref, acc_ref, *, alpha):
    j = pl.program_id(1)

    @pl.when(j == 0)
    def _():
        acc_ref[...] = jnp.zeros_like(acc_ref)

    h = h_ref[...]
    a = _dot(h, wg_ref[...])
    mid = (_silu(a) * _dot(h, wu_ref[...])).astype(BF16)
    acc_ref[...] += _dot(mid, wd_ref[...])

    @pl.when(j == pl.num_programs(1) - 1)
    def _():
        gt = mod_ref[0, 5:6, :]
        o_ref[...] = _layer_norm(alpha * x_ref[...] + (1.0 + gt) * acc_ref[...], lg_ref[...], lb_ref[...])


def _dense_ffn(h2, x1, mod_l, w_gate, w_up, w_down, ln_g, ln_b, alpha, tm, tf):
    bsz, seq, d = x1.shape
    n = bsz * seq
    dff = w_gate.shape[1]
    per_b = seq // tm
    return pl.pallas_call(
        functools.partial(_ffn_kernel, alpha=alpha),
        grid=(n // tm, dff // tf),
        in_specs=[
            pl.BlockSpec((tm, d), lambda i, j: (i, 0)),
            pl.BlockSpec((tm, d), lambda i, j: (i, 0)),
            pl.BlockSpec((1, N_MOD, d), lambda i, j: (i // per_b, 0, 0)),
            pl.BlockSpec((d, tf), lambda i, j: (0, j)),
            pl.BlockSpec((d, tf), lambda i, j: (0, j)),
            pl.BlockSpec((tf, d), lambda i, j: (j, 0)),
            pl.BlockSpec((1, d), lambda i, j: (0, 0)),
            pl.BlockSpec((1, d), lambda i, j: (0, 0)),
        ],
        out_specs=pl.BlockSpec((tm, d), lambda i, j: (i, 0)),
        out_shape=jax.ShapeDtypeStruct((n, d), F32),
        scratch_shapes=[pltpu.VMEM((tm, d), F32)],
        compiler_params=_cparams("parallel", "arbitrary"),
        name="dense_ffn",
    )(h2.reshape(n, d), x1.reshape(n, d), mod_l, w_gate.astype(BF16), w_up.astype(BF16), w_down.astype(BF16),
      ln_g.reshape(1, d), ln_b.reshape(1, d)).reshape(bsz, seq, d)


def _route_tables(route, count, rb, nblk):
    e = route[:, RT_E1:RT_E2 + 1].astype(jnp.int32)
    rank = route[:, RT_R1:RT_R2 + 1].astype(jnp.int32)
    cnt = count[0, :N_EXPERTS].astype(jnp.int32)
    padded = ((cnt + rb - 1) // rb) * rb
    ends = jnp.cumsum(padded)
    starts = ends - padded
    pos = jnp.sum(jnp.where(e[..., None] == jnp.arange(N_EXPERTS), starts, 0), axis=-1) + rank
    n_valid = ends[-1] // rb
    blk = jnp.minimum(jnp.arange(nblk, dtype=jnp.int32), n_valid - 1)
    blk_expert = jnp.sum((blk[:, None] * rb >= ends[None, :]).astype(jnp.int32), axis=-1)
    blk_expert = jnp.minimum(blk_expert, N_EXPERTS - 1)
    n_rows = jnp.full((1,), nblk * rb, jnp.int32)
    fill = jnp.stack([jnp.concatenate([starts + cnt, ends[-1:]]), jnp.concatenate([ends, n_rows])], axis=1)
    return (pos.astype(jnp.int32), blk_expert.astype(jnp.int32), n_valid.astype(jnp.int32).reshape(1),
            fill.reshape(-1).astype(jnp.int32))


def _row_copy(src, src_row, dst, dst_row, sem):
    return pltpu.make_async_copy(src.at[pl.ds(src_row, 1), :], dst.at[pl.ds(dst_row, 1), :], sem)


def _dispatch_kernel(fill_ref, pos_ref, h_ref, xs_hbm, sem, *, tb):
    i = pl.program_id(0)

    def issue(r, carry):
        for kk in range(2):
            _row_copy(h_ref, r, xs_hbm, pos_ref[0, 0, 2 * r + kk], sem.at[0]).start()
        return carry

    lax.fori_loop(0, tb, issue, 0, unroll=8)

    @pl.when(i == 0)
    def _():
        for e in range(N_EXPERTS + 1):
            lo = fill_ref[2 * e]
            hi = fill_ref[2 * e + 1]

            def fill(p, carry):
                _row_copy(h_ref, 0, xs_hbm, p, sem.at[1]).start()
                return carry

            def drain(p, carry):
                _row_copy(h_ref, 0, xs_hbm, p, sem.at[1]).wait()
                return carry

            lax.fori_loop(lo, hi, fill, 0)
            lax.fori_loop(lo, hi, drain, 0)

    for _ in range(2):
        pltpu.make_async_copy(h_ref, xs_hbm.at[pl.ds(0, tb), :], sem.at[0]).wait()


def _dispatch(h2, pos, fill, n_rows, tb):
    n, d = h2.shape
    grid_spec = pltpu.PrefetchScalarGridSpec(
        num_scalar_prefetch=1,
        grid=(n // tb,),
        in_specs=[
            pl.BlockSpec((1, 1, 2 * tb), lambda i, f: (i, 0, 0), memory_space=pltpu.SMEM),
            pl.BlockSpec((tb, d), lambda i, f: (i, 0)),
        ],
        out_specs=pl.BlockSpec(memory_space=pl.ANY),
        scratch_shapes=[pltpu.SemaphoreType.DMA((2,))],
    )
    return pl.pallas_call(
        functools.partial(_dispatch_kernel, tb=tb),
        grid_spec=grid_spec,
        out_shape=jax.ShapeDtypeStruct((n_rows, d), F32),
        compiler_params=_cparams("arbitrary"),
        name="moe_dispatch",
    )(fill, pos.reshape(n // tb, 1, 2 * tb), h2)


def _expert_kernel(be_ref, nv_ref, xs_ref, wg_ref, wu_ref, wd_ref, y_ref, acc_ref, xb_ref):
    del be_ref
    i = pl.program_id(0)
    j = pl.program_id(1)

    @pl.when(i < nv_ref[0])
    def _():
        @pl.when(j == 0)
        def _():
            acc_ref[...] = jnp.zeros_like(acc_ref)
            xb_ref[...] = xs_ref[...].astype(BF16)

        h = xb_ref[...]
        a = _dot(h, wg_ref[0])
        mid = (_silu(a) * _dot(h, wu_ref[0])).astype(BF16)
        acc_ref[...] += _dot(mid, wd_ref[0])

        @pl.when(j == pl.num_programs(1) - 1)
        def _():
            y_ref[...] = acc_ref[...]

    @pl.when((i >= nv_ref[0]) & (j == 0))
    def _():
        y_ref[...] = jnp.zeros_like(y_ref)


def _experts(xs, blk_expert, n_valid, w_gate, w_up, w_down, rb, tf):
    n_rows, d = xs.shape
    dff = w_gate.shape[2]
    nj = dff // tf
    row_blk = lambda i, j, be, nv: (jnp.minimum(i, nv[0] - 1), 0)
    jj = lambda i, j, nv: jnp.where(i < nv[0], j, nj - 1)
    grid_spec = pltpu.PrefetchScalarGridSpec(
        num_scalar_prefetch=2,
        grid=(n_rows // rb, nj),
        in_specs=[
            pl.BlockSpec((rb, d), row_blk),
            pl.BlockSpec((1, d, tf), lambda i, j, be, nv: (be[i], 0, jj(i, j, nv))),
            pl.BlockSpec((1, d, tf), lambda i, j, be, nv: (be[i], 0, jj(i, j, nv))),
            pl.BlockSpec((1, tf, d), lambda i, j, be, nv: (be[i], jj(i, j, nv), 0)),
        ],
        out_specs=pl.BlockSpec((rb, d), lambda i, j, be, nv: (i, 0)),
        scratch_shapes=[pltpu.VMEM((rb, d), F32), pltpu.VMEM((rb, d), BF16)],
    )
    return pl.pallas_call(
        _expert_kernel,
        grid_spec=grid_spec,
        out_shape=jax.ShapeDtypeStruct((n_rows, d), F32),
        compiler_params=_cparams("arbitrary", "arbitrary"),
        name="moe_experts",
    )(blk_expert, n_valid, xs, w_gate.astype(BF16), w_up.astype(BF16), w_down.astype(BF16))


def _combine_kernel(pos_ref, posn_ref, route_ref, x_ref, mod_ref, lg_ref, lb_ref, y_hbm, o_ref, buf, sem,
                    *, tb, alpha):
    i = pl.program_id(0)
    n = pl.num_programs(0)
    slot = i % 2

    def issue(p_ref, s):
        def body(r, carry):
            for kk in range(2):
                _row_copy(y_hbm, p_ref[0, 0, 2 * r + kk], buf.at[s, kk], r, sem.at[s]).start()
            return carry
        lax.fori_loop(0, tb, body, 0, unroll=8)

    @pl.when(i == 0)
    def _():
        issue(pos_ref, 0)

    @pl.when(i + 1 < n)
    def _():
        issue(posn_ref, 1 - slot)

    for kk in range(2):
        pltpu.make_async_copy(y_hbm.at[pl.ds(0, tb), :], buf.at[slot, kk], sem.at[slot]).wait()

    rec = route_ref[...]
    f = rec[:, RT_W1:RT_W1 + 1] * buf[slot, 0] + rec[:, RT_W2:RT_W2 + 1] * buf[slot, 1]
    gt = mod_ref[0, 5:6, :]
    o_ref[...] = _layer_norm(alpha * x_ref[...] + (1.0 + gt) * f, lg_ref[...], lb_ref[...])


def _combine(ys, pos, route, x1, mod_l, ln_g, ln_b, alpha, tb, seq):
    n, d = x1.shape
    nb = n // tb
    per_b = seq // tb
    pos3 = pos.reshape(nb, 1, 2 * tb)
    return pl.pallas_call(
        functools.partial(_combine_kernel, tb=tb, alpha=alpha),
        grid=(nb,),
        in_specs=[
            pl.BlockSpec((1, 1, 2 * tb), lambda i: (i, 0, 0), memory_space=pltpu.SMEM),
            pl.BlockSpec((1, 1, 2 * tb), lambda i: (jnp.minimum(i + 1, nb - 1), 0, 0), memory_space=pltpu.SMEM),
            pl.BlockSpec((tb, ROUTER_LANES), lambda i: (i, 0)),
            pl.BlockSpec((tb, d), lambda i: (i, 0)),
            pl.BlockSpec((1, N_MOD, d), lambda i: (i // per_b, 0, 0)),
            pl.BlockSpec((1, d), lambda i: (0, 0)),
            pl.BlockSpec((1, d), lambda i: (0, 0)),
            pl.BlockSpec(memory_space=pl.ANY),
        ],
        out_specs=pl.BlockSpec((tb, d), lambda i: (i, 0)),
        out_shape=jax.ShapeDtypeStruct((n, d), F32),
        scratch_shapes=[pltpu.VMEM((2, 2, tb, d), F32), pltpu.SemaphoreType.DMA((2,))],
        compiler_params=_cparams("arbitrary"),
        name="moe_combine",
    )(pos3, pos3, route, x1, mod_l, ln_g.reshape(1, d), ln_b.reshape(1, d), ys)


def _moe_ffn(h2, x1, route, count, mod_l, w_gate, w_up, w_down, ln_g, ln_b, alpha, rb, tf, tb):
    bsz, seq, d = x1.shape
    n = bsz * seq
    n_rows = 2 * n + N_EXPERTS * rb
    route2 = route.reshape(n, ROUTER_LANES)
    pos, blk_expert, n_valid, fill = _route_tables(route2, count, rb, n_rows // rb)
    xs = _dispatch(h2.reshape(n, d), pos, fill, n_rows, rb)
    ys = _experts(xs, blk_expert, n_valid, w_gate, w_up, w_down, rb, tf)
    out = _combine(ys, pos, route2, x1.reshape(n, d), mod_l, ln_g, ln_b, alpha, tb, seq)
    return out.reshape(bsz, seq, d)


def _tile(n, want):
    t = min(n, want)
    assert n % t == 0, (n, want)
    return t


def kernel(x, c, mod_w, mod_b, w_in, w_out, s5_lam_re, s5_lam_im, s5_log_dt, s5_b_re, s5_b_im, s5_c_re, s5_c_im, s5_d, s5_w_glu, s5_b_glu, gla_w_alpha_up, gla_b_alpha, gla_head_gain, ln_mix_g, ln_mix_b, ffn_w_gate, ffn_w_up, ffn_w_down, moe_w_router, moe_b_router, moe_w_gate, moe_w_up, moe_w_down, ln_ffn_g, ln_ffn_b):
    bsz, seq, d = x.shape
    depth = mod_w.shape[0]
    alpha = (2.0 * depth) ** 0.25
    tm = _tile(seq, 512)
    tt = _tile(seq, 16)
    tg = _tile(seq, 256)
    tm_ffn = _tile(seq, 1024)
    tf = _tile(D_FF, 512)
    tb = _tile(seq, 256)

    mod =_modulation(c, mod_w, mod_b).reshape(depth, bsz, N_MOD, d)
    for layer in range(depth):
        mod_l = mod[layer]
        w_gate_fold = _gate_fold(w_in[layer][:, IN_P[0]:], gla_w_alpha_up[layer])
        w_cat = jnp.concatenate([w_in[layer][:, :IN_P[0]], w_gate_fold], axis=1).astype(BF16)
        u_tm, q, k, v, g_out, pre = _in_projection(x, mod_l, w_cat, tm)

        s5w = _s5_weights(s5_lam_re[layer], s5_lam_im[layer], s5_log_dt[layer], s5_b_re[layer], s5_b_im[layer],
                          s5_c_re[layer], s5_c_im[layer])
        y_s5 = _s5_group(u_tm.reshape(seq, bsz, S5_WIDTH), *s5w, s5_d[layer], s5_w_glu[layer], s5_b_glu[layer], tt)
        y_s5 = y_s5.reshape(seq, bsz * S5_WIDTH)
        y_gla = _gla_group(q, k, v, g_out, pre, gla_b_alpha[layer], gla_head_gain[layer], tg)

        i = layer // 2
        if layer % 2 == 0:
            x1, h2 = _out_projection(y_s5, y_gla, x, mod_l, w_out[layer], ln_mix_g[layer], ln_mix_b[layer], alpha, tm)
            x = _dense_ffn(h2, x1, mod_l, ffn_w_gate[i], ffn_w_up[i], ffn_w_down[i],
                           ln_ffn_g[layer], ln_ffn_b[layer], alpha, tm_ffn, tf)
        else:
            x1, h2, route, count = _out_projection(y_s5, y_gla, x, mod_l, w_out[layer], ln_mix_g[layer],
                                                   ln_mix_b[layer], alpha, tm,
                                                   router=(moe_w_router[i], moe_b_router[i]))
            x = _moe_ffn(h2, x1, route, count, mod_l, moe_w_gate[i], moe_w_up[i], moe_w_down[i],
                         ln_ffn_g[layer], ln_ffn_b[layer], alpha, tm_ffn, tf, tb)
    return x
```

```python
import functools
import math

import jax
import jax.numpy as jnp
from jax import lax
from jax.experimental import pallas as pl
from jax.experimental.pallas import tpu as pltpu

F32 = jnp.float32
BF16 = jnp.bfloat16

D_MODEL = 1024
S5_WIDTH = 512
S5_GROUP_CH = 16
S5_GROUPS = 32
S5_STATE = 64
S5_LANES = S5_GROUPS * S5_STATE
GLA_WIDTH = 512
GLA_HEADS = 4
GLA_DK = 256
GLA_HEAD_K = 64
GLA_HEAD_V = 128
GLA_GATE_RANK = 16
GLA_GATE_TAU = 16.0
GLA_CHUNK = 64
D_FF = 3584
N_EXPERTS = 8
N_MOD = 6
LN_EPS = 1e-5
RMS_EPS = 1e-6
ROUTER_LANES = 128
NEG_BIG = -1e30

VMEM_LIMIT = 56 * 1024 * 1024


def _cparams(*sem):
    return pltpu.CompilerParams(dimension_semantics=sem, vmem_limit_bytes=VMEM_LIMIT)


def _dot(a, b):
    return jnp.dot(a, b, preferred_element_type=F32)


def _split2(a):
    a1 = a.astype(BF16)
    a2 = (a - a1.astype(F32)).astype(BF16)
    return a1, a2


def _split3(a):
    a1 = a.astype(BF16)
    r1 = a - a1.astype(F32)
    a2 = r1.astype(BF16)
    a3 = (r1 - a2.astype(F32)).astype(BF16)
    return a1, a2, a3


def _dot_f32(a, b):
    a1, a2, a3 = _split3(a)
    b1, b2, b3 = _split3(b)
    lo = _dot(a1, b3) + _dot(a2, b2) + _dot(a3, b1)
    mid = _dot(a1, b2) + _dot(a2, b1)
    return lo + mid + _dot(a1, b1)


def _dot_f32x3(a, b):
    a1, a2 = _split2(a)
    b1, b2 = _split2(b)
    return (_dot(a1, b2) + _dot(a2, b1)) + _dot(a1, b1)


def _silu(x):
    return x * jax.nn.sigmoid(x)


def _layer_norm(r, gain, bias):
    mu = jnp.mean(r, axis=-1, keepdims=True)
    d = r - mu
    var = jnp.mean(d * d, axis=-1, keepdims=True)
    return d * lax.rsqrt(var + LN_EPS) * gain + bias


def _mod_kernel(c_ref, w_ref, b_ref, o_ref):
    o_ref[0] = _dot_f32(_silu(c_ref[...]), w_ref[0]) + b_ref[0]


def _modulation(c, mod_w, mod_b):
    depth, d, nd = mod_w.shape
    bsz = c.shape[0]
    return pl.pallas_call(
        _mod_kernel,
        grid=(depth, nd // d),
        in_specs=[
            pl.BlockSpec((bsz, d), lambda l, n: (0, 0)),
            pl.BlockSpec((1, d, d), lambda l, n: (l, 0, n)),
            pl.BlockSpec((1, 1, d), lambda l, n: (l, 0, n)),
        ],
        out_specs=pl.BlockSpec((1, bsz, d), lambda l, n: (l, 0, n)),
        out_shape=jax.ShapeDtypeStruct((depth, bsz, nd), F32),
        compiler_params=_cparams("parallel", "parallel"),
        name="modulation",
    )(c, mod_w, mod_b.reshape(depth, 1, nd))


def _gate_fold_kernel(wl_ref, wu_ref, o_ref):
    o_ref[...] = _dot_f32(wl_ref[...], wu_ref[...])


def _gate_fold(w_low, w_up):
    d = w_low.shape[0]
    pad = 128 - GLA_GATE_RANK
    wl = jnp.pad(w_low, ((0, 0), (0, pad)))
    wu = jnp.pad(w_up, ((0, pad), (0, 0)))
    return pl.pallas_call(
        _gate_fold_kernel,
        out_shape=jax.ShapeDtypeStruct((d, GLA_DK), F32),
        name="gate_fold",
    )(wl, wu)


def _s5_disc_kernel(lre_ref, lim_ref, ldt_ref, bre_ref, bim_ref, lam_ref, bbre_ref, bbim_ref):
    lre = lre_ref[...]
    lim = lim_ref[...]
    dt = jnp.exp(ldt_ref[...])
    mag = jnp.exp(lre * dt)
    ang = lim * dt
    are = mag * jnp.cos(ang)
    aim = mag * jnp.sin(ang)
    lam_ref[0] = are
    lam_ref[1] = aim
    den = lre * lre + lim * lim
    nre = are - 1.0
    cre = (nre * lre + aim * lim) / den
    cim = (aim * lre - nre * lim) / den
    for h in range(S5_GROUP_CH):
        bre = bre_ref[h]
        bim = bim_ref[h]
        bbre_ref[h] = cre * bre - cim * bim
        bbim_ref[h] = cre * bim + cim * bre


def _s5_discretise(lam_re, lam_im, log_dt, b_re, b_im):
    g, p = lam_re.shape
    hch = b_re.shape[-1]
    bre_t = jnp.transpose(b_re, (2, 0, 1))
    bim_t = jnp.transpose(b_im, (2, 0, 1))
    lam, bbre, bbim = pl.pallas_call(
        _s5_disc_kernel,
        out_shape=(
            jax.ShapeDtypeStruct((2, g, p), F32),
            jax.ShapeDtypeStruct((hch, g, p), F32),
            jax.ShapeDtypeStruct((hch, g, p), F32),
        ),
        name="s5_discretise",
    )(lam_re, lam_im, log_dt.reshape(g, 1), bre_t, bim_t)
    return lam, bbre, bbim


def _s5_weights(lam_re, lam_im, log_dt, b_re, b_im, c_re, c_im):
    g, p, hch = S5_GROUPS, S5_STATE, S5_GROUP_CH
    lam, bbre, bbim = _s5_discretise(lam_re, lam_im, log_dt, b_re, b_im)
    lam_rows = lam.reshape(2, g * p)
    eye = jnp.eye(g, dtype=F32)

    def in_blocks(bb):
        dense = jnp.einsum('hgp,gk->ghkp', bb, eye).reshape(g * hch, g * p)
        blocks = [dense[128 * (n // 2):128 * (n // 2) + 128, 256 * n:256 * (n + 1)] for n in range(8)]
        return jnp.stack(blocks).astype(BF16)

    def out_blocks(cc, sign):
        dense = jnp.einsum('ghp,gk->gpkh', cc, eye).reshape(g * p, g * hch) * sign
        blocks = [dense[1024 * m:1024 * (m + 1), 256 * m:256 * (m + 1)] for m in range(2)]
        return jnp.stack(blocks).astype(BF16)

    return lam_rows, in_blocks(bbre), in_blocks(bbim), out_blocks(c_re, 1.0), out_blocks(c_im, -1.0)


IN_U = (0, 512)
IN_Q = (512, 768)
IN_K = (768, 1024)
IN_V = (1024, 1536)
IN_G = (1536, 2048)
IN_P = (2048, 2304)


def _inproj_kernel(x_ref, mod_ref, w_ref, u_ref, q_ref, k_ref, v_ref, g_ref, p_ref):
    sh = mod_ref[0, 0:1, :]
    sc = mod_ref[0, 1:2, :]
    h = (x_ref[0] * (1.0 + sc) + sh).astype(BF16)
    u_ref[...] = _dot(h, w_ref[:, IN_U[0]:IN_U[1]]).astype(BF16)
    q_ref[0] = _dot(h, w_ref[:, IN_Q[0]:IN_Q[1]]).astype(BF16)
    k_ref[0] = _dot(h, w_ref[:, IN_K[0]:IN_K[1]]).astype(BF16)
    v_ref[0] = _dot(h, w_ref[:, IN_V[0]:IN_V[1]]).astype(BF16)
    g_ref[0] = _dot(h, w_ref[:, IN_G[0]:IN_G[1]]).astype(BF16)
    p_ref[0] = _dot(h, w_ref[:, IN_P[0]:IN_P[1]])


def _in_projection(x, mod_l, w_cat, tm):
    bsz, seq, d = x.shape
    ncol = w_cat.shape[1]
    bspec = lambda w: pl.BlockSpec((1, tm, w), lambda b, t: (b, t, 0))
    return pl.pallas_call(
        _inproj_kernel,
        grid=(bsz, seq // tm),
        in_specs=[
            pl.BlockSpec((1, tm, d), lambda b, t: (b, t, 0)),
            pl.BlockSpec((1, N_MOD, d), lambda b, t: (b, 0, 0)),
            pl.BlockSpec((d, ncol), lambda b, t: (0, 0)),
        ],
        out_specs=[
            pl.BlockSpec((tm, S5_WIDTH), lambda b, t: (t, b)),
            bspec(GLA_DK), bspec(GLA_DK), bspec(GLA_WIDTH), bspec(GLA_WIDTH), bspec(GLA_DK),
        ],
        out_shape=[
            jax.ShapeDtypeStruct((seq, bsz * S5_WIDTH), BF16),
            jax.ShapeDtypeStruct((bsz, seq, GLA_DK), BF16),
            jax.ShapeDtypeStruct((bsz, seq, GLA_DK), BF16),
            jax.ShapeDtypeStruct((bsz, seq, GLA_WIDTH), BF16),
            jax.ShapeDtypeStruct((bsz, seq, GLA_WIDTH), BF16),
            jax.ShapeDtypeStruct((bsz, seq, GLA_DK), F32),
        ],
        compiler_params=_cparams("parallel", "parallel"),
        name="in_projection",
    )(x, mod_l, w_cat)


S5_SCAN_LANES = 256


def _s5_kernel(u_ref, lam_ref, wbre_ref, wbim_ref, wcre_ref, wcim_ref, d_ref, wglu_ref, bglu_ref,
               o_ref, hre, him, st_re, st_im, *, tt, nb):
    @pl.when(pl.program_id(0) == 0)
    def _():
        st_re[...] = jnp.zeros_like(st_re)
        st_im[...] = jnp.zeros_like(st_im)

    rows = tt * nb
    ub = u_ref[...].reshape(rows, S5_WIDTH)
    for n in range(8):
        lhs = ub[:, 128 * (n // 2):128 * (n // 2) + 128]
        hre[:, 256 * n:256 * (n + 1)] = _dot(lhs, wbre_ref[n])
        him[:, 256 * n:256 * (n + 1)] = _dot(lhs, wbim_ref[n])

    for c in range(S5_LANES // S5_SCAN_LANES):
        ls = slice(c * S5_SCAN_LANES, (c + 1) * S5_SCAN_LANES)
        lr = jnp.broadcast_to(lam_ref[0:1, ls], (nb, S5_SCAN_LANES))
        li = jnp.broadcast_to(lam_ref[1:2, ls], (nb, S5_SCAN_LANES))
        hr = st_re[:, ls]
        hi = st_im[:, ls]
        for t in range(tt):
            rs = slice(t * nb, (t + 1) * nb)
            nr = lr * hr - li * hi + hre[rs, ls]
            ni = lr * hi + li * hr + him[rs, ls]
            hre[rs, ls] = nr
            him[rs, ls] = ni
            hr, hi = nr, ni
        st_re[:, ls] = hr
        st_im[:, ls] = hi

    ys = []
    for m in range(2):
        ks = slice(1024 * m, 1024 * (m + 1))
        ys.append(_dot(hre[:, ks].astype(BF16), wcre_ref[m]) + _dot(him[:, ks].astype(BF16), wcim_ref[m]))
    y = jnp.concatenate(ys, axis=-1) + d_ref[...] * ub.astype(F32)
    y = 0.5 * y * (1.0 + jnp.tanh(math.sqrt(2.0 / math.pi) * (y + 0.044715 * (y * y * y))))
    z = y * jax.nn.sigmoid(_dot(y.astype(BF16), wglu_ref[...]) + bglu_ref[...])
    o_ref[...] = z.astype(BF16).reshape(tt, nb, S5_WIDTH)


def _s5_group(u_tm, lam_rows, wbre, wbim, wcre, wcim, d_skip, w_glu, b_glu, tt):
    seq, nb, width = u_tm.shape
    rows = tt * nb
    full = lambda a: pl.BlockSpec(a.shape, lambda t: (0,) * a.ndim)
    d2 = d_skip.reshape(1, width)
    b2 = b_glu.reshape(1, width)
    wg = w_glu.astype(BF16)
    return pl.pallas_call(
        functools.partial(_s5_kernel, tt=tt, nb=nb),
        grid=(seq // tt,),
        in_specs=[
            pl.BlockSpec((tt, nb, width), lambda t: (t, 0, 0)),
            full(lam_rows), full(wbre), full(wbim), full(wcre), full(wcim), full(d2), full(wg), full(b2),
        ],
        out_specs=pl.BlockSpec((tt, nb, width), lambda t: (t, 0, 0)),
        out_shape=jax.ShapeDtypeStruct((seq, nb, width), BF16),
        scratch_shapes=[
            pltpu.VMEM((rows, S5_LANES), F32),
            pltpu.VMEM((rows, S5_LANES), F32),
            pltpu.VMEM((nb, S5_LANES), F32),
            pltpu.VMEM((nb, S5_LANES), F32),
        ],
        compiler_params=_cparams("arbitrary"),
        name="s5_group",
    )(u_tm, lam_rows, wbre, wbim, wcre, wcim, d2, wg, b2)


def _gla_kernel(q_ref, k_ref, v_ref, g_ref, p_ref, ba_ref, gain_ref, o_ref, s_ref, oi_ref, upd_ref, *, nchunk):
    @pl.when(pl.program_id(1) == 0)
    def _():
        s_ref[...] = jnp.zeros_like(s_ref)

    c = GLA_CHUNK
    nh = GLA_HEADS
    srow = lax.broadcasted_iota(jnp.int32, (nh * c, c), 0)
    scol = lax.broadcasted_iota(jnp.int32, (nh * c, c), 1)
    causal = (srow & (c - 1)) >= scol
    lane = lax.broadcasted_iota(jnp.int32, (1, GLA_DK), 1)
    head_lanes = [(lane >= GLA_HEAD_K * h) & (lane < GLA_HEAD_K * (h + 1)) for h in range(nh)]
    vrow = lax.broadcasted_iota(jnp.int32, (2 * GLA_HEAD_V, 2 * GLA_HEAD_K), 0)
    kcol = lax.broadcasted_iota(jnp.int32, (2 * GLA_HEAD_V, 2 * GLA_HEAD_K), 1)
    same_head = (vrow < GLA_HEAD_V) == (kcol < GLA_HEAD_K)
    pair_k = [slice(2 * GLA_HEAD_K * pr, 2 * GLA_HEAD_K * (pr + 1)) for pr in range(nh // 2)]
    pair_v = [slice(2 * GLA_HEAD_V * pr, 2 * GLA_HEAD_V * (pr + 1)) for pr in range(nh // 2)]
    scale = GLA_HEAD_K ** -0.5
    mid = c // 2 - 1

    t = nchunk * c
    trow = lax.broadcasted_iota(jnp.int32, (t, t), 0)
    tcol = lax.broadcasted_iota(jnp.int32, (t, t), 1)
    chunk_tri = jnp.where((tcol <= trow) & (tcol >= (trow & -c)), 1.0, 0.0).astype(BF16)
    pre = p_ref[0] + ba_ref[...]
    log_alpha = (jnp.minimum(pre, 0.0) - jnp.log1p(jnp.exp(-jnp.abs(pre)))) * (1.0 / GLA_GATE_TAU)
    a1, a2, a3 = _split3(log_alpha)
    b = (_dot(chunk_tri, a3) + _dot(chunk_tri, a2) + _dot(chunk_tri, a1)).reshape(nchunk, c, GLA_DK)
    b_mid = b[:, mid:mid + 1, :]
    b_last = b[:, c - 1:c, :]
    q = (q_ref[0].astype(F32) * scale).reshape(nchunk, c, GLA_DK)
    k = k_ref[0].astype(F32).reshape(nchunk, c, GLA_DK)
    q_in = (q * jnp.exp(b)).astype(BF16)
    q_e = q * jnp.exp(b - b_mid)
    k_e = (k * jnp.exp(b_mid - b)).astype(BF16)
    k_d = (k * jnp.exp(b_last - b)).astype(BF16)
    decay = jnp.exp(b_last)
    for ci in range(nchunk):
        sl = slice(ci * c, (ci + 1) * c)
        q_stack = jnp.concatenate([jnp.where(head_lanes[h], q_e[ci], 0.0) for h in range(nh)], axis=0).astype(BF16)
        scores = lax.dot_general(q_stack, k_e[ci], (((1,), (1,)), ((), ())),
                                 preferred_element_type=F32)
        scores = jnp.where(causal, scores, 0.0).astype(BF16)
        v_all = v_ref[0, sl, :]
        pv = _dot(scores, v_all)
        for h in range(nh):
            os_ = slice(GLA_HEAD_V * h, GLA_HEAD_V * (h + 1))
            oi_ref[sl, os_] = pv[c * h:c * (h + 1), os_]
        for pr in range(nh // 2):
            upd = lax.dot_general(v_all[:, pair_v[pr]], k_d[ci][:, pair_k[pr]], (((0,), (0,)), ((), ())),
                                  preferred_element_type=F32)
            upd_ref[ci, pr] = jnp.where(same_head, upd, 0.0)

    for pr in range(nh // 2):
        s_t = s_ref[pr]
        for ci in range(nchunk):
            sl = slice(ci * c, (ci + 1) * c)
            oi_ref[sl, pair_v[pr]] += lax.dot_general(q_in[ci][:, pair_k[pr]], s_t.astype(BF16),
                                                      (((1,), (1,)), ((), ())), preferred_element_type=F32)
            s_t = decay[ci][:, pair_k[pr]] * s_t + upd_ref[ci, pr]
        s_ref[pr] = s_t

    for h in range(nh):
        os_ = slice(GLA_HEAD_V * h, GLA_HEAD_V * (h + 1))
        o_h = oi_ref[:, os_]
        ms = jnp.mean(o_h * o_h, axis=-1, keepdims=True)
        o_n = o_h * lax.rsqrt(ms + RMS_EPS) * gain_ref[:, os_]
        o_ref[0, :, os_] = (o_n * _silu(g_ref[0, :, os_].astype(F32))).astype(BF16)


def _gla_group(q, k, v, g_out, pre, b_alpha, head_gain, tg):
    bsz, seq, _ = q.shape
    blk = lambda w: pl.BlockSpec((1, tg, w), lambda b, t: (b, t, 0))
    ba = b_alpha.reshape(1, GLA_DK)
    gain = head_gain.reshape(1, GLA_WIDTH)
    return pl.pallas_call(
        functools.partial(_gla_kernel, nchunk=tg // GLA_CHUNK),
        grid=(bsz, seq // tg),
        in_specs=[
            blk(GLA_DK), blk(GLA_DK), blk(GLA_WIDTH), blk(GLA_WIDTH), blk(GLA_DK),
            pl.BlockSpec((1, GLA_DK), lambda b, t: (0, 0)),
            pl.BlockSpec((1, GLA_WIDTH), lambda b, t: (0, 0)),
        ],
        out_specs=blk(GLA_WIDTH),
        out_shape=jax.ShapeDtypeStruct((bsz, seq, GLA_WIDTH), BF16),
        scratch_shapes=[
            pltpu.VMEM((GLA_HEADS // 2, 2 * GLA_HEAD_V, 2 * GLA_HEAD_K), F32),
            pltpu.VMEM((tg, GLA_WIDTH), F32),
            pltpu.VMEM((tg // GLA_CHUNK, GLA_HEADS // 2, 2 * GLA_HEAD_V, 2 * GLA_HEAD_K), F32),
        ],
        compiler_params=_cparams("parallel", "arbitrary"),
        name="gla_group",
    )(q, k, v, g_out, pre, ba, gain)


def _mix_out(ys_ref, yg_ref, x_ref, mod_ref, w_ref, lg_ref, lb_ref, alpha):
    y = _dot(ys_ref[...], w_ref[0:S5_WIDTH, :]) + _dot(yg_ref[0], w_ref[S5_WIDTH:, :])
    gt = mod_ref[0, 2:3, :]
    x1 = _layer_norm(alpha * x_ref[0] + (1.0 + gt) * y, lg_ref[...], lb_ref[...])
    h2 = x1 * (1.0 + mod_ref[0, 4:5, :]) + mod_ref[0, 3:4, :]
    return x1, h2


def _outproj_kernel(ys_ref, yg_ref, x_ref, mod_ref, w_ref, lg_ref, lb_ref, x1_ref, h2_ref, *, alpha):
    x1, h2 = _mix_out(ys_ref, yg_ref, x_ref, mod_ref, w_ref, lg_ref, lb_ref, alpha)
    x1_ref[0] = x1
    h2_ref[0] = h2.astype(BF16)


RT_E1, RT_E2, RT_R1, RT_R2, RT_W1, RT_W2 = range(6)


def _top2_route(logits, count):
    tm = logits.shape[0]
    lane = lax.broadcasted_iota(jnp.int32, logits.shape, 1).astype(F32)
    l0 = jnp.where(lane < N_EXPERTS, logits, NEG_BIG)
    m1 = jnp.max(l0, axis=-1, keepdims=True)
    i1 = jnp.min(jnp.where(l0 == m1, lane, float(ROUTER_LANES)), axis=-1, keepdims=True)
    sel1 = lane == i1
    l1 = jnp.where(sel1, NEG_BIG, l0)
    m2 = jnp.max(l1, axis=-1, keepdims=True)
    i2 = jnp.min(jnp.where(l1 == m2, lane, float(ROUTER_LANES)), axis=-1, keepdims=True)
    sel2 = lane == i2
    e2 = jnp.exp(m2 - m1)
    w1 = 1.0 / (1.0 + e2)
    w2 = e2 / (1.0 + e2)
    chosen = jnp.where(sel1, 1.0, 0.0) + jnp.where(sel2, 1.0, 0.0)
    row = lax.broadcasted_iota(jnp.int32, (tm, tm), 0)
    col = lax.broadcasted_iota(jnp.int32, (tm, tm), 1)
    earlier = jnp.where(row > col, 1.0, 0.0).astype(BF16)
    before = _dot(earlier, chosen.astype(BF16)) + count
    r1 = jnp.sum(jnp.where(sel1, before, 0.0), axis=-1, keepdims=True)
    r2 = jnp.sum(jnp.where(sel2, before, 0.0), axis=-1, keepdims=True)
    rec = jnp.zeros_like(logits)
    for ln, val in ((RT_E1, i1), (RT_E2, i2), (RT_R1, r1), (RT_R2, r2), (RT_W1, w1), (RT_W2, w2)):
        rec = jnp.where(lane == float(ln), val, rec)
    return rec, count + jnp.sum(chosen, axis=0, keepdims=True)


def _outproj_router_kernel(ys_ref, yg_ref, x_ref, mod_ref, w_ref, lg_ref, lb_ref, wr_ref, br_ref,
                           x1_ref, route_ref, count_ref, cnt, *, alpha):
    @pl.when((pl.program_id(0) == 0) & (pl.program_id(1) == 0))
    def _():
        cnt[...] = jnp.zeros_like(cnt)

    x1, h2 = _mix_out(ys_ref, yg_ref, x_ref, mod_ref, w_ref, lg_ref, lb_ref, alpha)
    x1_ref[0] = x1
    logits = _dot_f32x3(h2, wr_ref[...]) + br_ref[...]
    rec, new_count = _top2_route(logits, cnt[...])
    route_ref[0] = rec
    cnt[...] = new_count
    count_ref[...] = jnp.broadcast_to(new_count, count_ref.shape)


def _out_projection(y_s5_tm, y_gla, x, mod_l, w_out, ln_g, ln_b, alpha, tm, router=None):
    bsz, seq, d = x.shape
    row = lambda a: a.reshape(1, -1)
    in_specs = [
        pl.BlockSpec((tm, S5_WIDTH), lambda b, t: (t, b)),
        pl.BlockSpec((1, tm, GLA_WIDTH), lambda b, t: (b, t, 0)),
        pl.BlockSpec((1, tm, d), lambda b, t: (b, t, 0)),
        pl.BlockSpec((1, N_MOD, d), lambda b, t: (b, 0, 0)),
        pl.BlockSpec((d, d), lambda b, t: (0, 0)),
        pl.BlockSpec((1, d), lambda b, t: (0, 0)),
        pl.BlockSpec((1, d), lambda b, t: (0, 0)),
    ]
    out_specs = [
        pl.BlockSpec((1, tm, d), lambda b, t: (b, t, 0)),
        pl.BlockSpec((1, tm, d), lambda b, t: (b, t, 0)),
    ]
    out_shape = [jax.ShapeDtypeStruct((bsz, seq, d), F32), jax.ShapeDtypeStruct((bsz, seq, d), BF16)]
    args = [y_s5_tm, y_gla, x, mod_l, w_out.astype(BF16), row(ln_g), row(ln_b)]
    scratch = []
    sem = ("parallel", "parallel")
    if router is None:
        body = functools.partial(_outproj_kernel, alpha=alpha)
    else:
        w_router, b_router = router
        pad = ROUTER_LANES - N_EXPERTS
        in_specs += [pl.BlockSpec((d, ROUTER_LANES), lambda b, t: (0, 0)),
                     pl.BlockSpec((1, ROUTER_LANES), lambda b, t: (0, 0))]
        out_specs = out_specs[:1] + [pl.BlockSpec((1, tm, ROUTER_LANES), lambda b, t: (b, t, 0)),
                                     pl.BlockSpec((8, ROUTER_LANES), lambda b, t: (0, 0))]
        out_shape = out_shape[:1] + [jax.ShapeDtypeStruct((bsz, seq, ROUTER_LANES), F32),
                                     jax.ShapeDtypeStruct((8, ROUTER_LANES), F32)]
        args += [jnp.pad(w_router, ((0, 0), (0, pad))), jnp.pad(row(b_router), ((0, 0), (0, pad)))]
        body = functools.partial(_outproj_router_kernel, alpha=alpha)
        scratch = [pltpu.VMEM((1, ROUTER_LANES), F32)]
        sem = ("arbitrary", "arbitrary")
    return pl.pallas_call(
        body,
        grid=(bsz, seq // tm),
        in_specs=in_specs,
        out_specs=out_specs,
        out_shape=out_shape,
        scratch_shapes=scratch,
        compiler_params=_cparams(*sem),
        name="out_projection",
    )(*args)


def _ffn_kernel(h_ref, x_ref, mod_ref, wg_ref, wu_ref, wd_ref, lg_ref, lb_ref, o_ref, acc_ref, *, alpha):
    j = pl.program_id(1)

    @pl.when(j == 0)
    def _():
        acc_ref[...] = jnp.zeros_like(acc_ref)

    h = h_ref[...]
    a = _dot(h, wg_ref[...])
    mid = (_silu(a) * _dot(h, wu_ref[...])).astype(BF16)
    acc_ref[...] += _dot(mid, wd_ref[...])

    @pl.when(j == pl.num_programs(1) - 1)
    def _():
        gt = mod_ref[0, 5:6, :]
        o_ref[...] = _layer_norm(alpha * x_ref[...] + (1.0 + gt) * acc_ref[...], lg_ref[...], lb_ref[...])


def _dense_ffn(h2, x1, mod_l, w_gate, w_up, w_down, ln_g, ln_b, alpha, tm, tf):
    bsz, seq, d = x1.shape
    n = bsz * seq
    dff = w_gate.shape[1]
    per_b = seq // tm
    return pl.pallas_call(
        functools.partial(_ffn_kernel, alpha=alpha),
        grid=(n // tm, dff // tf),
        in_specs=[
            pl.BlockSpec((tm, d), lambda i, j: (i, 0)),
            pl.BlockSpec((tm, d), lambda i, j: (i, 0)),
            pl.BlockSpec((1, N_MOD, d), lambda i, j: (i // per_b, 0, 0)),
            pl.BlockSpec((d, tf), lambda i, j: (0, j)),
            pl.BlockSpec((d, tf), lambda i, j: (0, j)),
            pl.BlockSpec((tf, d), lambda i, j: (j, 0)),
            pl.BlockSpec((1, d), lambda i, j: (0, 0)),
            pl.BlockSpec((1, d), lambda i, j: (0, 0)),
        ],
        out_specs=pl.BlockSpec((tm, d), lambda i, j: (i, 0)),
        out_shape=jax.ShapeDtypeStruct((n, d), F32),
        scratch_shapes=[pltpu.VMEM((tm, d), F32)],
        compiler_params=_cparams("parallel", "arbitrary"),
        name="dense_ffn",
    )(h2.reshape(n, d), x1.reshape(n, d), mod_l, w_gate.astype(BF16), w_up.astype(BF16), w_down.astype(BF16),
      ln_g.reshape(1, d), ln_b.reshape(1, d)).reshape(bsz, seq, d)


def _route_tables(route, count, rb, nblk):
    e = route[:, RT_E1:RT_E2 + 1].astype(jnp.int32)
    rank = route[:, RT_R1:RT_R2 + 1].astype(jnp.int32)
    cnt = count[0, :N_EXPERTS].astype(jnp.int32)
    padded = ((cnt + rb - 1) // rb) * rb
    ends = jnp.cumsum(padded)
    starts = ends - padded
    pos = jnp.sum(jnp.where(e[..., None] == jnp.arange(N_EXPERTS), starts, 0), axis=-1) + rank
    n_valid = ends[-1] // rb
    blk = jnp.minimum(jnp.arange(nblk, dtype=jnp.int32), n_valid - 1)
    blk_expert = jnp.sum((blk[:, None] * rb >= ends[None, :]).astype(jnp.int32), axis=-1)
    blk_expert = jnp.minimum(blk_expert, N_EXPERTS - 1)
    n_rows = jnp.full((1,), nblk * rb, jnp.int32)
    fill = jnp.stack([jnp.concatenate([starts + cnt, ends[-1:]]), jnp.concatenate([ends, n_rows])], axis=1)
    return (pos.astype(jnp.int32), blk_expert.astype(jnp.int32), n_valid.astype(jnp.int32).reshape(1),
            fill.reshape(-1).astype(jnp.int32))


SUBLANES = 8


def _staged_row(ref, tile, sub):
    return ref.at[tile, pl.ds(sub, 1), :]


def _dispatch_kernel(fill_ref, pos_ref, x_ref, mod_ref, xs_hbm, hs_ref, sem, *, tb):
    i = pl.program_id(0)
    h = x_ref[...] * (1.0 + mod_ref[0, 4:5, :]) + mod_ref[0, 3:4, :]
    hs_ref[...] = h.reshape(tb // SUBLANES, SUBLANES, h.shape[-1])

    def issue(r8, carry):
        for sub in range(SUBLANES):
            for kk in range(2):
                dst_row = pos_ref[0, 0, 2 * SUBLANES * r8 + 2 * sub + kk]
                pltpu.make_async_copy(_staged_row(hs_ref, r8, sub), xs_hbm.at[pl.ds(dst_row, 1), :],
                                      sem.at[0]).start()
        return carry

    lax.fori_loop(0, tb // SUBLANES, issue, 0)

    @pl.when(i == 0)
    def _():
        for e in range(N_EXPERTS + 1):
            lo = fill_ref[2 * e]
            hi = fill_ref[2 * e + 1]

            def fill(p, carry):
                pltpu.make_async_copy(_staged_row(hs_ref, 0, 0), xs_hbm.at[pl.ds(p, 1), :], sem.at[1]).start()
                return carry

            def drain(p, carry):
                pltpu.make_async_copy(_staged_row(hs_ref, 0, 0), xs_hbm.at[pl.ds(p, 1), :], sem.at[1]).wait()
                return carry

            lax.fori_loop(lo, hi, fill, 0)
            lax.fori_loop(lo, hi, drain, 0)

    for _ in range(2):
        pltpu.make_async_copy(x_ref, xs_hbm.at[pl.ds(0, tb), :], sem.at[0]).wait()


def _dispatch(x1, mod_l, pos, fill, n_rows, tb, seq):
    n, d = x1.shape
    per_b = seq // tb
    grid_spec = pltpu.PrefetchScalarGridSpec(
        num_scalar_prefetch=1,
        grid=(n // tb,),
        in_specs=[
            pl.BlockSpec((1, 1, 2 * tb), lambda i, f: (i, 0, 0), memory_space=pltpu.SMEM),
            pl.BlockSpec((tb, d), lambda i, f: (i, 0)),
            pl.BlockSpec((1, N_MOD, d), lambda i, f: (i // per_b, 0, 0)),
        ],
        out_specs=pl.BlockSpec(memory_space=pl.ANY),
        scratch_shapes=[pltpu.VMEM((tb // SUBLANES, SUBLANES, d), F32), pltpu.SemaphoreType.DMA((2,))],
    )
    return pl.pallas_call(
        functools.partial(_dispatch_kernel, tb=tb),
        grid_spec=grid_spec,
        out_shape=jax.ShapeDtypeStruct((n_rows, d), F32),
        compiler_params=_cparams("arbitrary"),
        name="moe_dispatch",
    )(fill, pos.reshape(n // tb, 1, 2 * tb), x1, mod_l)


def _expert_kernel(be_ref, nv_ref, xs_ref, wg_ref, wu_ref, wd_ref, y_ref, acc_ref, xb_ref):
    del be_ref
    i = pl.program_id(0)
    j = pl.program_id(1)

    @pl.when(i < nv_ref[0])
    def _():
        @pl.when(j == 0)
        def _():
            acc_ref[...] = jnp.zeros_like(acc_ref)
            xb_ref[...] = xs_ref[...].astype(BF16)

        h = xb_ref[...]
        a = _dot(h, wg_ref[0])
        mid = (_silu(a) * _dot(h, wu_ref[0])).astype(BF16)
        acc_ref[...] += _dot(mid, wd_ref[0])

        @pl.when(j == pl.num_programs(1) - 1)
        def _():
            y_ref[...] = acc_ref[...]

    @pl.when((i >= nv_ref[0]) & (j == 0))
    def _():
        y_ref[...] = jnp.zeros_like(y_ref)


def _experts(xs, blk_expert, n_valid, w_gate, w_up, w_down, rb, tf):
    n_rows, d = xs.shape
    dff = w_gate.shape[2]
    nj = dff // tf
    row_blk = lambda i, j, be, nv: (jnp.minimum(i, nv[0] - 1), 0)
    jj = lambda i, j, nv: jnp.where(i < nv[0], j, nj - 1)
    grid_spec = pltpu.PrefetchScalarGridSpec(
        num_scalar_prefetch=2,
        grid=(n_rows // rb, nj),
        in_specs=[
            pl.BlockSpec((rb, d), row_blk),
            pl.BlockSpec((1, d, tf), lambda i, j, be, nv: (be[i], 0, jj(i, j, nv))),
            pl.BlockSpec((1, d, tf), lambda i, j, be, nv: (be[i], 0, jj(i, j, nv))),
            pl.BlockSpec((1, tf, d), lambda i, j, be, nv: (be[i], jj(i, j, nv), 0)),
        ],
        out_specs=pl.BlockSpec((rb, d), lambda i, j, be, nv: (i, 0)),
        scratch_shapes=[pltpu.VMEM((rb, d), F32), pltpu.VMEM((rb, d), BF16)],
    )
    return pl.pallas_call(
        _expert_kernel,
        grid_spec=grid_spec,
        out_shape=jax.ShapeDtypeStruct((n_rows, d), F32),
        compiler_params=_cparams("arbitrary", "arbitrary"),
        name="moe_experts",
    )(blk_expert, n_valid, xs, w_gate.astype(BF16), w_up.astype(BF16), w_down.astype(BF16))


def _combine_kernel(pos_ref, posn_ref, route_ref, x_ref, mod_ref, lg_ref, lb_ref, y_hbm, o_ref, buf, sem,
                    *, tb, alpha):
    i = pl.program_id(0)
    slot = i % 2
    d = o_ref.shape[-1]
    ntile = tb // SUBLANES

    def issue_tile(p_ref, s, r8):
        for sub in range(SUBLANES):
            for kk in range(2):
                src_row = p_ref[0, 0, 2 * SUBLANES * r8 + 2 * sub + kk]
                pltpu.make_async_copy(y_hbm.at[pl.ds(src_row, 1), :], _staged_row(buf.at[s, kk], r8, sub),
                                      sem.at[s]).start()

    def wait_slot(s):
        for _ in range(2):
            pltpu.make_async_copy(y_hbm.at[pl.ds(0, tb), :], o_ref, sem.at[s]).wait()

    def issue_all(p_ref, s):
        def body(r8, carry):
            issue_tile(p_ref, s, r8)
            return carry
        lax.fori_loop(0, ntile, body, 0)

    @pl.when(i == 0)
    def _():
        issue_all(pos_ref, 0)

    issue_all(posn_ref, 1 - slot)
    wait_slot(slot)
    rec = route_ref[...]
    y0 = buf[slot, 0].reshape(tb, d)
    y1 = buf[slot, 1].reshape(tb, d)
    f = rec[:, RT_W1:RT_W1 + 1] * y0 + rec[:, RT_W2:RT_W2 + 1] * y1
    gt = mod_ref[0, 5:6, :]
    o_ref[...] = _layer_norm(alpha * x_ref[...] + (1.0 + gt) * f, lg_ref[...], lb_ref[...])

    @pl.when(i == pl.num_programs(0) - 1)
    def _():
        wait_slot(1 - slot)


def _combine(ys, pos, route, x1, mod_l, ln_g, ln_b, alpha, tb, seq):
    n, d = x1.shape
    nb = n // tb
    per_b = seq // tb
    pos3 = pos.reshape(nb, 1, 2 * tb)
    return pl.pallas_call(
        functools.partial(_combine_kernel, tb=tb, alpha=alpha),
        grid=(nb,),
        in_specs=[
            pl.BlockSpec((1, 1, 2 * tb), lambda i: (i, 0, 0), memory_space=pltpu.SMEM),
            pl.BlockSpec((1, 1, 2 * tb), lambda i: (jnp.minimum(i + 1, nb - 1), 0, 0), memory_space=pltpu.SMEM),
            pl.BlockSpec((tb, ROUTER_LANES), lambda i: (i, 0)),
            pl.BlockSpec((tb, d), lambda i: (i, 0)),
            pl.BlockSpec((1, N_MOD, d), lambda i: (i // per_b, 0, 0)),
            pl.BlockSpec((1, d), lambda i: (0, 0)),
            pl.BlockSpec((1, d), lambda i: (0, 0)),
            pl.BlockSpec(memory_space=pl.ANY),
        ],
        out_specs=pl.BlockSpec((tb, d), lambda i: (i, 0)),
        out_shape=jax.ShapeDtypeStruct((n, d), F32),
        scratch_shapes=[pltpu.VMEM((2, 2, tb // SUBLANES, SUBLANES, d), F32), pltpu.SemaphoreType.DMA((2,))],
        compiler_params=_cparams("arbitrary"),
        name="moe_combine",
    )(pos3, pos3, route, x1, mod_l, ln_g.reshape(1, d), ln_b.reshape(1, d), ys)


def _moe_ffn(x1, route, count, mod_l, w_gate, w_up, w_down, ln_g, ln_b, alpha, rb, tf, tb):
    bsz, seq, d = x1.shape
    n = bsz * seq
    n_rows = 2 * n + N_EXPERTS * rb
    route2 = route.reshape(n, ROUTER_LANES)
    pos, blk_expert, n_valid, fill = _route_tables(route2, count, rb, n_rows // rb)
    xs = _dispatch(x1.reshape(n, d), mod_l, pos, fill, n_rows, rb, seq)
    ys = _experts(xs, blk_expert, n_valid, w_gate, w_up, w_down, rb, tf)
    out = _combine(ys, pos, route2, x1.reshape(n, d), mod_l, ln_g, ln_b, alpha, tb, seq)
    return out.reshape(bsz, seq, d)


def _tile(n, want):
    t = min(n, want)
    assert n % t == 0, (n, want)
    return t


def kernel(x, c, mod_w, mod_b, w_in, w_out, s5_lam_re, s5_lam_im, s5_log_dt, s5_b_re, s5_b_im, s5_c_re, s5_c_im, s5_d, s5_w_glu, s5_b_glu, gla_w_alpha_up, gla_b_alpha, gla_head_gain, ln_mix_g, ln_mix_b, ffn_w_gate, ffn_w_up, ffn_w_down, moe_w_router, moe_b_router, moe_w_gate, moe_w_up, moe_w_down, ln_ffn_g, ln_ffn_b):
    bsz, seq, d = x.shape
    depth = mod_w.shape[0]
    alpha = (2.0 * depth) ** 0.25
    tm = _tile(seq, 512)
    tt = _tile(seq, 16)
    tg = _tile(seq, 512)
    tm_ffn = _tile(seq, 1024)
    tf = _tile(D_FF, 512)
    tb = _tile(seq, 256)

    mod =_modulation(c, mod_w, mod_b).reshape(depth, bsz, N_MOD, d)
    for layer in range(depth):
        mod_l = mod[layer]
        w_gate_fold = _gate_fold(w_in[layer][:, IN_P[0]:], gla_w_alpha_up[layer])
        w_cat = jnp.concatenate([w_in[layer][:, :IN_P[0]], w_gate_fold], axis=1).astype(BF16)
        u_tm, q, k, v, g_out, pre = _in_projection(x, mod_l, w_cat, tm)

        s5w = _s5_weights(s5_lam_re[layer], s5_lam_im[layer], s5_log_dt[layer], s5_b_re[layer], s5_b_im[layer],
                          s5_c_re[layer], s5_c_im[layer])
        y_s5 = _s5_group(u_tm.reshape(seq, bsz, S5_WIDTH), *s5w, s5_d[layer], s5_w_glu[layer], s5_b_glu[layer], tt)
        y_s5 = y_s5.reshape(seq, bsz * S5_WIDTH)
        y_gla = _gla_group(q, k, v, g_out, pre, gla_b_alpha[layer], gla_head_gain[layer], tg)

        i = layer // 2
        if layer % 2 == 0:
            x1, h2 = _out_projection(y_s5, y_gla, x, mod_l, w_out[layer], ln_mix_g[layer], ln_mix_b[layer], alpha, tm)
            x = _dense_ffn(h2, x1, mod_l, ffn_w_gate[i], ffn_w_up[i], ffn_w_down[i],
                           ln_ffn_g[layer], ln_ffn_b[layer], alpha, tm_ffn, tf)
        else:
            x1, route, count = _out_projection(y_s5, y_gla, x, mod_l, w_out[layer], ln_mix_g[layer],
                                                   ln_mix_b[layer], alpha, tm,
                                                   router=(moe_w_router[i], moe_b_router[i]))
            x = _moe_ffn(x1, route, count, mod_l, moe_w_gate[i], moe_w_up[i], moe_w_down[i],
                         ln_ffn_g[layer], ln_ffn_b[layer], alpha, tm_ffn, tf, tb)
    return x
```

```python
import functools
import math

import jax
import jax.numpy as jnp
from jax import lax
from jax.experimental import pallas as pl
from jax.experimental.pallas import tpu as pltpu

F32 = jnp.float32
BF16 = jnp.bfloat16

D_MODEL = 1024
S5_WIDTH = 512
S5_GROUP_CH = 16
S5_GROUPS = 32
S5_STATE = 64
S5_LANES = S5_GROUPS * S5_STATE
GLA_WIDTH = 512
GLA_HEADS = 4
GLA_DK = 256
GLA_HEAD_K = 64
GLA_HEAD_V = 128
GLA_GATE_RANK = 16
GLA_GATE_TAU = 16.0
GLA_CHUNK = 64
D_FF = 3584
N_EXPERTS = 8
N_MOD = 6
LN_EPS = 1e-5
RMS_EPS = 1e-6
ROUTER_LANES = 128
NEG_BIG = -1e30

VMEM_LIMIT = 56 * 1024 * 1024


def _cparams(*sem):
    return pltpu.CompilerParams(dimension_semantics=sem, vmem_limit_bytes=VMEM_LIMIT)


def _dot(a, b):
    return jnp.dot(a, b, preferred_element_type=F32)


def _split2(a):
    a1 = a.astype(BF16)
    a2 = (a - a1.astype(F32)).astype(BF16)
    return a1, a2


def _split3(a):
    a1 = a.astype(BF16)
    r1 = a - a1.astype(F32)
    a2 = r1.astype(BF16)
    a3 = (r1 - a2.astype(F32)).astype(BF16)
    return a1, a2, a3


def _dot_f32(a, b):
    a1, a2, a3 = _split3(a)
    b1, b2, b3 = _split3(b)
    lo = _dot(a1, b3) + _dot(a2, b2) + _dot(a3, b1)
    mid = _dot(a1, b2) + _dot(a2, b1)
    return lo + mid + _dot(a1, b1)


def _dot_f32x3(a, b):
    a1, a2 = _split2(a)
    b1, b2 = _split2(b)
    return (_dot(a1, b2) + _dot(a2, b1)) + _dot(a1, b1)


def _silu(x):
    return x * jax.nn.sigmoid(x)


def _layer_norm(r, gain, bias):
    mu = jnp.mean(r, axis=-1, keepdims=True)
    d = r - mu
    var = jnp.mean(d * d, axis=-1, keepdims=True)
    return d * lax.rsqrt(var + LN_EPS) * gain + bias


def _mod_kernel(c_ref, w_ref, b_ref, o_ref):
    o_ref[0] = _dot_f32(_silu(c_ref[...]), w_ref[0]) + b_ref[0]


def _modulation(c, mod_w, mod_b):
    depth, d, nd = mod_w.shape
    bsz = c.shape[0]
    return pl.pallas_call(
        _mod_kernel,
        grid=(depth, nd // d),
        in_specs=[
            pl.BlockSpec((bsz, d), lambda l, n: (0, 0)),
            pl.BlockSpec((1, d, d), lambda l, n: (l, 0, n)),
            pl.BlockSpec((1, 1, d), lambda l, n: (l, 0, n)),
        ],
        out_specs=pl.BlockSpec((1, bsz, d), lambda l, n: (l, 0, n)),
        out_shape=jax.ShapeDtypeStruct((depth, bsz, nd), F32),
        compiler_params=_cparams("parallel", "parallel"),
        name="modulation",
    )(c, mod_w, mod_b.reshape(depth, 1, nd))


def _gate_fold_kernel(wl_ref, wu_ref, o_ref):
    o_ref[...] = _dot_f32(wl_ref[...], wu_ref[...])


def _gate_fold(w_low, w_up):
    d = w_low.shape[0]
    pad = 128 - GLA_GATE_RANK
    wl = jnp.pad(w_low, ((0, 0), (0, pad)))
    wu = jnp.pad(w_up, ((0, pad), (0, 0)))
    return pl.pallas_call(
        _gate_fold_kernel,
        out_shape=jax.ShapeDtypeStruct((d, GLA_DK), F32),
        name="gate_fold",
    )(wl, wu)


def _s5_disc_kernel(lre_ref, lim_ref, ldt_ref, bre_ref, bim_ref, lam_ref, bbre_ref, bbim_ref):
    lre = lre_ref[...]
    lim = lim_ref[...]
    dt = jnp.exp(ldt_ref[...])
    mag = jnp.exp(lre * dt)
    ang = lim * dt
    are = mag * jnp.cos(ang)
    aim = mag * jnp.sin(ang)
    lam_ref[0] = are
    lam_ref[1] = aim
    den = lre * lre + lim * lim
    nre = are - 1.0
    cre = (nre * lre + aim * lim) / den
    cim = (aim * lre - nre * lim) / den
    for h in range(S5_GROUP_CH):
        bre = bre_ref[h]
        bim = bim_ref[h]
        bbre_ref[h] = cre * bre - cim * bim
        bbim_ref[h] = cre * bim + cim * bre


def _s5_discretise(lam_re, lam_im, log_dt, b_re, b_im):
    g, p = lam_re.shape
    hch = b_re.shape[-1]
    bre_t = jnp.transpose(b_re, (2, 0, 1))
    bim_t = jnp.transpose(b_im, (2, 0, 1))
    lam, bbre, bbim = pl.pallas_call(
        _s5_disc_kernel,
        out_shape=(
            jax.ShapeDtypeStruct((2, g, p), F32),
            jax.ShapeDtypeStruct((hch, g, p), F32),
            jax.ShapeDtypeStruct((hch, g, p), F32),
        ),
        name="s5_discretise",
    )(lam_re, lam_im, log_dt.reshape(g, 1), bre_t, bim_t)
    return lam, bbre, bbim


def _s5_weights(lam_re, lam_im, log_dt, b_re, b_im, c_re, c_im):
    g, p, hch = S5_GROUPS, S5_STATE, S5_GROUP_CH
    lam, bbre, bbim = _s5_discretise(lam_re, lam_im, log_dt, b_re, b_im)
    lam_rows = lam.reshape(2, g * p)
    eye = jnp.eye(g, dtype=F32)

    def in_blocks(bb):
        dense = jnp.einsum('hgp,gk->ghkp', bb, eye).reshape(g * hch, g * p)
        blocks = [dense[128 * (n // 2):128 * (n // 2) + 128, 256 * n:256 * (n + 1)] for n in range(8)]
        return jnp.stack(blocks).astype(BF16)

    def out_blocks(cc, sign):
        dense = jnp.einsum('ghp,gk->gpkh', cc, eye).reshape(g * p, g * hch) * sign
        blocks = [dense[1024 * m:1024 * (m + 1), 256 * m:256 * (m + 1)] for m in range(2)]
        return jnp.stack(blocks).astype(BF16)

    return lam_rows, in_blocks(bbre), in_blocks(bbim), out_blocks(c_re, 1.0), out_blocks(c_im, -1.0)


IN_U = (0, 512)
IN_Q = (512, 768)
IN_K = (768, 1024)
IN_V = (1024, 1536)
IN_G = (1536, 2048)
IN_P = (2048, 2304)


def _inproj_kernel(x_ref, mod_ref, w_ref, u_ref, q_ref, k_ref, v_ref, g_ref, p_ref):
    sh = mod_ref[0, 0:1, :]
    sc = mod_ref[0, 1:2, :]
    h = (x_ref[0] * (1.0 + sc) + sh).astype(BF16)
    u_ref[...] = _dot(h, w_ref[:, IN_U[0]:IN_U[1]]).astype(BF16)
    q_ref[0] = _dot(h, w_ref[:, IN_Q[0]:IN_Q[1]]).astype(BF16)
    k_ref[0] = _dot(h, w_ref[:, IN_K[0]:IN_K[1]]).astype(BF16)
    v_ref[0] = _dot(h, w_ref[:, IN_V[0]:IN_V[1]]).astype(BF16)
    g_ref[0] = _dot(h, w_ref[:, IN_G[0]:IN_G[1]]).astype(BF16)
    p_ref[0] = _dot(h, w_ref[:, IN_P[0]:IN_P[1]])


def _in_projection(x, mod_l, w_cat, tm):
    bsz, seq, d = x.shape
    ncol = w_cat.shape[1]
    bspec = lambda w: pl.BlockSpec((1, tm, w), lambda b, t: (b, t, 0))
    return pl.pallas_call(
        _inproj_kernel,
        grid=(bsz, seq // tm),
        in_specs=[
            pl.BlockSpec((1, tm, d), lambda b, t: (b, t, 0)),
            pl.BlockSpec((1, N_MOD, d), lambda b, t: (b, 0, 0)),
            pl.BlockSpec((d, ncol), lambda b, t: (0, 0)),
        ],
        out_specs=[
            pl.BlockSpec((tm, S5_WIDTH), lambda b, t: (t, b)),
            bspec(GLA_DK), bspec(GLA_DK), bspec(GLA_WIDTH), bspec(GLA_WIDTH), bspec(GLA_DK),
        ],
        out_shape=[
            jax.ShapeDtypeStruct((seq, bsz * S5_WIDTH), BF16),
            jax.ShapeDtypeStruct((bsz, seq, GLA_DK), BF16),
            jax.ShapeDtypeStruct((bsz, seq, GLA_DK), BF16),
            jax.ShapeDtypeStruct((bsz, seq, GLA_WIDTH), BF16),
            jax.ShapeDtypeStruct((bsz, seq, GLA_WIDTH), BF16),
            jax.ShapeDtypeStruct((bsz, seq, GLA_DK), F32),
        ],
        compiler_params=_cparams("parallel", "parallel"),
        name="in_projection",
    )(x, mod_l, w_cat)


S5_SCAN_LANES = 256


def _s5_kernel(u_ref, lam_ref, wbre_ref, wbim_ref, wcre_ref, wcim_ref, d_ref, wglu_ref, bglu_ref,
               o_ref, hre, him, st_re, st_im, *, tt, nb):
    @pl.when(pl.program_id(0) == 0)
    def _():
        st_re[...] = jnp.zeros_like(st_re)
        st_im[...] = jnp.zeros_like(st_im)

    rows = tt * nb
    ub = u_ref[...].reshape(rows, S5_WIDTH)
    for n in range(8):
        lhs = ub[:, 128 * (n // 2):128 * (n // 2) + 128]
        hre[:, 256 * n:256 * (n + 1)] = _dot(lhs, wbre_ref[n])
        him[:, 256 * n:256 * (n + 1)] = _dot(lhs, wbim_ref[n])

    for c in range(S5_LANES // S5_SCAN_LANES):
        ls = slice(c * S5_SCAN_LANES, (c + 1) * S5_SCAN_LANES)
        lr = jnp.broadcast_to(lam_ref[0:1, ls], (nb, S5_SCAN_LANES))
        li = jnp.broadcast_to(lam_ref[1:2, ls], (nb, S5_SCAN_LANES))
        hr = st_re[:, ls]
        hi = st_im[:, ls]
        for t in range(tt):
            rs = slice(t * nb, (t + 1) * nb)
            nr = lr * hr - li * hi + hre[rs, ls]
            ni = lr * hi + li * hr + him[rs, ls]
            hre[rs, ls] = nr
            him[rs, ls] = ni
            hr, hi = nr, ni
        st_re[:, ls] = hr
        st_im[:, ls] = hi

    ys = []
    for m in range(2):
        ks = slice(1024 * m, 1024 * (m + 1))
        ys.append(_dot(hre[:, ks].astype(BF16), wcre_ref[m]) + _dot(him[:, ks].astype(BF16), wcim_ref[m]))
    y = jnp.concatenate(ys, axis=-1) + d_ref[...] * ub.astype(F32)
    y = 0.5 * y * (1.0 + jnp.tanh(math.sqrt(2.0 / math.pi) * (y + 0.044715 * (y * y * y))))
    z = y * jax.nn.sigmoid(_dot(y.astype(BF16), wglu_ref[...]) + bglu_ref[...])
    o_ref[...] = z.astype(BF16).reshape(tt, nb, S5_WIDTH)


def _s5_group(u_tm, lam_rows, wbre, wbim, wcre, wcim, d_skip, w_glu, b_glu, tt):
    seq, nb, width = u_tm.shape
    rows = tt * nb
    full = lambda a: pl.BlockSpec(a.shape, lambda t: (0,) * a.ndim)
    d2 = d_skip.reshape(1, width)
    b2 = b_glu.reshape(1, width)
    wg = w_glu.astype(BF16)
    return pl.pallas_call(
        functools.partial(_s5_kernel, tt=tt, nb=nb),
        grid=(seq // tt,),
        in_specs=[
            pl.BlockSpec((tt, nb, width), lambda t: (t, 0, 0)),
            full(lam_rows), full(wbre), full(wbim), full(wcre), full(wcim), full(d2), full(wg), full(b2),
        ],
        out_specs=pl.BlockSpec((tt, nb, width), lambda t: (t, 0, 0)),
        out_shape=jax.ShapeDtypeStruct((seq, nb, width), BF16),
        scratch_shapes=[
            pltpu.VMEM((rows, S5_LANES), F32),
            pltpu.VMEM((rows, S5_LANES), F32),
            pltpu.VMEM((nb, S5_LANES), F32),
            pltpu.VMEM((nb, S5_LANES), F32),
        ],
        compiler_params=_cparams("arbitrary"),
        name="s5_group",
    )(u_tm, lam_rows, wbre, wbim, wcre, wcim, d2, wg, b2)


def _gla_kernel(q_ref, k_ref, v_ref, g_ref, p_ref, ba_ref, gain_ref, o_ref, s_ref, oi_ref, upd_ref, *, nchunk):
    @pl.when(pl.program_id(1) == 0)
    def _():
        s_ref[...] = jnp.zeros_like(s_ref)

    c = GLA_CHUNK
    nh = GLA_HEADS
    srow = lax.broadcasted_iota(jnp.int32, (nh * c, c), 0)
    scol = lax.broadcasted_iota(jnp.int32, (nh * c, c), 1)
    causal = (srow & (c - 1)) >= scol
    lane = lax.broadcasted_iota(jnp.int32, (1, GLA_DK), 1)
    head_lanes = [(lane >= GLA_HEAD_K * h) & (lane < GLA_HEAD_K * (h + 1)) for h in range(nh)]
    vrow = lax.broadcasted_iota(jnp.int32, (2 * GLA_HEAD_V, 2 * GLA_HEAD_K), 0)
    kcol = lax.broadcasted_iota(jnp.int32, (2 * GLA_HEAD_V, 2 * GLA_HEAD_K), 1)
    same_head = (vrow < GLA_HEAD_V) == (kcol < GLA_HEAD_K)
    pair_k = [slice(2 * GLA_HEAD_K * pr, 2 * GLA_HEAD_K * (pr + 1)) for pr in range(nh // 2)]
    pair_v = [slice(2 * GLA_HEAD_V * pr, 2 * GLA_HEAD_V * (pr + 1)) for pr in range(nh // 2)]
    scale = GLA_HEAD_K ** -0.5
    mid = c // 2 - 1

    t = nchunk * c
    trow = lax.broadcasted_iota(jnp.int32, (t, t), 0)
    tcol = lax.broadcasted_iota(jnp.int32, (t, t), 1)
    chunk_tri = jnp.where((tcol <= trow) & (tcol >= (trow & -c)), 1.0, 0.0).astype(BF16)
    pre = p_ref[0] + ba_ref[...]
    log_alpha = (jnp.minimum(pre, 0.0) - jnp.log1p(jnp.exp(-jnp.abs(pre)))) * (1.0 / GLA_GATE_TAU)
    a1, a2, a3 = _split3(log_alpha)
    b = (_dot(chunk_tri, a3) + _dot(chunk_tri, a2) + _dot(chunk_tri, a1)).reshape(nchunk, c, GLA_DK)
    b_mid = b[:, mid:mid + 1, :]
    b_last = b[:, c - 1:c, :]
    q = (q_ref[0].astype(F32) * scale).reshape(nchunk, c, GLA_DK)
    k = k_ref[0].astype(F32).reshape(nchunk, c, GLA_DK)
    q_in = (q * jnp.exp(b)).astype(BF16)
    q_e = q * jnp.exp(b - b_mid)
    k_e = (k * jnp.exp(b_mid - b)).astype(BF16)
    k_d = (k * jnp.exp(b_last - b)).astype(BF16)
    decay = jnp.exp(b_last)
    for ci in range(nchunk):
        sl = slice(ci * c, (ci + 1) * c)
        q_stack = jnp.concatenate([jnp.where(head_lanes[h], q_e[ci], 0.0) for h in range(nh)], axis=0).astype(BF16)
        scores = lax.dot_general(q_stack, k_e[ci], (((1,), (1,)), ((), ())),
                                 preferred_element_type=F32)
        scores = jnp.where(causal, scores, 0.0).astype(BF16)
        v_all = v_ref[0, sl, :]
        pv = _dot(scores, v_all)
        for h in range(nh):
            os_ = slice(GLA_HEAD_V * h, GLA_HEAD_V * (h + 1))
            oi_ref[sl, os_] = pv[c * h:c * (h + 1), os_]
        for pr in range(nh // 2):
            upd = lax.dot_general(v_all[:, pair_v[pr]], k_d[ci][:, pair_k[pr]], (((0,), (0,)), ((), ())),
                                  preferred_element_type=F32)
            upd_ref[ci, pr] = jnp.where(same_head, upd, 0.0)

    for pr in range(nh // 2):
        s_t = s_ref[pr]
        for ci in range(nchunk):
            sl = slice(ci * c, (ci + 1) * c)
            oi_ref[sl, pair_v[pr]] += lax.dot_general(q_in[ci][:, pair_k[pr]], s_t.astype(BF16),
                                                      (((1,), (1,)), ((), ())), preferred_element_type=F32)
            s_t = decay[ci][:, pair_k[pr]] * s_t + upd_ref[ci, pr]
        s_ref[pr] = s_t

    for h in range(nh):
        os_ = slice(GLA_HEAD_V * h, GLA_HEAD_V * (h + 1))
        o_h = oi_ref[:, os_]
        ms = jnp.mean(o_h * o_h, axis=-1, keepdims=True)
        o_n = o_h * lax.rsqrt(ms + RMS_EPS) * gain_ref[:, os_]
        o_ref[0, :, os_] = (o_n * _silu(g_ref[0, :, os_].astype(F32))).astype(BF16)


def _gla_group(q, k, v, g_out, pre, b_alpha, head_gain, tg):
    bsz, seq, _ = q.shape
    blk = lambda w: pl.BlockSpec((1, tg, w), lambda b, t: (b, t, 0))
    ba = b_alpha.reshape(1, GLA_DK)
    gain = head_gain.reshape(1, GLA_WIDTH)
    return pl.pallas_call(
        functools.partial(_gla_kernel, nchunk=tg // GLA_CHUNK),
        grid=(bsz, seq // tg),
        in_specs=[
            blk(GLA_DK), blk(GLA_DK), blk(GLA_WIDTH), blk(GLA_WIDTH), blk(GLA_DK),
            pl.BlockSpec((1, GLA_DK), lambda b, t: (0, 0)),
            pl.BlockSpec((1, GLA_WIDTH), lambda b, t: (0, 0)),
        ],
        out_specs=blk(GLA_WIDTH),
        out_shape=jax.ShapeDtypeStruct((bsz, seq, GLA_WIDTH), BF16),
        scratch_shapes=[
            pltpu.VMEM((GLA_HEADS // 2, 2 * GLA_HEAD_V, 2 * GLA_HEAD_K), F32),
            pltpu.VMEM((tg, GLA_WIDTH), F32),
            pltpu.VMEM((tg // GLA_CHUNK, GLA_HEADS // 2, 2 * GLA_HEAD_V, 2 * GLA_HEAD_K), F32),
        ],
        compiler_params=_cparams("parallel", "arbitrary"),
        name="gla_group",
    )(q, k, v, g_out, pre, ba, gain)


def _mix_out(ys_ref, yg_ref, x_ref, mod_ref, w_ref, lg_ref, lb_ref, alpha):
    y = _dot(ys_ref[...], w_ref[0:S5_WIDTH, :]) + _dot(yg_ref[0], w_ref[S5_WIDTH:, :])
    gt = mod_ref[0, 2:3, :]
    x1 = _layer_norm(alpha * x_ref[0] + (1.0 + gt) * y, lg_ref[...], lb_ref[...])
    h2 = x1 * (1.0 + mod_ref[0, 4:5, :]) + mod_ref[0, 3:4, :]
    return x1, h2


RT_E1, RT_E2, RT_R1, RT_R2, RT_W1, RT_W2 = range(6)


def _top2_route(logits, count):
    tm = logits.shape[0]
    lane = lax.broadcasted_iota(jnp.int32, logits.shape, 1).astype(F32)
    l0 = jnp.where(lane < N_EXPERTS, logits, NEG_BIG)
    m1 = jnp.max(l0, axis=-1, keepdims=True)
    i1 = jnp.min(jnp.where(l0 == m1, lane, float(ROUTER_LANES)), axis=-1, keepdims=True)
    sel1 = lane == i1
    l1 = jnp.where(sel1, NEG_BIG, l0)
    m2 = jnp.max(l1, axis=-1, keepdims=True)
    i2 = jnp.min(jnp.where(l1 == m2, lane, float(ROUTER_LANES)), axis=-1, keepdims=True)
    sel2 = lane == i2
    e2 = jnp.exp(m2 - m1)
    w1 = 1.0 / (1.0 + e2)
    w2 = e2 / (1.0 + e2)
    chosen = jnp.where(sel1, 1.0, 0.0) + jnp.where(sel2, 1.0, 0.0)
    row = lax.broadcasted_iota(jnp.int32, (tm, tm), 0)
    col = lax.broadcasted_iota(jnp.int32, (tm, tm), 1)
    earlier = jnp.where(row > col, 1.0, 0.0).astype(BF16)
    before = _dot(earlier, chosen.astype(BF16)) + count
    r1 = jnp.sum(jnp.where(sel1, before, 0.0), axis=-1, keepdims=True)
    r2 = jnp.sum(jnp.where(sel2, before, 0.0), axis=-1, keepdims=True)
    rec = jnp.zeros_like(logits)
    for ln, val in ((RT_E1, i1), (RT_E2, i2), (RT_R1, r1), (RT_R2, r2), (RT_W1, w1), (RT_W2, w2)):
        rec = jnp.where(lane == float(ln), val, rec)
    return rec, count + jnp.sum(chosen, axis=0, keepdims=True)


def _outproj_router_kernel(ys_ref, yg_ref, x_ref, mod_ref, w_ref, lg_ref, lb_ref, wr_ref, br_ref,
                           x1_ref, route_ref, count_ref, cnt, *, alpha):
    @pl.when((pl.program_id(0) == 0) & (pl.program_id(1) == 0))
    def _():
        cnt[...] = jnp.zeros_like(cnt)

    x1, h2 = _mix_out(ys_ref, yg_ref, x_ref, mod_ref, w_ref, lg_ref, lb_ref, alpha)
    x1_ref[0] = x1
    logits = _dot_f32x3(h2, wr_ref[...]) + br_ref[...]
    rec, new_count = _top2_route(logits, cnt[...])
    route_ref[0] = rec
    cnt[...] = new_count
    count_ref[...] = jnp.broadcast_to(new_count, count_ref.shape)


def _out_projection_router(y_s5_tm, y_gla, x, mod_l, w_out, ln_g, ln_b, w_router, b_router, alpha, tm):
    bsz, seq, d = x.shape
    row = lambda a: a.reshape(1, -1)
    const = lambda shape: pl.BlockSpec(shape, lambda b, t: (0,) * len(shape))
    pad = ROUTER_LANES - N_EXPERTS
    return pl.pallas_call(
        functools.partial(_outproj_router_kernel, alpha=alpha),
        grid=(bsz, seq // tm),
        in_specs=[
            pl.BlockSpec((tm, S5_WIDTH), lambda b, t: (t, b)),
            pl.BlockSpec((1, tm, GLA_WIDTH), lambda b, t: (b, t, 0)),
            pl.BlockSpec((1, tm, d), lambda b, t: (b, t, 0)),
            pl.BlockSpec((1, N_MOD, d), lambda b, t: (b, 0, 0)),
            const((d, d)), const((1, d)), const((1, d)),
            const((d, ROUTER_LANES)), const((1, ROUTER_LANES)),
        ],
        out_specs=[
            pl.BlockSpec((1, tm, d), lambda b, t: (b, t, 0)),
            pl.BlockSpec((1, tm, ROUTER_LANES), lambda b, t: (b, t, 0)),
            const((8, ROUTER_LANES)),
        ],
        out_shape=[
            jax.ShapeDtypeStruct((bsz, seq, d), F32),
            jax.ShapeDtypeStruct((bsz, seq, ROUTER_LANES), F32),
            jax.ShapeDtypeStruct((8, ROUTER_LANES), F32),
        ],
        scratch_shapes=[pltpu.VMEM((1, ROUTER_LANES), F32)],
        compiler_params=_cparams("arbitrary", "arbitrary"),
        name="out_projection",
    )(y_s5_tm, y_gla, x, mod_l, w_out.astype(BF16), row(ln_g), row(ln_b),
      jnp.pad(w_router, ((0, 0), (0, pad))), jnp.pad(row(b_router), ((0, 0), (0, pad))))


def _mix_ffn_kernel(ys_ref, yg_ref, x_ref, mod_ref, wo_ref, lg1_ref, lb1_ref, wg_ref, wu_ref, wd_ref,
                    lg2_ref, lb2_ref, o_ref, acc_ref, x1_ref, hb_ref, *, alpha):
    j = pl.program_id(1)

    @pl.when(j == 0)
    def _():
        x1, h2 = _mix_out(ys_ref, yg_ref, x_ref, mod_ref, wo_ref, lg1_ref, lb1_ref, alpha)
        x1_ref[...] = x1
        hb_ref[...] = h2.astype(BF16)
        acc_ref[...] = jnp.zeros_like(acc_ref)

    h = hb_ref[...]
    a = _dot(h, wg_ref[...])
    mid = (_silu(a) * _dot(h, wu_ref[...])).astype(BF16)
    acc_ref[...] += _dot(mid, wd_ref[...])

    @pl.when(j == pl.num_programs(1) - 1)
    def _():
        gt = mod_ref[0, 5:6, :]
        o_ref[0] = _layer_norm(alpha * x1_ref[...] + (1.0 + gt) * acc_ref[...], lg2_ref[...], lb2_ref[...])


def _mix_dense_ffn(y_s5_tm, y_gla, x, mod_l, w_out, ln1_g, ln1_b, w_gate, w_up, w_down, ln2_g, ln2_b,
                   alpha, tm, tf):
    bsz, seq, d = x.shape
    dff = w_gate.shape[1]
    per_b = seq // tm
    row = lambda a: a.reshape(1, d)
    const = lambda shape: pl.BlockSpec(shape, lambda i, j: (0,) * len(shape))
    tok = lambda w: pl.BlockSpec((1, tm, w), lambda i, j: (i // per_b, i % per_b, 0))
    return pl.pallas_call(
        functools.partial(_mix_ffn_kernel, alpha=alpha),
        grid=(bsz * per_b, dff // tf),
        in_specs=[
            pl.BlockSpec((tm, S5_WIDTH), lambda i, j: (i % per_b, i // per_b)),
            tok(GLA_WIDTH),
            tok(d),
            pl.BlockSpec((1, N_MOD, d), lambda i, j: (i // per_b, 0, 0)),
            const((d, d)), const((1, d)), const((1, d)),
            pl.BlockSpec((d, tf), lambda i, j: (0, j)),
            pl.BlockSpec((d, tf), lambda i, j: (0, j)),
            pl.BlockSpec((tf, d), lambda i, j: (j, 0)),
            const((1, d)), const((1, d)),
        ],
        out_specs=tok(d),
        out_shape=jax.ShapeDtypeStruct((bsz, seq, d), F32),
        scratch_shapes=[pltpu.VMEM((tm, d), F32), pltpu.VMEM((tm, d), F32), pltpu.VMEM((tm, d), BF16)],
        compiler_params=_cparams("parallel", "arbitrary"),
        name="mix_dense_ffn",
    )(y_s5_tm, y_gla, x, mod_l, w_out.astype(BF16), row(ln1_g), row(ln1_b),
      w_gate.astype(BF16), w_up.astype(BF16), w_down.astype(BF16), row(ln2_g), row(ln2_b))


def _route_tables(route, count, rb, nblk):
    e = route[:, RT_E1:RT_E2 + 1].astype(jnp.int32)
    rank = route[:, RT_R1:RT_R2 + 1].astype(jnp.int32)
    cnt = count[0, :N_EXPERTS].astype(jnp.int32)
    padded = ((cnt + rb - 1) // rb) * rb
    ends = jnp.cumsum(padded)
    starts = ends - padded
    pos = jnp.sum(jnp.where(e[..., None] == jnp.arange(N_EXPERTS), starts, 0), axis=-1) + rank
    n_valid = ends[-1] // rb
    blk = jnp.minimum(jnp.arange(nblk, dtype=jnp.int32), n_valid - 1)
    blk_expert = jnp.sum((blk[:, None] * rb >= ends[None, :]).astype(jnp.int32), axis=-1)
    blk_expert = jnp.minimum(blk_expert, N_EXPERTS - 1)
    n_rows = jnp.full((1,), nblk * rb, jnp.int32)
    fill = jnp.stack([jnp.concatenate([starts + cnt, ends[-1:]]), jnp.concatenate([ends, n_rows])], axis=1)
    return (pos.astype(jnp.int32), blk_expert.astype(jnp.int32), n_valid.astype(jnp.int32).reshape(1),
            fill.reshape(-1).astype(jnp.int32))


SUBLANES = 8


def _staged_row(ref, tile, sub):
    return ref.at[tile, pl.ds(sub, 1), :]


def _dispatch_kernel(fill_ref, pos_ref, x_ref, mod_ref, xs_hbm, hs_ref, sem, *, tb):
    i = pl.program_id(0)
    h = x_ref[...] * (1.0 + mod_ref[0, 4:5, :]) + mod_ref[0, 3:4, :]
    hs_ref[...] = h.reshape(tb // SUBLANES, SUBLANES, h.shape[-1])

    def issue(r8, carry):
        for sub in range(SUBLANES):
            for kk in range(2):
                dst_row = pos_ref[0, 0, 2 * SUBLANES * r8 + 2 * sub + kk]
                pltpu.make_async_copy(_staged_row(hs_ref, r8, sub), xs_hbm.at[pl.ds(dst_row, 1), :],
                                      sem.at[0]).start()
        return carry

    lax.fori_loop(0, tb // SUBLANES, issue, 0)

    @pl.when(i == 0)
    def _():
        for e in range(N_EXPERTS + 1):
            lo = fill_ref[2 * e]
            hi = fill_ref[2 * e + 1]

            def fill(p, carry):
                pltpu.make_async_copy(_staged_row(hs_ref, 0, 0), xs_hbm.at[pl.ds(p, 1), :], sem.at[1]).start()
                return carry

            def drain(p, carry):
                pltpu.make_async_copy(_staged_row(hs_ref, 0, 0), xs_hbm.at[pl.ds(p, 1), :], sem.at[1]).wait()
                return carry

            lax.fori_loop(lo, hi, fill, 0)
            lax.fori_loop(lo, hi, drain, 0)

    for _ in range(2):
        pltpu.make_async_copy(x_ref, xs_hbm.at[pl.ds(0, tb), :], sem.at[0]).wait()


def _dispatch(x1, mod_l, pos, fill, n_rows, tb, seq):
    n, d = x1.shape
    per_b = seq // tb
    grid_spec = pltpu.PrefetchScalarGridSpec(
        num_scalar_prefetch=1,
        grid=(n // tb,),
        in_specs=[
            pl.BlockSpec((1, 1, 2 * tb), lambda i, f: (i, 0, 0), memory_space=pltpu.SMEM),
            pl.BlockSpec((tb, d), lambda i, f: (i, 0)),
            pl.BlockSpec((1, N_MOD, d), lambda i, f: (i // per_b, 0, 0)),
        ],
        out_specs=pl.BlockSpec(memory_space=pl.ANY),
        scratch_shapes=[pltpu.VMEM((tb // SUBLANES, SUBLANES, d), F32), pltpu.SemaphoreType.DMA((2,))],
    )
    return pl.pallas_call(
        functools.partial(_dispatch_kernel, tb=tb),
        grid_spec=grid_spec,
        out_shape=jax.ShapeDtypeStruct((n_rows, d), F32),
        compiler_params=_cparams("arbitrary"),
        name="moe_dispatch",
    )(fill, pos.reshape(n // tb, 1, 2 * tb), x1, mod_l)


def _expert_kernel(be_ref, nv_ref, xs_ref, wg_ref, wu_ref, wd_ref, y_ref, acc_ref, xb_ref):
    del be_ref
    i = pl.program_id(0)
    j = pl.program_id(1)

    @pl.when(i < nv_ref[0])
    def _():
        @pl.when(j == 0)
        def _():
            acc_ref[...] = jnp.zeros_like(acc_ref)
            xb_ref[...] = xs_ref[...].astype(BF16)

        h = xb_ref[...]
        a = _dot(h, wg_ref[0])
        mid = (_silu(a) * _dot(h, wu_ref[0])).astype(BF16)
        acc_ref[...] += _dot(mid, wd_ref[0])

        @pl.when(j == pl.num_programs(1) - 1)
        def _():
            y_ref[...] = acc_ref[...]

    @pl.when((i >= nv_ref[0]) & (j == 0))
    def _():
        y_ref[...] = jnp.zeros_like(y_ref)


def _experts(xs, blk_expert, n_valid, w_gate, w_up, w_down, rb, tf):
    n_rows, d = xs.shape
    dff = w_gate.shape[2]
    nj = dff // tf
    row_blk = lambda i, j, be, nv: (jnp.minimum(i, nv[0] - 1), 0)
    jj = lambda i, j, nv: jnp.where(i < nv[0], j, nj - 1)
    grid_spec = pltpu.PrefetchScalarGridSpec(
        num_scalar_prefetch=2,
        grid=(n_rows // rb, nj),
        in_specs=[
            pl.BlockSpec((rb, d), row_blk),
            pl.BlockSpec((1, d, tf), lambda i, j, be, nv: (be[i], 0, jj(i, j, nv))),
            pl.BlockSpec((1, d, tf), lambda i, j, be, nv: (be[i], 0, jj(i, j, nv))),
            pl.BlockSpec((1, tf, d), lambda i, j, be, nv: (be[i], jj(i, j, nv), 0)),
        ],
        out_specs=pl.BlockSpec((rb, d), lambda i, j, be, nv: (i, 0)),
        scratch_shapes=[pltpu.VMEM((rb, d), F32), pltpu.VMEM((rb, d), BF16)],
    )
    return pl.pallas_call(
        _expert_kernel,
        grid_spec=grid_spec,
        out_shape=jax.ShapeDtypeStruct((n_rows, d), F32),
        compiler_params=_cparams("arbitrary", "arbitrary"),
        name="moe_experts",
    )(blk_expert, n_valid, xs, w_gate.astype(BF16), w_up.astype(BF16), w_down.astype(BF16))


def _combine_kernel(pos_ref, posn_ref, route_ref, x_ref, mod_ref, lg_ref, lb_ref, y_hbm, o_ref, buf, sem,
                    *, tb, alpha):
    i = pl.program_id(0)
    slot = i % 2
    d = o_ref.shape[-1]
    ntile = tb // SUBLANES

    def issue_tile(p_ref, s, r8):
        for sub in range(SUBLANES):
            for kk in range(2):
                src_row = p_ref[0, 0, 2 * SUBLANES * r8 + 2 * sub + kk]
                pltpu.make_async_copy(y_hbm.at[pl.ds(src_row, 1), :], _staged_row(buf.at[s, kk], r8, sub),
                                      sem.at[s]).start()

    def wait_slot(s):
        for _ in range(2):
            pltpu.make_async_copy(y_hbm.at[pl.ds(0, tb), :], o_ref, sem.at[s]).wait()

    def issue_all(p_ref, s):
        def body(r8, carry):
            issue_tile(p_ref, s, r8)
            return carry
        lax.fori_loop(0, ntile, body, 0)

    @pl.when(i == 0)
    def _():
        issue_all(pos_ref, 0)

    issue_all(posn_ref, 1 - slot)
    wait_slot(slot)
    rec = route_ref[...]
    y0 = buf[slot, 0].reshape(tb, d)
    y1 = buf[slot, 1].reshape(tb, d)
    f = rec[:, RT_W1:RT_W1 + 1] * y0 + rec[:, RT_W2:RT_W2 + 1] * y1
    gt = mod_ref[0, 5:6, :]
    o_ref[...] = _layer_norm(alpha * x_ref[...] + (1.0 + gt) * f, lg_ref[...], lb_ref[...])

    @pl.when(i == pl.num_programs(0) - 1)
    def _():
        wait_slot(1 - slot)


def _combine(ys, pos, route, x1, mod_l, ln_g, ln_b, alpha, tb, seq):
    n, d = x1.shape
    nb = n // tb
    per_b = seq // tb
    pos3 = pos.reshape(nb, 1, 2 * tb)
    return pl.pallas_call(
        functools.partial(_combine_kernel, tb=tb, alpha=alpha),
        grid=(nb,),
        in_specs=[
            pl.BlockSpec((1, 1, 2 * tb), lambda i: (i, 0, 0), memory_space=pltpu.SMEM),
            pl.BlockSpec((1, 1, 2 * tb), lambda i: (jnp.minimum(i + 1, nb - 1), 0, 0), memory_space=pltpu.SMEM),
            pl.BlockSpec((tb, ROUTER_LANES), lambda i: (i, 0)),
            pl.BlockSpec((tb, d), lambda i: (i, 0)),
            pl.BlockSpec((1, N_MOD, d), lambda i: (i // per_b, 0, 0)),
            pl.BlockSpec((1, d), lambda i: (0, 0)),
            pl.BlockSpec((1, d), lambda i: (0, 0)),
            pl.BlockSpec(memory_space=pl.ANY),
        ],
        out_specs=pl.BlockSpec((tb, d), lambda i: (i, 0)),
        out_shape=jax.ShapeDtypeStruct((n, d), F32),
        scratch_shapes=[pltpu.VMEM((2, 2, tb // SUBLANES, SUBLANES, d), F32), pltpu.SemaphoreType.DMA((2,))],
        compiler_params=_cparams("arbitrary"),
        name="moe_combine",
    )(pos3, pos3, route, x1, mod_l, ln_g.reshape(1, d), ln_b.reshape(1, d), ys)


def _moe_ffn(x1, route, count, mod_l, w_gate, w_up, w_down, ln_g, ln_b, alpha, rb, tf, tb):
    bsz, seq, d = x1.shape
    n = bsz * seq
    n_rows = 2 * n + N_EXPERTS * rb
    route2 = route.reshape(n, ROUTER_LANES)
    pos, blk_expert, n_valid, fill = _route_tables(route2, count, rb, n_rows // rb)
    xs = _dispatch(x1.reshape(n, d), mod_l, pos, fill, n_rows, rb, seq)
    ys = _experts(xs, blk_expert, n_valid, w_gate, w_up, w_down, rb, tf)
    out = _combine(ys, pos, route2, x1.reshape(n, d), mod_l, ln_g, ln_b, alpha, tb, seq)
    return out.reshape(bsz, seq, d)


def _tile(n, want):
    t = min(n, want)
    assert n % t == 0, (n, want)
    return t


def kernel(x, c, mod_w, mod_b, w_in, w_out, s5_lam_re, s5_lam_im, s5_log_dt, s5_b_re, s5_b_im, s5_c_re, s5_c_im, s5_d, s5_w_glu, s5_b_glu, gla_w_alpha_up, gla_b_alpha, gla_head_gain, ln_mix_g, ln_mix_b, ffn_w_gate, ffn_w_up, ffn_w_down, moe_w_router, moe_b_router, moe_w_gate, moe_w_up, moe_w_down, ln_ffn_g, ln_ffn_b):
    bsz, seq, d = x.shape
    depth = mod_w.shape[0]
    alpha = (2.0 * depth) ** 0.25
    tm = _tile(seq, 512)
    tt = _tile(seq, 16)
    tg = _tile(seq, 512)
    tm_ffn = _tile(seq, 1024)
    tf = _tile(D_FF, 512)
    tf_moe = _tile(D_FF, 896)
    tb = _tile(seq, 256)

    mod = _modulation(c, mod_w, mod_b).reshape(depth, bsz, N_MOD, d)
    for layer in range(depth):
        mod_l = mod[layer]
        w_gate_fold = _gate_fold(w_in[layer][:, IN_P[0]:], gla_w_alpha_up[layer])
        w_cat = jnp.concatenate([w_in[layer][:, :IN_P[0]], w_gate_fold], axis=1).astype(BF16)
        u_tm, q, k, v, g_out, pre = _in_projection(x, mod_l, w_cat, tm)

        s5w = _s5_weights(s5_lam_re[layer], s5_lam_im[layer], s5_log_dt[layer], s5_b_re[layer], s5_b_im[layer],
                          s5_c_re[layer], s5_c_im[layer])
        y_s5 = _s5_group(u_tm.reshape(seq, bsz, S5_WIDTH), *s5w, s5_d[layer], s5_w_glu[layer], s5_b_glu[layer], tt)
        y_s5 = y_s5.reshape(seq, bsz * S5_WIDTH)
        y_gla = _gla_group(q, k, v, g_out, pre, gla_b_alpha[layer], gla_head_gain[layer], tg)

        i = layer // 2
        if layer % 2 == 0:
            x = _mix_dense_ffn(y_s5, y_gla, x, mod_l, w_out[layer], ln_mix_g[layer], ln_mix_b[layer],
                               ffn_w_gate[i], ffn_w_up[i], ffn_w_down[i], ln_ffn_g[layer], ln_ffn_b[layer],
                               alpha, tm_ffn, tf)
        else:
            x1, route, count = _out_projection_router(y_s5, y_gla, x, mod_l, w_out[layer], ln_mix_g[layer],
                                                      ln_mix_b[layer], moe_w_router[i], moe_b_router[i], alpha, tm)
            x = _moe_ffn(x1, route, count, mod_l, moe_w_gate[i], moe_w_up[i], moe_w_down[i],
                         ln_ffn_g[layer], ln_ffn_b[layer], alpha, tm_ffn, tf_moe, tb)
    return x
```

```python
import functools
import math

import jax
import jax.numpy as jnp
from jax import lax
from jax.experimental import pallas as pl
from jax.experimental.pallas import tpu as pltpu

F32 = jnp.float32
BF16 = jnp.bfloat16

D_MODEL = 1024
S5_WIDTH = 512
S5_GROUP_CH = 16
S5_GROUPS = 32
S5_STATE = 64
S5_LANES = S5_GROUPS * S5_STATE
GLA_WIDTH = 512
GLA_HEADS = 4
GLA_DK = 256
GLA_HEAD_K = 64
GLA_HEAD_V = 128
GLA_GATE_RANK = 16
GLA_GATE_TAU = 16.0
GLA_CHUNK = 64
D_FF = 3584
N_EXPERTS = 8
N_MOD = 6
LN_EPS = 1e-5
RMS_EPS = 1e-6
ROUTER_LANES = 128
NEG_BIG = -1e30

VMEM_LIMIT = 56 * 1024 * 1024


def _cparams(*sem):
    return pltpu.CompilerParams(dimension_semantics=sem, vmem_limit_bytes=VMEM_LIMIT)


def _dot(a, b):
    return jnp.dot(a, b, preferred_element_type=F32)


def _split2(a):
    a1 = a.astype(BF16)
    a2 = (a - a1.astype(F32)).astype(BF16)
    return a1, a2


def _split3(a):
    a1 = a.astype(BF16)
    r1 = a - a1.astype(F32)
    a2 = r1.astype(BF16)
    a3 = (r1 - a2.astype(F32)).astype(BF16)
    return a1, a2, a3


def _dot_f32(a, b):
    a1, a2, a3 = _split3(a)
    b1, b2, b3 = _split3(b)
    lo = _dot(a1, b3) + _dot(a2, b2) + _dot(a3, b1)
    mid = _dot(a1, b2) + _dot(a2, b1)
    return lo + mid + _dot(a1, b1)


def _dot_f32x3(a, b):
    a1, a2 = _split2(a)
    b1, b2 = _split2(b)
    return (_dot(a1, b2) + _dot(a2, b1)) + _dot(a1, b1)


def _silu(x):
    return x * jax.nn.sigmoid(x)


def _layer_norm(r, gain, bias):
    mu = jnp.mean(r, axis=-1, keepdims=True)
    d = r - mu
    var = jnp.mean(d * d, axis=-1, keepdims=True)
    return d * lax.rsqrt(var + LN_EPS) * gain + bias


def _mod_kernel(c_ref, w_ref, b_ref, o_ref):
    o_ref[0] = _dot_f32(_silu(c_ref[...]), w_ref[0]) + b_ref[0]


def _modulation(c, mod_w, mod_b):
    depth, d, nd = mod_w.shape
    bsz = c.shape[0]
    return pl.pallas_call(
        _mod_kernel,
        grid=(depth, nd // d),
        in_specs=[
            pl.BlockSpec((bsz, d), lambda l, n: (0, 0)),
            pl.BlockSpec((1, d, d), lambda l, n: (l, 0, n)),
            pl.BlockSpec((1, 1, d), lambda l, n: (l, 0, n)),
        ],
        out_specs=pl.BlockSpec((1, bsz, d), lambda l, n: (l, 0, n)),
        out_shape=jax.ShapeDtypeStruct((depth, bsz, nd), F32),
        compiler_params=_cparams("parallel", "parallel"),
        name="modulation",
    )(c, mod_w, mod_b.reshape(depth, 1, nd))


def _gate_fold_kernel(wl_ref, wu_ref, o_ref):
    o_ref[...] = _dot_f32(wl_ref[...], wu_ref[...])


def _gate_fold(w_low, w_up):
    d = w_low.shape[0]
    pad = 128 - GLA_GATE_RANK
    wl = jnp.pad(w_low, ((0, 0), (0, pad)))
    wu = jnp.pad(w_up, ((0, pad), (0, 0)))
    return pl.pallas_call(
        _gate_fold_kernel,
        out_shape=jax.ShapeDtypeStruct((d, GLA_DK), F32),
        name="gate_fold",
    )(wl, wu)


def _s5_disc_kernel(lre_ref, lim_ref, ldt_ref, bre_ref, bim_ref, lam_ref, bbre_ref, bbim_ref):
    lre = lre_ref[...]
    lim = lim_ref[...]
    dt = jnp.exp(ldt_ref[...])
    mag = jnp.exp(lre * dt)
    ang = lim * dt
    are = mag * jnp.cos(ang)
    aim = mag * jnp.sin(ang)
    lam_ref[0] = are
    lam_ref[1] = aim
    den = lre * lre + lim * lim
    nre = are - 1.0
    cre = (nre * lre + aim * lim) / den
    cim = (aim * lre - nre * lim) / den
    for h in range(S5_GROUP_CH):
        bre = bre_ref[h]
        bim = bim_ref[h]
        bbre_ref[h] = cre * bre - cim * bim
        bbim_ref[h] = cre * bim + cim * bre


def _s5_discretise(lam_re, lam_im, log_dt, b_re, b_im):
    g, p = lam_re.shape
    hch = b_re.shape[-1]
    bre_t = jnp.transpose(b_re, (2, 0, 1))
    bim_t = jnp.transpose(b_im, (2, 0, 1))
    lam, bbre, bbim = pl.pallas_call(
        _s5_disc_kernel,
        out_shape=(
            jax.ShapeDtypeStruct((2, g, p), F32),
            jax.ShapeDtypeStruct((hch, g, p), F32),
            jax.ShapeDtypeStruct((hch, g, p), F32),
        ),
        name="s5_discretise",
    )(lam_re, lam_im, log_dt.reshape(g, 1), bre_t, bim_t)
    return lam, bbre, bbim


def _s5_weights(lam_re, lam_im, log_dt, b_re, b_im, c_re, c_im):
    g, p, hch = S5_GROUPS, S5_STATE, S5_GROUP_CH
    lam, bbre, bbim = _s5_discretise(lam_re, lam_im, log_dt, b_re, b_im)
    lam_rows = lam.reshape(2, g * p)
    eye = jnp.eye(g, dtype=F32)

    def in_blocks(bb):
        dense = jnp.einsum('hgp,gk->ghkp', bb, eye).reshape(g * hch, g * p)
        blocks = [dense[128 * (n // 2):128 * (n // 2) + 128, 256 * n:256 * (n + 1)] for n in range(8)]
        return jnp.stack(blocks).astype(BF16)

    def out_blocks(cc, sign):
        dense = jnp.einsum('ghp,gk->gpkh', cc, eye).reshape(g * p, g * hch) * sign
        blocks = [dense[1024 * m:1024 * (m + 1), 256 * m:256 * (m + 1)] for m in range(2)]
        return jnp.stack(blocks).astype(BF16)

    return lam_rows, in_blocks(bbre), in_blocks(bbim), out_blocks(c_re, 1.0), out_blocks(c_im, -1.0)


IN_U = (0, 512)
IN_Q = (512, 768)
IN_K = (768, 1024)
IN_V = (1024, 1536)
IN_G = (1536, 2048)
IN_P = (2048, 2304)


def _inproj_kernel(x_ref, mod_ref, w_ref, u_ref, q_ref, k_ref, v_ref, g_ref, p_ref):
    sh = mod_ref[0, 0:1, :]
    sc = mod_ref[0, 1:2, :]
    h = (x_ref[0] * (1.0 + sc) + sh).astype(BF16)
    u_ref[...] = _dot(h, w_ref[:, IN_U[0]:IN_U[1]]).astype(BF16)
    q_ref[0] = _dot(h, w_ref[:, IN_Q[0]:IN_Q[1]]).astype(BF16)
    k_ref[0] = _dot(h, w_ref[:, IN_K[0]:IN_K[1]]).astype(BF16)
    v_ref[0] = _dot(h, w_ref[:, IN_V[0]:IN_V[1]]).astype(BF16)
    g_ref[0] = _dot(h, w_ref[:, IN_G[0]:IN_G[1]]).astype(BF16)
    p_ref[0] = _dot(h, w_ref[:, IN_P[0]:IN_P[1]])


def _in_projection(x, mod_l, w_cat, tm):
    bsz, seq, d = x.shape
    ncol = w_cat.shape[1]
    bspec = lambda w: pl.BlockSpec((1, tm, w), lambda b, t: (b, t, 0))
    return pl.pallas_call(
        _inproj_kernel,
        grid=(bsz, seq // tm),
        in_specs=[
            pl.BlockSpec((1, tm, d), lambda b, t: (b, t, 0)),
            pl.BlockSpec((1, N_MOD, d), lambda b, t: (b, 0, 0)),
            pl.BlockSpec((d, ncol), lambda b, t: (0, 0)),
        ],
        out_specs=[
            pl.BlockSpec((tm, S5_WIDTH), lambda b, t: (t, b)),
            bspec(GLA_DK), bspec(GLA_DK), bspec(GLA_WIDTH), bspec(GLA_WIDTH), bspec(GLA_DK),
        ],
        out_shape=[
            jax.ShapeDtypeStruct((seq, bsz * S5_WIDTH), BF16),
            jax.ShapeDtypeStruct((bsz, seq, GLA_DK), BF16),
            jax.ShapeDtypeStruct((bsz, seq, GLA_DK), BF16),
            jax.ShapeDtypeStruct((bsz, seq, GLA_WIDTH), BF16),
            jax.ShapeDtypeStruct((bsz, seq, GLA_WIDTH), BF16),
            jax.ShapeDtypeStruct((bsz, seq, GLA_DK), F32),
        ],
        compiler_params=_cparams("parallel", "parallel"),
        name="in_projection",
    )(x, mod_l, w_cat)


S5_SCAN_LANES = 256


def _s5_kernel(u_ref, lam_ref, wbre_ref, wbim_ref, wcre_ref, wcim_ref, d_ref, wglu_ref, bglu_ref,
               o_ref, hre, him, st_re, st_im, *, tt, nb):
    @pl.when(pl.program_id(0) == 0)
    def _():
        st_re[...] = jnp.zeros_like(st_re)
        st_im[...] = jnp.zeros_like(st_im)

    rows = tt * nb
    ub = u_ref[...].reshape(rows, S5_WIDTH)
    for n in range(8):
        lhs = ub[:, 128 * (n // 2):128 * (n // 2) + 128]
        hre[:, 256 * n:256 * (n + 1)] = _dot(lhs, wbre_ref[n])
        him[:, 256 * n:256 * (n + 1)] = _dot(lhs, wbim_ref[n])

    for c in range(S5_LANES // S5_SCAN_LANES):
        ls = slice(c * S5_SCAN_LANES, (c + 1) * S5_SCAN_LANES)
        lr = jnp.broadcast_to(lam_ref[0:1, ls], (nb, S5_SCAN_LANES))
        li = jnp.broadcast_to(lam_ref[1:2, ls], (nb, S5_SCAN_LANES))
        hr = st_re[:, ls]
        hi = st_im[:, ls]
        for t in range(tt):
            rs = slice(t * nb, (t + 1) * nb)
            nr = lr * hr - li * hi + hre[rs, ls]
            ni = lr * hi + li * hr + him[rs, ls]
            hre[rs, ls] = nr
            him[rs, ls] = ni
            hr, hi = nr, ni
        st_re[:, ls] = hr
        st_im[:, ls] = hi

    ys = []
    for m in range(2):
        ks = slice(1024 * m, 1024 * (m + 1))
        ys.append(_dot(hre[:, ks].astype(BF16), wcre_ref[m]) + _dot(him[:, ks].astype(BF16), wcim_ref[m]))
    y = jnp.concatenate(ys, axis=-1) + d_ref[...] * ub.astype(F32)
    y = 0.5 * y * (1.0 + jnp.tanh(math.sqrt(2.0 / math.pi) * (y + 0.044715 * (y * y * y))))
    z = y * jax.nn.sigmoid(_dot(y.astype(BF16), wglu_ref[...]) + bglu_ref[...])
    o_ref[...] = z.astype(BF16).reshape(tt, nb, S5_WIDTH)


def _s5_group(u_tm, lam_rows, wbre, wbim, wcre, wcim, d_skip, w_glu, b_glu, tt):
    seq, nb, width = u_tm.shape
    rows = tt * nb
    full = lambda a: pl.BlockSpec(a.shape, lambda t: (0,) * a.ndim)
    d2 = d_skip.reshape(1, width)
    b2 = b_glu.reshape(1, width)
    wg = w_glu.astype(BF16)
    return pl.pallas_call(
        functools.partial(_s5_kernel, tt=tt, nb=nb),
        grid=(seq // tt,),
        in_specs=[
            pl.BlockSpec((tt, nb, width), lambda t: (t, 0, 0)),
            full(lam_rows), full(wbre), full(wbim), full(wcre), full(wcim), full(d2), full(wg), full(b2),
        ],
        out_specs=pl.BlockSpec((tt, nb, width), lambda t: (t, 0, 0)),
        out_shape=jax.ShapeDtypeStruct((seq, nb, width), BF16),
        scratch_shapes=[
            pltpu.VMEM((rows, S5_LANES), F32),
            pltpu.VMEM((rows, S5_LANES), F32),
            pltpu.VMEM((nb, S5_LANES), F32),
            pltpu.VMEM((nb, S5_LANES), F32),
        ],
        compiler_params=_cparams("arbitrary"),
        name="s5_group",
    )(u_tm, lam_rows, wbre, wbim, wcre, wcim, d2, wg, b2)


def _gla_kernel(q_ref, k_ref, v_ref, g_ref, p_ref, ba_ref, gain_ref, o_ref, s_ref, oi_ref, upd_ref, *, nchunk):
    @pl.when(pl.program_id(1) == 0)
    def _():
        s_ref[...] = jnp.zeros_like(s_ref)

    c = GLA_CHUNK
    nh = GLA_HEADS
    srow = lax.broadcasted_iota(jnp.int32, (nh * c, c), 0)
    scol = lax.broadcasted_iota(jnp.int32, (nh * c, c), 1)
    causal = (srow & (c - 1)) >= scol
    lane = lax.broadcasted_iota(jnp.int32, (1, GLA_DK), 1)
    head_lanes = [(lane >= GLA_HEAD_K * h) & (lane < GLA_HEAD_K * (h + 1)) for h in range(nh)]
    vrow = lax.broadcasted_iota(jnp.int32, (2 * GLA_HEAD_V, 2 * GLA_HEAD_K), 0)
    kcol = lax.broadcasted_iota(jnp.int32, (2 * GLA_HEAD_V, 2 * GLA_HEAD_K), 1)
    same_head = (vrow < GLA_HEAD_V) == (kcol < GLA_HEAD_K)
    pair_k = [slice(2 * GLA_HEAD_K * pr, 2 * GLA_HEAD_K * (pr + 1)) for pr in range(nh // 2)]
    pair_v = [slice(2 * GLA_HEAD_V * pr, 2 * GLA_HEAD_V * (pr + 1)) for pr in range(nh // 2)]
    scale = GLA_HEAD_K ** -0.5
    mid = c // 2 - 1

    t = nchunk * c
    trow = lax.broadcasted_iota(jnp.int32, (t, t), 0)
    tcol = lax.broadcasted_iota(jnp.int32, (t, t), 1)
    chunk_tri = jnp.where((tcol <= trow) & (tcol >= (trow & -c)), 1.0, 0.0).astype(BF16)
    pre = p_ref[0] + ba_ref[...]
    log_alpha = (jnp.minimum(pre, 0.0) - jnp.log1p(jnp.exp(-jnp.abs(pre)))) * (1.0 / GLA_GATE_TAU)
    a1, a2, a3 = _split3(log_alpha)
    b = (_dot(chunk_tri, a3) + _dot(chunk_tri, a2) + _dot(chunk_tri, a1)).reshape(nchunk, c, GLA_DK)
    b_mid = b[:, mid:mid + 1, :]
    b_last = b[:, c - 1:c, :]
    q = (q_ref[0].astype(F32) * scale).reshape(nchunk, c, GLA_DK)
    k = k_ref[0].astype(F32).reshape(nchunk, c, GLA_DK)
    q_in = (q * jnp.exp(b)).astype(BF16)
    q_e = q * jnp.exp(b - b_mid)
    k_e = (k * jnp.exp(b_mid - b)).astype(BF16)
    k_d = (k * jnp.exp(b_last - b)).astype(BF16)
    decay = jnp.exp(b_last)
    for ci in range(nchunk):
        sl = slice(ci * c, (ci + 1) * c)
        q_stack = jnp.concatenate([jnp.where(head_lanes[h], q_e[ci], 0.0) for h in range(nh)], axis=0).astype(BF16)
        scores = lax.dot_general(q_stack, k_e[ci], (((1,), (1,)), ((), ())),
                                 preferred_element_type=F32)
        scores = jnp.where(causal, scores, 0.0).astype(BF16)
        v_all = v_ref[0, sl, :]
        pv = _dot(scores, v_all)
        for h in range(nh):
            os_ = slice(GLA_HEAD_V * h, GLA_HEAD_V * (h + 1))
            oi_ref[sl, os_] = pv[c * h:c * (h + 1), os_]
        for pr in range(nh // 2):
            upd = lax.dot_general(v_all[:, pair_v[pr]], k_d[ci][:, pair_k[pr]], (((0,), (0,)), ((), ())),
                                  preferred_element_type=F32)
            upd_ref[ci, pr] = jnp.where(same_head, upd, 0.0)

    for pr in range(nh // 2):
        s_t = s_ref[pr]
        for ci in range(nchunk):
            sl = slice(ci * c, (ci + 1) * c)
            oi_ref[sl, pair_v[pr]] += lax.dot_general(q_in[ci][:, pair_k[pr]], s_t.astype(BF16),
                                                      (((1,), (1,)), ((), ())), preferred_element_type=F32)
            s_t = decay[ci][:, pair_k[pr]] * s_t + upd_ref[ci, pr]
        s_ref[pr] = s_t

    for h in range(nh):
        os_ = slice(GLA_HEAD_V * h, GLA_HEAD_V * (h + 1))
        o_h = oi_ref[:, os_]
        ms = jnp.mean(o_h * o_h, axis=-1, keepdims=True)
        o_n = o_h * lax.rsqrt(ms + RMS_EPS) * gain_ref[:, os_]
        o_ref[0, :, os_] = (o_n * _silu(g_ref[0, :, os_].astype(F32))).astype(BF16)


def _gla_group(q, k, v, g_out, pre, b_alpha, head_gain, tg):
    bsz, seq, _ = q.shape
    blk = lambda w: pl.BlockSpec((1, tg, w), lambda b, t: (b, t, 0))
    ba = b_alpha.reshape(1, GLA_DK)
    gain = head_gain.reshape(1, GLA_WIDTH)
    return pl.pallas_call(
        functools.partial(_gla_kernel, nchunk=tg // GLA_CHUNK),
        grid=(bsz, seq // tg),
        in_specs=[
            blk(GLA_DK), blk(GLA_DK), blk(GLA_WIDTH), blk(GLA_WIDTH), blk(GLA_DK),
            pl.BlockSpec((1, GLA_DK), lambda b, t: (0, 0)),
            pl.BlockSpec((1, GLA_WIDTH), lambda b, t: (0, 0)),
        ],
        out_specs=blk(GLA_WIDTH),
        out_shape=jax.ShapeDtypeStruct((bsz, seq, GLA_WIDTH), BF16),
        scratch_shapes=[
            pltpu.VMEM((GLA_HEADS // 2, 2 * GLA_HEAD_V, 2 * GLA_HEAD_K), F32),
            pltpu.VMEM((tg, GLA_WIDTH), F32),
            pltpu.VMEM((tg // GLA_CHUNK, GLA_HEADS // 2, 2 * GLA_HEAD_V, 2 * GLA_HEAD_K), F32),
        ],
        compiler_params=_cparams("parallel", "arbitrary"),
        name="gla_group",
    )(q, k, v, g_out, pre, ba, gain)


def _mix_out(ys_ref, yg_ref, x_ref, mod_ref, w_ref, lg_ref, lb_ref, alpha):
    y = _dot(ys_ref[...], w_ref[0:S5_WIDTH, :]) + _dot(yg_ref[0], w_ref[S5_WIDTH:, :])
    gt = mod_ref[0, 2:3, :]
    x1 = _layer_norm(alpha * x_ref[0] + (1.0 + gt) * y, lg_ref[...], lb_ref[...])
    h2 = x1 * (1.0 + mod_ref[0, 4:5, :]) + mod_ref[0, 3:4, :]
    return x1, h2


RT_E1, RT_E2, RT_R1, RT_R2, RT_W1, RT_W2 = range(6)


def _top2_route(logits, count):
    tm = logits.shape[0]
    lane = lax.broadcasted_iota(jnp.int32, logits.shape, 1).astype(F32)
    l0 = jnp.where(lane < N_EXPERTS, logits, NEG_BIG)
    m1 = jnp.max(l0, axis=-1, keepdims=True)
    i1 = jnp.min(jnp.where(l0 == m1, lane, float(ROUTER_LANES)), axis=-1, keepdims=True)
    sel1 = lane == i1
    l1 = jnp.where(sel1, NEG_BIG, l0)
    m2 = jnp.max(l1, axis=-1, keepdims=True)
    i2 = jnp.min(jnp.where(l1 == m2, lane, float(ROUTER_LANES)), axis=-1, keepdims=True)
    sel2 = lane == i2
    e2 = jnp.exp(m2 - m1)
    w1 = 1.0 / (1.0 + e2)
    w2 = e2 / (1.0 + e2)
    chosen = jnp.where(sel1, 1.0, 0.0) + jnp.where(sel2, 1.0, 0.0)
    row = lax.broadcasted_iota(jnp.int32, (tm, tm), 0)
    col = lax.broadcasted_iota(jnp.int32, (tm, tm), 1)
    earlier = jnp.where(row > col, 1.0, 0.0).astype(BF16)
    before = _dot(earlier, chosen.astype(BF16)) + count
    r1 = jnp.sum(jnp.where(sel1, before, 0.0), axis=-1, keepdims=True)
    r2 = jnp.sum(jnp.where(sel2, before, 0.0), axis=-1, keepdims=True)
    rec = jnp.zeros_like(logits)
    for ln, val in ((RT_E1, i1), (RT_E2, i2), (RT_R1, r1), (RT_R2, r2), (RT_W1, w1), (RT_W2, w2)):
        rec = jnp.where(lane == float(ln), val, rec)
    return rec, count + jnp.sum(chosen, axis=0, keepdims=True)


def _outproj_router_kernel(ys_ref, yg_ref, x_ref, mod_ref, w_ref, lg_ref, lb_ref, wr_ref, br_ref,
                           x1_ref, route_ref, count_ref, cnt, *, alpha):
    @pl.when((pl.program_id(0) == 0) & (pl.program_id(1) == 0))
    def _():
        cnt[...] = jnp.zeros_like(cnt)

    x1, h2 = _mix_out(ys_ref, yg_ref, x_ref, mod_ref, w_ref, lg_ref, lb_ref, alpha)
    x1_ref[0] = x1
    logits = _dot_f32x3(h2, wr_ref[...]) + br_ref[...]
    rec, new_count = _top2_route(logits, cnt[...])
    route_ref[0] = rec
    cnt[...] = new_count
    count_ref[...] = jnp.broadcast_to(new_count, count_ref.shape)


def _out_projection_router(y_s5_tm, y_gla, x, mod_l, w_out, ln_g, ln_b, w_router, b_router, alpha, tm):
    bsz, seq, d = x.shape
    row = lambda a: a.reshape(1, -1)
    const = lambda shape: pl.BlockSpec(shape, lambda b, t: (0,) * len(shape))
    pad = ROUTER_LANES - N_EXPERTS
    return pl.pallas_call(
        functools.partial(_outproj_router_kernel, alpha=alpha),
        grid=(bsz, seq // tm),
        in_specs=[
            pl.BlockSpec((tm, S5_WIDTH), lambda b, t: (t, b)),
            pl.BlockSpec((1, tm, GLA_WIDTH), lambda b, t: (b, t, 0)),
            pl.BlockSpec((1, tm, d), lambda b, t: (b, t, 0)),
            pl.BlockSpec((1, N_MOD, d), lambda b, t: (b, 0, 0)),
            const((d, d)), const((1, d)), const((1, d)),
            const((d, ROUTER_LANES)), const((1, ROUTER_LANES)),
        ],
        out_specs=[
            pl.BlockSpec((1, tm, d), lambda b, t: (b, t, 0)),
            pl.BlockSpec((1, tm, ROUTER_LANES), lambda b, t: (b, t, 0)),
            const((8, ROUTER_LANES)),
        ],
        out_shape=[
            jax.ShapeDtypeStruct((bsz, seq, d), F32),
            jax.ShapeDtypeStruct((bsz, seq, ROUTER_LANES), F32),
            jax.ShapeDtypeStruct((8, ROUTER_LANES), F32),
        ],
        scratch_shapes=[pltpu.VMEM((1, ROUTER_LANES), F32)],
        compiler_params=_cparams("arbitrary", "arbitrary"),
        name="out_projection",
    )(y_s5_tm, y_gla, x, mod_l, w_out.astype(BF16), row(ln_g), row(ln_b),
      jnp.pad(w_router, ((0, 0), (0, pad))), jnp.pad(row(b_router), ((0, 0), (0, pad))))


def _mix_ffn_kernel(ys_ref, yg_ref, x_ref, mod_ref, wo_ref, lg1_ref, lb1_ref, wg_ref, wu_ref, wd_ref,
                    lg2_ref, lb2_ref, o_ref, acc_ref, x1_ref, hb_ref, *, alpha):
    j = pl.program_id(1)

    @pl.when(j == 0)
    def _():
        x1, h2 = _mix_out(ys_ref, yg_ref, x_ref, mod_ref, wo_ref, lg1_ref, lb1_ref, alpha)
        x1_ref[...] = x1
        hb_ref[...] = h2.astype(BF16)
        acc_ref[...] = jnp.zeros_like(acc_ref)

    h = hb_ref[...]
    a = _dot(h, wg_ref[...])
    mid = (_silu(a) * _dot(h, wu_ref[...])).astype(BF16)
    acc_ref[...] += _dot(mid, wd_ref[...])

    @pl.when(j == pl.num_programs(1) - 1)
    def _():
        gt = mod_ref[0, 5:6, :]
        o_ref[0] = _layer_norm(alpha * x1_ref[...] + (1.0 + gt) * acc_ref[...], lg2_ref[...], lb2_ref[...])


def _mix_dense_ffn(y_s5_tm, y_gla, x, mod_l, w_out, ln1_g, ln1_b, w_gate, w_up, w_down, ln2_g, ln2_b,
                   alpha, tm, tf):
    bsz, seq, d = x.shape
    dff = w_gate.shape[1]
    per_b = seq // tm
    row = lambda a: a.reshape(1, d)
    const = lambda shape: pl.BlockSpec(shape, lambda i, j: (0,) * len(shape))
    tok = lambda w: pl.BlockSpec((1, tm, w), lambda i, j: (i // per_b, i % per_b, 0))
    return pl.pallas_call(
        functools.partial(_mix_ffn_kernel, alpha=alpha),
        grid=(bsz * per_b, dff // tf),
        in_specs=[
            pl.BlockSpec((tm, S5_WIDTH), lambda i, j: (i % per_b, i // per_b)),
            tok(GLA_WIDTH),
            tok(d),
            pl.BlockSpec((1, N_MOD, d), lambda i, j: (i // per_b, 0, 0)),
            const((d, d)), const((1, d)), const((1, d)),
            pl.BlockSpec((d, tf), lambda i, j: (0, j)),
            pl.BlockSpec((d, tf), lambda i, j: (0, j)),
            pl.BlockSpec((tf, d), lambda i, j: (j, 0)),
            const((1, d)), const((1, d)),
        ],
        out_specs=tok(d),
        out_shape=jax.ShapeDtypeStruct((bsz, seq, d), F32),
        scratch_shapes=[pltpu.VMEM((tm, d), F32), pltpu.VMEM((tm, d), F32), pltpu.VMEM((tm, d), BF16)],
        compiler_params=_cparams("parallel", "arbitrary"),
        name="mix_dense_ffn",
    )(y_s5_tm, y_gla, x, mod_l, w_out.astype(BF16), row(ln1_g), row(ln1_b),
      w_gate.astype(BF16), w_up.astype(BF16), w_down.astype(BF16), row(ln2_g), row(ln2_b))


def _route_tables(route, count, rb, nblk):
    e = route[:, RT_E1:RT_E2 + 1].astype(jnp.int32)
    rank = route[:, RT_R1:RT_R2 + 1].astype(jnp.int32)
    cnt = count[0, :N_EXPERTS].astype(jnp.int32)
    padded = ((cnt + rb - 1) // rb) * rb
    ends = jnp.cumsum(padded)
    starts = ends - padded
    pos = jnp.sum(jnp.where(e[..., None] == jnp.arange(N_EXPERTS), starts, 0), axis=-1) + rank
    n_valid = ends[-1] // rb
    blk = jnp.minimum(jnp.arange(nblk, dtype=jnp.int32), n_valid - 1)
    blk_expert = jnp.sum((blk[:, None] * rb >= ends[None, :]).astype(jnp.int32), axis=-1)
    blk_expert = jnp.minimum(blk_expert, N_EXPERTS - 1)
    n_rows = jnp.full((1,), nblk * rb, jnp.int32)
    fill = jnp.stack([jnp.concatenate([starts + cnt, ends[-1:]]), jnp.concatenate([ends, n_rows])], axis=1)
    return (pos.astype(jnp.int32), blk_expert.astype(jnp.int32), n_valid.astype(jnp.int32).reshape(1),
            fill.reshape(-1).astype(jnp.int32))


SUBLANES = 8


def _staged_row(ref, tile, sub):
    return ref.at[tile, pl.ds(sub, 1), :]


def _dispatch_kernel(fill_ref, pos_ref, x_ref, mod_ref, xs_hbm, hs_ref, sem, *, tb):
    i = pl.program_id(0)
    h = x_ref[...] * (1.0 + mod_ref[0, 4:5, :]) + mod_ref[0, 3:4, :]
    hs_ref[...] = h.reshape(tb // SUBLANES, SUBLANES, h.shape[-1])

    def issue(r8, carry):
        for sub in range(SUBLANES):
            for kk in range(2):
                dst_row = pos_ref[0, 0, 2 * SUBLANES * r8 + 2 * sub + kk]
                pltpu.make_async_copy(_staged_row(hs_ref, r8, sub), xs_hbm.at[pl.ds(dst_row, 1), :],
                                      sem.at[0]).start()
        return carry

    lax.fori_loop(0, tb // SUBLANES, issue, 0)

    @pl.when(i == 0)
    def _():
        for e in range(N_EXPERTS + 1):
            lo = fill_ref[2 * e]
            hi = fill_ref[2 * e + 1]

            def fill(p, carry):
                pltpu.make_async_copy(_staged_row(hs_ref, 0, 0), xs_hbm.at[pl.ds(p, 1), :], sem.at[1]).start()
                return carry

            def drain(p, carry):
                pltpu.make_async_copy(_staged_row(hs_ref, 0, 0), xs_hbm.at[pl.ds(p, 1), :], sem.at[1]).wait()
                return carry

            lax.fori_loop(lo, hi, fill, 0)
            lax.fori_loop(lo, hi, drain, 0)

    for _ in range(2):
        pltpu.make_async_copy(x_ref, xs_hbm.at[pl.ds(0, tb), :], sem.at[0]).wait()


def _dispatch(x1, mod_l, pos, fill, n_rows, tb, seq):
    n, d = x1.shape
    per_b = seq // tb
    grid_spec = pltpu.PrefetchScalarGridSpec(
        num_scalar_prefetch=1,
        grid=(n // tb,),
        in_specs=[
            pl.BlockSpec((1, 1, 2 * tb), lambda i, f: (i, 0, 0), memory_space=pltpu.SMEM),
            pl.BlockSpec((tb, d), lambda i, f: (i, 0)),
            pl.BlockSpec((1, N_MOD, d), lambda i, f: (i // per_b, 0, 0)),
        ],
        out_specs=pl.BlockSpec(memory_space=pl.ANY),
        scratch_shapes=[pltpu.VMEM((tb // SUBLANES, SUBLANES, d), F32), pltpu.SemaphoreType.DMA((2,))],
    )
    return pl.pallas_call(
        functools.partial(_dispatch_kernel, tb=tb),
        grid_spec=grid_spec,
        out_shape=jax.ShapeDtypeStruct((n_rows, d), F32),
        compiler_params=_cparams("arbitrary"),
        name="moe_dispatch",
    )(fill, pos.reshape(n // tb, 1, 2 * tb), x1, mod_l)


def _expert_kernel(be_ref, nv_ref, xs_ref, wg_ref, wu_ref, wd_ref, y_ref, acc_ref, xb_ref):
    del be_ref
    i = pl.program_id(0)
    j = pl.program_id(1)

    @pl.when(i < nv_ref[0])
    def _():
        @pl.when(j == 0)
        def _():
            acc_ref[...] = jnp.zeros_like(acc_ref)
            xb_ref[...] = xs_ref[...].astype(BF16)

        h = xb_ref[...]
        a = _dot(h, wg_ref[0])
        mid = (_silu(a) * _dot(h, wu_ref[0])).astype(BF16)
        acc_ref[...] += _dot(mid, wd_ref[0])

        @pl.when(j == pl.num_programs(1) - 1)
        def _():
            y_ref[...] = acc_ref[...]

    @pl.when((i >= nv_ref[0]) & (j == 0))
    def _():
        y_ref[...] = jnp.zeros_like(y_ref)


def _experts(xs, blk_expert, n_valid, w_gate, w_up, w_down, rb, tf):
    n_rows, d = xs.shape
    dff = w_gate.shape[2]
    nj = dff // tf
    row_blk = lambda i, j, be, nv: (jnp.minimum(i, nv[0] - 1), 0)
    jj = lambda i, j, nv: jnp.where(i < nv[0], j, nj - 1)
    grid_spec = pltpu.PrefetchScalarGridSpec(
        num_scalar_prefetch=2,
        grid=(n_rows // rb, nj),
        in_specs=[
            pl.BlockSpec((rb, d), row_blk),
            pl.BlockSpec((1, d, tf), lambda i, j, be, nv: (be[i], 0, jj(i, j, nv))),
            pl.BlockSpec((1, d, tf), lambda i, j, be, nv: (be[i], 0, jj(i, j, nv))),
            pl.BlockSpec((1, tf, d), lambda i, j, be, nv: (be[i], jj(i, j, nv), 0)),
        ],
        out_specs=pl.BlockSpec((rb, d), lambda i, j, be, nv: (i, 0)),
        scratch_shapes=[pltpu.VMEM((rb, d), F32), pltpu.VMEM((rb, d), BF16)],
    )
    return pl.pallas_call(
        _expert_kernel,
        grid_spec=grid_spec,
        out_shape=jax.ShapeDtypeStruct((n_rows, d), F32),
        compiler_params=_cparams("arbitrary", "arbitrary"),
        name="moe_experts",
    )(blk_expert, n_valid, xs, w_gate.astype(BF16), w_up.astype(BF16), w_down.astype(BF16))


def _combine_kernel(pos_ref, posn_ref, route_ref, x_ref, mod_ref, lg_ref, lb_ref, y_hbm, o_ref, buf, sem,
                    *, tb, alpha):
    i = pl.program_id(0)
    slot = i % 2
    d = o_ref.shape[-1]
    ntile = tb // SUBLANES

    def issue_tile(p_ref, s, r8):
        for sub in range(SUBLANES):
            for kk in range(2):
                src_row = p_ref[0, 0, 2 * SUBLANES * r8 + 2 * sub + kk]
                pltpu.make_async_copy(y_hbm.at[pl.ds(src_row, 1), :], _staged_row(buf.at[s, kk], r8, sub),
                                      sem.at[s]).start()

    def wait_slot(s):
        for _ in range(2):
            pltpu.make_async_copy(y_hbm.at[pl.ds(0, tb), :], o_ref, sem.at[s]).wait()

    def issue_all(p_ref, s):
        def body(r8, carry):
            issue_tile(p_ref, s, r8)
            return carry
        lax.fori_loop(0, ntile, body, 0)

    @pl.when(i == 0)
    def _():
        issue_all(pos_ref, 0)

    issue_all(posn_ref, 1 - slot)
    wait_slot(slot)
    rec = route_ref[...]
    y0 = buf[slot, 0].reshape(tb, d)
    y1 = buf[slot, 1].reshape(tb, d)
    f = rec[:, RT_W1:RT_W1 + 1] * y0 + rec[:, RT_W2:RT_W2 + 1] * y1
    gt = mod_ref[0, 5:6, :]
    o_ref[...] = _layer_norm(alpha * x_ref[...] + (1.0 + gt) * f, lg_ref[...], lb_ref[...])

    @pl.when(i == pl.num_programs(0) - 1)
    def _():
        wait_slot(1 - slot)


def _combine(ys, pos, route, x1, mod_l, ln_g, ln_b, alpha, tb, seq):
    n, d = x1.shape
    nb = n // tb
    per_b = seq // tb
    pos3 = pos.reshape(nb, 1, 2 * tb)
    return pl.pallas_call(
        functools.partial(_combine_kernel, tb=tb, alpha=alpha),
        grid=(nb,),
        in_specs=[
            pl.BlockSpec((1, 1, 2 * tb), lambda i: (i, 0, 0), memory_space=pltpu.SMEM),
            pl.BlockSpec((1, 1, 2 * tb), lambda i: (jnp.minimum(i + 1, nb - 1), 0, 0), memory_space=pltpu.SMEM),
            pl.BlockSpec((tb, ROUTER_LANES), lambda i: (i, 0)),
            pl.BlockSpec((tb, d), lambda i: (i, 0)),
            pl.BlockSpec((1, N_MOD, d), lambda i: (i // per_b, 0, 0)),
            pl.BlockSpec((1, d), lambda i: (0, 0)),
            pl.BlockSpec((1, d), lambda i: (0, 0)),
            pl.BlockSpec(memory_space=pl.ANY),
        ],
        out_specs=pl.BlockSpec((tb, d), lambda i: (i, 0)),
        out_shape=jax.ShapeDtypeStruct((n, d), F32),
        scratch_shapes=[pltpu.VMEM((2, 2, tb // SUBLANES, SUBLANES, d), F32), pltpu.SemaphoreType.DMA((2,))],
        compiler_params=_cparams("arbitrary"),
        name="moe_combine",
    )(pos3, pos3, route, x1, mod_l, ln_g.reshape(1, d), ln_b.reshape(1, d), ys)


def _moe_ffn(x1, route, count, mod_l, w_gate, w_up, w_down, ln_g, ln_b, alpha, rb, tf, tb):
    bsz, seq, d = x1.shape
    n = bsz * seq
    n_rows = 2 * n + N_EXPERTS * rb
    route2 = route.reshape(n, ROUTER_LANES)
    pos, blk_expert, n_valid, fill = _route_tables(route2, count, rb, n_rows // rb)
    xs = _dispatch(x1.reshape(n, d), mod_l, pos, fill, n_rows, rb, seq)
    ys = _experts(xs, blk_expert, n_valid, w_gate, w_up, w_down, rb, tf)
    out = _combine(ys, pos, route2, x1.reshape(n, d), mod_l, ln_g, ln_b, alpha, tb, seq)
    return out.reshape(bsz, seq, d)


def _tile(n, want):
    t = min(n, want)
    assert n % t == 0, (n, want)
    return t


def kernel(x, c, mod_w, mod_b, w_in, w_out, s5_lam_re, s5_lam_im, s5_log_dt, s5_b_re, s5_b_im, s5_c_re, s5_c_im, s5_d, s5_w_glu, s5_b_glu, gla_w_alpha_up, gla_b_alpha, gla_head_gain, ln_mix_g, ln_mix_b, ffn_w_gate, ffn_w_up, ffn_w_down, moe_w_router, moe_b_router, moe_w_gate, moe_w_up, moe_w_down, ln_ffn_g, ln_ffn_b):
    bsz, seq, d = x.shape
    depth = mod_w.shape[0]
    alpha = (2.0 * depth) ** 0.25
    tm = _tile(seq, 512)
    tt = _tile(seq, 16)
    tg = _tile(seq, 512)
    tm_ffn = _tile(seq, 1024)
    tf = _tile(D_FF, 512)
    tb = _tile(seq, 256)

    mod = _modulation(c, mod_w, mod_b).reshape(depth, bsz, N_MOD, d)
    for layer in range(depth):
        mod_l = mod[layer]
        w_gate_fold = _gate_fold(w_in[layer][:, IN_P[0]:], gla_w_alpha_up[layer])
        w_cat = jnp.concatenate([w_in[layer][:, :IN_P[0]], w_gate_fold], axis=1).astype(BF16)
        u_tm, q, k, v, g_out, pre = _in_projection(x, mod_l, w_cat, tm)

        s5w = _s5_weights(s5_lam_re[layer], s5_lam_im[layer], s5_log_dt[layer], s5_b_re[layer], s5_b_im[layer],
                          s5_c_re[layer], s5_c_im[layer])
        y_s5 = _s5_group(u_tm.reshape(seq, bsz, S5_WIDTH), *s5w, s5_d[layer], s5_w_glu[layer], s5_b_glu[layer], tt)
        y_s5 = y_s5.reshape(seq, bsz * S5_WIDTH)
        y_gla = _gla_group(q, k, v, g_out, pre, gla_b_alpha[layer], gla_head_gain[layer], tg)

        i = layer // 2
        if layer % 2 == 0:
            x = _mix_dense_ffn(y_s5, y_gla, x, mod_l, w_out[layer], ln_mix_g[layer], ln_mix_b[layer],
                               ffn_w_gate[i], ffn_w_up[i], ffn_w_down[i], ln_ffn_g[layer], ln_ffn_b[layer],
                               alpha, tm_ffn, tf)
        else:
            x1, route, count = _out_projection_router(y_s5, y_gla, x, mod_l, w_out[layer], ln_mix_g[layer],
                                                      ln_mix_b[layer], moe_w_router[i], moe_b_router[i], alpha, tm)
            x = _moe_ffn(x1, route, count, mod_l, moe_w_gate[i], moe_w_up[i], moe_w_down[i],
                         ln_ffn_g[layer], ln_ffn_b[layer], alpha, tm_ffn, tf, tb)
    return x
```

```python
import functools
import math

import jax
import jax.numpy as jnp
from jax import lax
from jax.experimental import pallas as pl
from jax.experimental.pallas import tpu as pltpu

F32 = jnp.float32
BF16 = jnp.bfloat16

D_MODEL = 1024
S5_WIDTH = 512
S5_GROUP_CH = 16
S5_GROUPS = 32
S5_STATE = 64
S5_LANES = S5_GROUPS * S5_STATE
GLA_WIDTH = 512
GLA_HEADS = 4
GLA_DK = 256
GLA_HEAD_K = 64
GLA_HEAD_V = 128
GLA_GATE_RANK = 16
GLA_GATE_TAU = 16.0
GLA_CHUNK = 64
D_FF = 3584
N_EXPERTS = 8
N_MOD = 6
LN_EPS = 1e-5
RMS_EPS = 1e-6
ROUTER_LANES = 128
NEG_BIG = -1e30

VMEM_LIMIT = 56 * 1024 * 1024


def _cparams(*sem):
    return pltpu.CompilerParams(dimension_semantics=sem, vmem_limit_bytes=VMEM_LIMIT)


def _dot(a, b):
    return jnp.dot(a, b, preferred_element_type=F32)


def _split2(a):
    a1 = a.astype(BF16)
    a2 = (a - a1.astype(F32)).astype(BF16)
    return a1, a2


def _split3(a):
    a1 = a.astype(BF16)
    r1 = a - a1.astype(F32)
    a2 = r1.astype(BF16)
    a3 = (r1 - a2.astype(F32)).astype(BF16)
    return a1, a2, a3


def _dot_f32(a, b):
    a1, a2, a3 = _split3(a)
    b1, b2, b3 = _split3(b)
    lo = _dot(a1, b3) + _dot(a2, b2) + _dot(a3, b1)
    mid = _dot(a1, b2) + _dot(a2, b1)
    return lo + mid + _dot(a1, b1)


def _dot_f32x3(a, b):
    a1, a2 = _split2(a)
    b1, b2 = _split2(b)
    return (_dot(a1, b2) + _dot(a2, b1)) + _dot(a1, b1)


def _silu(x):
    return x * jax.nn.sigmoid(x)


def _layer_norm(r, gain, bias):
    mu = jnp.mean(r, axis=-1, keepdims=True)
    d = r - mu
    var = jnp.mean(d * d, axis=-1, keepdims=True)
    return d * lax.rsqrt(var + LN_EPS) * gain + bias


def _mod_kernel(c_ref, w_ref, b_ref, o_ref):
    o_ref[0] = _dot_f32(_silu(c_ref[...]), w_ref[0]) + b_ref[0]


def _modulation(c, mod_w, mod_b):
    depth, d, nd = mod_w.shape
    bsz = c.shape[0]
    return pl.pallas_call(
        _mod_kernel,
        grid=(depth, nd // d),
        in_specs=[
            pl.BlockSpec((bsz, d), lambda l, n: (0, 0)),
            pl.BlockSpec((1, d, d), lambda l, n: (l, 0, n)),
            pl.BlockSpec((1, 1, d), lambda l, n: (l, 0, n)),
        ],
        out_specs=pl.BlockSpec((1, bsz, d), lambda l, n: (l, 0, n)),
        out_shape=jax.ShapeDtypeStruct((depth, bsz, nd), F32),
        compiler_params=_cparams("parallel", "parallel"),
        name="modulation",
    )(c, mod_w, mod_b.reshape(depth, 1, nd))


def _gate_fold_kernel(wl_ref, wu_ref, o_ref):
    o_ref[...] = _dot_f32(wl_ref[...], wu_ref[...])


def _gate_fold(w_low, w_up):
    d = w_low.shape[0]
    pad = 128 - GLA_GATE_RANK
    wl = jnp.pad(w_low, ((0, 0), (0, pad)))
    wu = jnp.pad(w_up, ((0, pad), (0, 0)))
    return pl.pallas_call(
        _gate_fold_kernel,
        out_shape=jax.ShapeDtypeStruct((d, GLA_DK), F32),
        name="gate_fold",
    )(wl, wu)


def _s5_disc_kernel(lre_ref, lim_ref, ldt_ref, bre_ref, bim_ref, lam_ref, bbre_ref, bbim_ref):
    lre = lre_ref[...]
    lim = lim_ref[...]
    dt = jnp.exp(ldt_ref[...])
    mag = jnp.exp(lre * dt)
    ang = lim * dt
    are = mag * jnp.cos(ang)
    aim = mag * jnp.sin(ang)
    lam_ref[0] = are
    lam_ref[1] = aim
    den = lre * lre + lim * lim
    nre = are - 1.0
    cre = (nre * lre + aim * lim) / den
    cim = (aim * lre - nre * lim) / den
    for h in range(S5_GROUP_CH):
        bre = bre_ref[h]
        bim = bim_ref[h]
        bbre_ref[h] = cre * bre - cim * bim
        bbim_ref[h] = cre * bim + cim * bre


def _s5_discretise(lam_re, lam_im, log_dt, b_re, b_im):
    g, p = lam_re.shape
    hch = b_re.shape[-1]
    bre_t = jnp.transpose(b_re, (2, 0, 1))
    bim_t = jnp.transpose(b_im, (2, 0, 1))
    lam, bbre, bbim = pl.pallas_call(
        _s5_disc_kernel,
        out_shape=(
            jax.ShapeDtypeStruct((2, g, p), F32),
            jax.ShapeDtypeStruct((hch, g, p), F32),
            jax.ShapeDtypeStruct((hch, g, p), F32),
        ),
        name="s5_discretise",
    )(lam_re, lam_im, log_dt.reshape(g, 1), bre_t, bim_t)
    return lam, bbre, bbim


def _s5_weights(lam_re, lam_im, log_dt, b_re, b_im, c_re, c_im):
    g, p, hch = S5_GROUPS, S5_STATE, S5_GROUP_CH
    lam, bbre, bbim = _s5_discretise(lam_re, lam_im, log_dt, b_re, b_im)
    lam_rows = lam.reshape(2, g * p)
    eye = jnp.eye(g, dtype=F32)

    def in_blocks(bb):
        dense = jnp.einsum('hgp,gk->ghkp', bb, eye).reshape(g * hch, g * p)
        blocks = [dense[128 * (n // 2):128 * (n // 2) + 128, 256 * n:256 * (n + 1)] for n in range(8)]
        return jnp.stack(blocks).astype(BF16)

    def out_blocks(cc, sign):
        dense = jnp.einsum('ghp,gk->gpkh', cc, eye).reshape(g * p, g * hch) * sign
        blocks = [dense[1024 * m:1024 * (m + 1), 256 * m:256 * (m + 1)] for m in range(2)]
        return jnp.stack(blocks).astype(BF16)

    return lam_rows, in_blocks(bbre), in_blocks(bbim), out_blocks(c_re, 1.0), out_blocks(c_im, -1.0)


IN_U = (0, 512)
IN_Q = (512, 768)
IN_K = (768, 1024)
IN_V = (1024, 1536)
IN_G = (1536, 2048)
IN_P = (2048, 2304)


def _inproj_kernel(x_ref, mod_ref, w_ref, u_ref, q_ref, k_ref, v_ref, g_ref, p_ref):
    sh = mod_ref[0, 0:1, :]
    sc = mod_ref[0, 1:2, :]
    h = (x_ref[0] * (1.0 + sc) + sh).astype(BF16)
    u_ref[...] = _dot(h, w_ref[:, IN_U[0]:IN_U[1]]).astype(BF16)
    q_ref[0] = _dot(h, w_ref[:, IN_Q[0]:IN_Q[1]]).astype(BF16)
    k_ref[0] = _dot(h, w_ref[:, IN_K[0]:IN_K[1]]).astype(BF16)
    v_ref[0] = _dot(h, w_ref[:, IN_V[0]:IN_V[1]]).astype(BF16)
    g_ref[0] = _dot(h, w_ref[:, IN_G[0]:IN_G[1]]).astype(BF16)
    p_ref[0] = _dot(h, w_ref[:, IN_P[0]:IN_P[1]])


def _in_projection(x, mod_l, w_cat, tm):
    bsz, seq, d = x.shape
    ncol = w_cat.shape[1]
    bspec = lambda w: pl.BlockSpec((1, tm, w), lambda b, t: (b, t, 0))
    return pl.pallas_call(
        _inproj_kernel,
        grid=(bsz, seq // tm),
        in_specs=[
            pl.BlockSpec((1, tm, d), lambda b, t: (b, t, 0)),
            pl.BlockSpec((1, N_MOD, d), lambda b, t: (b, 0, 0)),
            pl.BlockSpec((d, ncol), lambda b, t: (0, 0)),
        ],
        out_specs=[
            pl.BlockSpec((tm, S5_WIDTH), lambda b, t: (t, b)),
            bspec(GLA_DK), bspec(GLA_DK), bspec(GLA_WIDTH), bspec(GLA_WIDTH), bspec(GLA_DK),
        ],
        out_shape=[
            jax.ShapeDtypeStruct((seq, bsz * S5_WIDTH), BF16),
            jax.ShapeDtypeStruct((bsz, seq, GLA_DK), BF16),
            jax.ShapeDtypeStruct((bsz, seq, GLA_DK), BF16),
            jax.ShapeDtypeStruct((bsz, seq, GLA_WIDTH), BF16),
            jax.ShapeDtypeStruct((bsz, seq, GLA_WIDTH), BF16),
            jax.ShapeDtypeStruct((bsz, seq, GLA_DK), F32),
        ],
        compiler_params=_cparams("parallel", "parallel"),
        name="in_projection",
    )(x, mod_l, w_cat)


S5_SCAN_LANES = 256


def _s5_kernel(u_ref, lam_ref, wbre_ref, wbim_ref, wcre_ref, wcim_ref, d_ref, wglu_ref, bglu_ref,
               o_ref, hre, him, st_re, st_im, *, tt, nb):
    @pl.when(pl.program_id(0) == 0)
    def _():
        st_re[...] = jnp.zeros_like(st_re)
        st_im[...] = jnp.zeros_like(st_im)

    rows = tt * nb
    ub = u_ref[...].reshape(rows, S5_WIDTH)
    for n in range(8):
        lhs = ub[:, 128 * (n // 2):128 * (n // 2) + 128]
        hre[:, 256 * n:256 * (n + 1)] = _dot(lhs, wbre_ref[n])
        him[:, 256 * n:256 * (n + 1)] = _dot(lhs, wbim_ref[n])

    for c in range(S5_LANES // S5_SCAN_LANES):
        ls = slice(c * S5_SCAN_LANES, (c + 1) * S5_SCAN_LANES)
        lr = jnp.broadcast_to(lam_ref[0:1, ls], (nb, S5_SCAN_LANES))
        li = jnp.broadcast_to(lam_ref[1:2, ls], (nb, S5_SCAN_LANES))
        hr = st_re[:, ls]
        hi = st_im[:, ls]
        for t in range(tt):
            rs = slice(t * nb, (t + 1) * nb)
            nr = lr * hr - li * hi + hre[rs, ls]
            ni = lr * hi + li * hr + him[rs, ls]
            hre[rs, ls] = nr
            him[rs, ls] = ni
            hr, hi = nr, ni
        st_re[:, ls] = hr
        st_im[:, ls] = hi

    ys = []
    for m in range(2):
        ks = slice(1024 * m, 1024 * (m + 1))
        ys.append(_dot(hre[:, ks].astype(BF16), wcre_ref[m]) + _dot(him[:, ks].astype(BF16), wcim_ref[m]))
    y = jnp.concatenate(ys, axis=-1) + d_ref[...] * ub.astype(F32)
    y = 0.5 * y * (1.0 + jnp.tanh(math.sqrt(2.0 / math.pi) * (y + 0.044715 * (y * y * y))))
    z = y * jax.nn.sigmoid(_dot(y.astype(BF16), wglu_ref[...]) + bglu_ref[...])
    o_ref[...] = z.astype(BF16).reshape(tt, nb, S5_WIDTH)


def _s5_group(u_tm, lam_rows, wbre, wbim, wcre, wcim, d_skip, w_glu, b_glu, tt):
    seq, nb, width = u_tm.shape
    rows = tt * nb
    full = lambda a: pl.BlockSpec(a.shape, lambda t: (0,) * a.ndim)
    d2 = d_skip.reshape(1, width)
    b2 = b_glu.reshape(1, width)
    wg = w_glu.astype(BF16)
    return pl.pallas_call(
        functools.partial(_s5_kernel, tt=tt, nb=nb),
        grid=(seq // tt,),
        in_specs=[
            pl.BlockSpec((tt, nb, width), lambda t: (t, 0, 0)),
            full(lam_rows), full(wbre), full(wbim), full(wcre), full(wcim), full(d2), full(wg), full(b2),
        ],
        out_specs=pl.BlockSpec((tt, nb, width), lambda t: (t, 0, 0)),
        out_shape=jax.ShapeDtypeStruct((seq, nb, width), BF16),
        scratch_shapes=[
            pltpu.VMEM((rows, S5_LANES), F32),
            pltpu.VMEM((rows, S5_LANES), F32),
            pltpu.VMEM((nb, S5_LANES), F32),
            pltpu.VMEM((nb, S5_LANES), F32),
        ],
        compiler_params=_cparams("arbitrary"),
        name="s5_group",
    )(u_tm, lam_rows, wbre, wbim, wcre, wcim, d2, wg, b2)


def _gla_kernel(q_ref, k_ref, v_ref, g_ref, p_ref, ba_ref, gain_ref, o_ref, s_ref, oi_ref, upd_ref, *, nchunk):
    @pl.when(pl.program_id(1) == 0)
    def _():
        s_ref[...] = jnp.zeros_like(s_ref)

    c = GLA_CHUNK
    nh = GLA_HEADS
    srow = lax.broadcasted_iota(jnp.int32, (nh * c, c), 0)
    scol = lax.broadcasted_iota(jnp.int32, (nh * c, c), 1)
    causal = (srow & (c - 1)) >= scol
    lane = lax.broadcasted_iota(jnp.int32, (1, GLA_DK), 1)
    head_lanes = [(lane >= GLA_HEAD_K * h) & (lane < GLA_HEAD_K * (h + 1)) for h in range(nh)]
    vrow = lax.broadcasted_iota(jnp.int32, (2 * GLA_HEAD_V, 2 * GLA_HEAD_K), 0)
    kcol = lax.broadcasted_iota(jnp.int32, (2 * GLA_HEAD_V, 2 * GLA_HEAD_K), 1)
    same_head = (vrow < GLA_HEAD_V) == (kcol < GLA_HEAD_K)
    pair_k = [slice(2 * GLA_HEAD_K * pr, 2 * GLA_HEAD_K * (pr + 1)) for pr in range(nh // 2)]
    pair_v = [slice(2 * GLA_HEAD_V * pr, 2 * GLA_HEAD_V * (pr + 1)) for pr in range(nh // 2)]
    scale = GLA_HEAD_K ** -0.5
    mid = c // 2 - 1

    t = nchunk * c
    slab = math.gcd(t, 4 * c)
    trow = lax.broadcasted_iota(jnp.int32, (slab, slab), 0)
    tcol = lax.broadcasted_iota(jnp.int32, (slab, slab), 1)
    chunk_tri = jnp.where((tcol <= trow) & (tcol >= (trow & -c)), 1.0, 0.0).astype(BF16)
    pre = p_ref[0] + ba_ref[...]
    log_alpha = (jnp.minimum(pre, 0.0) - jnp.log(1.0 + jnp.exp(-jnp.abs(pre)))) * (1.0 / GLA_GATE_TAU)
    a1, a2, a3 = _split3(log_alpha)
    b = jnp.concatenate(
        [_dot(chunk_tri, a3[r0:r0 + slab]) + _dot(chunk_tri, a2[r0:r0 + slab]) + _dot(chunk_tri, a1[r0:r0 + slab])
         for r0 in range(0, t, slab)], axis=0).reshape(nchunk, c, GLA_DK)
    b_mid = b[:, mid:mid + 1, :]
    b_last = b[:, c - 1:c, :]
    q = (q_ref[0].astype(F32) * scale).reshape(nchunk, c, GLA_DK)
    k = k_ref[0].astype(F32).reshape(nchunk, c, GLA_DK)
    q_in = (q * jnp.exp(b)).astype(BF16)
    q_e = q * jnp.exp(b - b_mid)
    k_e = (k * jnp.exp(b_mid - b)).astype(BF16)
    k_d = (k * jnp.exp(b_last - b)).astype(BF16)
    decay = jnp.exp(b_last)
    for ci in range(nchunk):
        sl = slice(ci * c, (ci + 1) * c)
        q_stack = jnp.concatenate([jnp.where(head_lanes[h], q_e[ci], 0.0) for h in range(nh)], axis=0).astype(BF16)
        scores = lax.dot_general(q_stack, k_e[ci], (((1,), (1,)), ((), ())),
                                 preferred_element_type=F32)
        scores = jnp.where(causal, scores, 0.0).astype(BF16)
        v_all = v_ref[0, sl, :]
        pv = _dot(scores, v_all)
        for h in range(nh):
            os_ = slice(GLA_HEAD_V * h, GLA_HEAD_V * (h + 1))
            oi_ref[sl, os_] = pv[c * h:c * (h + 1), os_]
        for pr in range(nh // 2):
            upd = lax.dot_general(v_all[:, pair_v[pr]], k_d[ci][:, pair_k[pr]], (((0,), (0,)), ((), ())),
                                  preferred_element_type=F32)
            upd_ref[ci, pr] = jnp.where(same_head, upd, 0.0)

    for pr in range(nh // 2):
        s_t = s_ref[pr]
        for ci in range(nchunk):
            sl = slice(ci * c, (ci + 1) * c)
            oi_ref[sl, pair_v[pr]] += lax.dot_general(q_in[ci][:, pair_k[pr]], s_t.astype(BF16),
                                                      (((1,), (1,)), ((), ())), preferred_element_type=F32)
            s_t = decay[ci][:, pair_k[pr]] * s_t + upd_ref[ci, pr]
        s_ref[pr] = s_t

    for h in range(nh):
        os_ = slice(GLA_HEAD_V * h, GLA_HEAD_V * (h + 1))
        o_h = oi_ref[:, os_]
        ms = jnp.mean(o_h * o_h, axis=-1, keepdims=True)
        o_n = o_h * lax.rsqrt(ms + RMS_EPS) * gain_ref[:, os_]
        o_ref[0, :, os_] = (o_n * _silu(g_ref[0, :, os_].astype(F32))).astype(BF16)


def _gla_group(q, k, v, g_out, pre, b_alpha, head_gain, tg):
    bsz, seq, _ = q.shape
    blk = lambda w: pl.BlockSpec((1, tg, w), lambda b, t: (b, t, 0))
    ba = b_alpha.reshape(1, GLA_DK)
    gain = head_gain.reshape(1, GLA_WIDTH)
    return pl.pallas_call(
        functools.partial(_gla_kernel, nchunk=tg // GLA_CHUNK),
        grid=(bsz, seq // tg),
        in_specs=[
            blk(GLA_DK), blk(GLA_DK), blk(GLA_WIDTH), blk(GLA_WIDTH), blk(GLA_DK),
            pl.BlockSpec((1, GLA_DK), lambda b, t: (0, 0)),
            pl.BlockSpec((1, GLA_WIDTH), lambda b, t: (0, 0)),
        ],
        out_specs=blk(GLA_WIDTH),
        out_shape=jax.ShapeDtypeStruct((bsz, seq, GLA_WIDTH), BF16),
        scratch_shapes=[
            pltpu.VMEM((GLA_HEADS // 2, 2 * GLA_HEAD_V, 2 * GLA_HEAD_K), F32),
            pltpu.VMEM((tg, GLA_WIDTH), F32),
            pltpu.VMEM((tg // GLA_CHUNK, GLA_HEADS // 2, 2 * GLA_HEAD_V, 2 * GLA_HEAD_K), F32),
        ],
        compiler_params=_cparams("parallel", "arbitrary"),
        name="gla_group",
    )(q, k, v, g_out, pre, ba, gain)


def _mix_out(ys_ref, yg_ref, x_ref, mod_ref, w_ref, lg_ref, lb_ref, alpha):
    y = _dot(ys_ref[...], w_ref[0:S5_WIDTH, :]) + _dot(yg_ref[0], w_ref[S5_WIDTH:, :])
    gt = mod_ref[0, 2:3, :]
    x1 = _layer_norm(alpha * x_ref[0] + (1.0 + gt) * y, lg_ref[...], lb_ref[...])
    h2 = x1 * (1.0 + mod_ref[0, 4:5, :]) + mod_ref[0, 3:4, :]
    return x1, h2


RT_E1, RT_E2, RT_R1, RT_R2, RT_W1, RT_W2 = range(6)


def _top2_route(logits, count, earlier):
    lane =lax.broadcasted_iota(jnp.int32, logits.shape, 1).astype(F32)
    l0 = jnp.where(lane < N_EXPERTS, logits, NEG_BIG)
    m1 = jnp.max(l0, axis=-1, keepdims=True)
    i1 = jnp.min(jnp.where(l0 == m1, lane, float(ROUTER_LANES)), axis=-1, keepdims=True)
    sel1 = lane == i1
    l1 = jnp.where(sel1, NEG_BIG, l0)
    m2 = jnp.max(l1, axis=-1, keepdims=True)
    i2 = jnp.min(jnp.where(l1 == m2, lane, float(ROUTER_LANES)), axis=-1, keepdims=True)
    sel2 = lane == i2
    e2 = jnp.exp(m2 - m1)
    w1 = 1.0 / (1.0 + e2)
    w2 = e2 / (1.0 + e2)
    chosen = jnp.where(sel1, 1.0, 0.0) + jnp.where(sel2, 1.0, 0.0)
    before = _dot(earlier, chosen.astype(BF16)) + count
    r1 = jnp.sum(jnp.where(sel1, before, 0.0), axis=-1, keepdims=True)
    r2 = jnp.sum(jnp.where(sel2, before, 0.0), axis=-1, keepdims=True)
    rec = jnp.zeros_like(logits)
    for ln, val in ((RT_E1, i1), (RT_E2, i2), (RT_R1, r1), (RT_R2, r2), (RT_W1, w1), (RT_W2, w2)):
        rec = jnp.where(lane == float(ln), val, rec)
    return rec, count + jnp.sum(chosen, axis=0, keepdims=True)


def _outproj_router_kernel(ys_ref, yg_ref, x_ref, mod_ref, w_ref, lg_ref, lb_ref, wr_ref, br_ref, earlier_ref,
                           x1_ref, route_ref, count_ref, cnt, *, alpha):
    @pl.when((pl.program_id(0) == 0) & (pl.program_id(1) == 0))
    def _():
        cnt[...] = jnp.zeros_like(cnt)

    x1, h2 = _mix_out(ys_ref, yg_ref, x_ref, mod_ref, w_ref, lg_ref, lb_ref, alpha)
    x1_ref[0] = x1
    logits = _dot_f32x3(h2, wr_ref[...]) + br_ref[...]
    rec, new_count = _top2_route(logits, cnt[...], earlier_ref[...])
    route_ref[0] = rec
    cnt[...] = new_count
    count_ref[...] = jnp.broadcast_to(new_count, count_ref.shape)


def _out_projection_router(y_s5_tm, y_gla, x, mod_l, w_out, ln_g, ln_b, w_router, b_router, alpha, tm):
    bsz, seq, d = x.shape
    row = lambda a: a.reshape(1, -1)
    const = lambda shape: pl.BlockSpec(shape, lambda b, t: (0,) * len(shape))
    pad = ROUTER_LANES - N_EXPERTS
    return pl.pallas_call(
        functools.partial(_outproj_router_kernel, alpha=alpha),
        grid=(bsz, seq // tm),
        in_specs=[
            pl.BlockSpec((tm, S5_WIDTH), lambda b, t: (t, b)),
            pl.BlockSpec((1, tm, GLA_WIDTH), lambda b, t: (b, t, 0)),
            pl.BlockSpec((1, tm, d), lambda b, t: (b, t, 0)),
            pl.BlockSpec((1, N_MOD, d), lambda b, t: (b, 0, 0)),
            const((d, d)), const((1, d)), const((1, d)),
            const((d, ROUTER_LANES)), const((1, ROUTER_LANES)), const((tm, tm)),
        ],
        out_specs=[
            pl.BlockSpec((1, tm, d), lambda b, t: (b, t, 0)),
            pl.BlockSpec((1, tm, ROUTER_LANES), lambda b, t: (b, t, 0)),
            const((8, ROUTER_LANES)),
        ],
        out_shape=[
            jax.ShapeDtypeStruct((bsz, seq, d), F32),
            jax.ShapeDtypeStruct((bsz, seq, ROUTER_LANES), F32),
            jax.ShapeDtypeStruct((8, ROUTER_LANES), F32),
        ],
        scratch_shapes=[pltpu.VMEM((1, ROUTER_LANES), F32)],
        compiler_params=_cparams("arbitrary", "arbitrary"),
        name="out_projection",
    )(y_s5_tm, y_gla, x, mod_l, w_out.astype(BF16), row(ln_g), row(ln_b),
      jnp.pad(w_router, ((0, 0), (0, pad))), jnp.pad(row(b_router), ((0, 0), (0, pad))),
      jnp.tri(tm, k=-1, dtype=BF16))


def _mix_ffn_kernel(ys_ref, yg_ref, x_ref, mod_ref, wo_ref, lg1_ref, lb1_ref, wg_ref, wu_ref, wd_ref,
                    lg2_ref, lb2_ref, o_ref, acc_ref, x1_ref, hb_ref, *, alpha):
    j = pl.program_id(1)

    @pl.when(j == 0)
    def _():
        x1, h2 = _mix_out(ys_ref, yg_ref, x_ref, mod_ref, wo_ref, lg1_ref, lb1_ref, alpha)
        x1_ref[...] = x1
        hb_ref[...] = h2.astype(BF16)
        acc_ref[...] = jnp.zeros_like(acc_ref)

    h = hb_ref[...]
    a = _dot(h, wg_ref[...])
    mid = (_silu(a) * _dot(h, wu_ref[...])).astype(BF16)
    acc_ref[...] += _dot(mid, wd_ref[...])

    @pl.when(j == pl.num_programs(1) - 1)
    def _():
        gt = mod_ref[0, 5:6, :]
        o_ref[0] = _layer_norm(alpha * x1_ref[...] + (1.0 + gt) * acc_ref[...], lg2_ref[...], lb2_ref[...])


def _mix_dense_ffn(y_s5_tm, y_gla, x, mod_l, w_out, ln1_g, ln1_b, w_gate, w_up, w_down, ln2_g, ln2_b,
                   alpha, tm, tf):
    bsz, seq, d = x.shape
    dff = w_gate.shape[1]
    per_b = seq // tm
    row = lambda a: a.reshape(1, d)
    const = lambda shape: pl.BlockSpec(shape, lambda i, j: (0,) * len(shape))
    tok = lambda w: pl.BlockSpec((1, tm, w), lambda i, j: (i // per_b, i % per_b, 0))
    return pl.pallas_call(
        functools.partial(_mix_ffn_kernel, alpha=alpha),
        grid=(bsz * per_b, dff // tf),
        in_specs=[
            pl.BlockSpec((tm, S5_WIDTH), lambda i, j: (i % per_b, i // per_b)),
            tok(GLA_WIDTH),
            tok(d),
            pl.BlockSpec((1, N_MOD, d), lambda i, j: (i // per_b, 0, 0)),
            const((d, d)), const((1, d)), const((1, d)),
            pl.BlockSpec((d, tf), lambda i, j: (0, j)),
            pl.BlockSpec((d, tf), lambda i, j: (0, j)),
            pl.BlockSpec((tf, d), lambda i, j: (j, 0)),
            const((1, d)), const((1, d)),
        ],
        out_specs=tok(d),
        out_shape=jax.ShapeDtypeStruct((bsz, seq, d), F32),
        scratch_shapes=[pltpu.VMEM((tm, d), F32), pltpu.VMEM((tm, d), F32), pltpu.VMEM((tm, d), BF16)],
        compiler_params=_cparams("parallel", "arbitrary"),
        name="mix_dense_ffn",
    )(y_s5_tm, y_gla, x, mod_l, w_out.astype(BF16), row(ln1_g), row(ln1_b),
      w_gate.astype(BF16), w_up.astype(BF16), w_down.astype(BF16), row(ln2_g), row(ln2_b))


def _route_tables(route, count, rb, nblk):
    e = route[:, RT_E1:RT_E2 + 1].astype(jnp.int32)
    rank = route[:, RT_R1:RT_R2 + 1].astype(jnp.int32)
    cnt = count[0, :N_EXPERTS].astype(jnp.int32)
    padded = ((cnt + rb - 1) // rb) * rb
    ends = jnp.cumsum(padded)
    starts = ends - padded
    pos = jnp.sum(jnp.where(e[..., None] == jnp.arange(N_EXPERTS), starts, 0), axis=-1) + rank
    n_valid = ends[-1] // rb
    blk = jnp.minimum(jnp.arange(nblk, dtype=jnp.int32), n_valid - 1)
    blk_expert = jnp.sum((blk[:, None] * rb >= ends[None, :]).astype(jnp.int32), axis=-1)
    blk_expert = jnp.minimum(blk_expert, N_EXPERTS - 1)
    n_rows = jnp.full((1,), nblk * rb, jnp.int32)
    fill = jnp.stack([jnp.concatenate([starts + cnt, ends[-1:]]), jnp.concatenate([ends, n_rows])], axis=1)
    return (pos.astype(jnp.int32), blk_expert.astype(jnp.int32), n_valid.astype(jnp.int32).reshape(1),
            fill.reshape(-1).astype(jnp.int32))


SUBLANES = 8


def _staged_row(ref, tile, sub):
    return ref.at[tile, pl.ds(sub, 1), :]


def _dispatch_kernel(fill_ref, pos_ref, x_ref, mod_ref, xs_hbm, hs_ref, sem, *, tb):
    i = pl.program_id(0)
    h = x_ref[...] * (1.0 + mod_ref[0, 4:5, :]) + mod_ref[0, 3:4, :]
    hs_ref[...] = h.reshape(tb // SUBLANES, SUBLANES, h.shape[-1])

    def issue(r8, carry):
        for sub in range(SUBLANES):
            for kk in range(2):
                dst_row = pos_ref[0, 0, 2 * SUBLANES * r8 + 2 * sub + kk]
                pltpu.make_async_copy(_staged_row(hs_ref, r8, sub), xs_hbm.at[pl.ds(dst_row, 1), :],
                                      sem.at[0]).start()
        return carry

    lax.fori_loop(0, tb // SUBLANES, issue, 0)

    @pl.when(i == 0)
    def _():
        for e in range(N_EXPERTS + 1):
            lo = fill_ref[2 * e]
            hi = fill_ref[2 * e + 1]

            def fill(p, carry):
                pltpu.make_async_copy(_staged_row(hs_ref, 0, 0), xs_hbm.at[pl.ds(p, 1), :], sem.at[1]).start()
                return carry

            def drain(p, carry):
                pltpu.make_async_copy(_staged_row(hs_ref, 0, 0), xs_hbm.at[pl.ds(p, 1), :], sem.at[1]).wait()
                return carry

            lax.fori_loop(lo, hi, fill, 0)
            lax.fori_loop(lo, hi, drain, 0)

    for _ in range(2):
        pltpu.make_async_copy(x_ref, xs_hbm.at[pl.ds(0, tb), :], sem.at[0]).wait()


def _dispatch(x1, mod_l, pos, fill, n_rows, tb, seq):
    n, d = x1.shape
    per_b = seq // tb
    grid_spec = pltpu.PrefetchScalarGridSpec(
        num_scalar_prefetch=1,
        grid=(n // tb,),
        in_specs=[
            pl.BlockSpec((1, 1, 2 * tb), lambda i, f: (i, 0, 0), memory_space=pltpu.SMEM),
            pl.BlockSpec((tb, d), lambda i, f: (i, 0)),
            pl.BlockSpec((1, N_MOD, d), lambda i, f: (i // per_b, 0, 0)),
        ],
        out_specs=pl.BlockSpec(memory_space=pl.ANY),
        scratch_shapes=[pltpu.VMEM((tb // SUBLANES, SUBLANES, d), F32), pltpu.SemaphoreType.DMA((2,))],
    )
    return pl.pallas_call(
        functools.partial(_dispatch_kernel, tb=tb),
        grid_spec=grid_spec,
        out_shape=jax.ShapeDtypeStruct((n_rows, d), F32),
        compiler_params=_cparams("arbitrary"),
        name="moe_dispatch",
    )(fill, pos.reshape(n // tb, 1, 2 * tb), x1, mod_l)


def _expert_kernel(be_ref, nv_ref, xs_ref, wg_ref, wu_ref, wd_ref, y_ref, acc_ref, xb_ref):
    del be_ref
    i = pl.program_id(0)
    j = pl.program_id(1)

    @pl.when(i < nv_ref[0])
    def _():
        @pl.when(j == 0)
        def _():
            acc_ref[...] = jnp.zeros_like(acc_ref)
            xb_ref[...] = xs_ref[...].astype(BF16)

        h = xb_ref[...]
        a = _dot(h, wg_ref[0])
        mid = (_silu(a) * _dot(h, wu_ref[0])).astype(BF16)
        acc_ref[...] += _dot(mid, wd_ref[0])

        @pl.when(j == pl.num_programs(1) - 1)
        def _():
            y_ref[...] = acc_ref[...]

    @pl.when((i >= nv_ref[0]) & (j == 0))
    def _():
        y_ref[...] = jnp.zeros_like(y_ref)


def _experts(xs, blk_expert, n_valid, w_gate, w_up, w_down, rb, tf):
    n_rows, d = xs.shape
    dff = w_gate.shape[2]
    nj = dff // tf
    row_blk = lambda i, j, be, nv: (jnp.minimum(i, nv[0] - 1), 0)
    jj = lambda i, j, nv: jnp.where(i < nv[0], j, nj - 1)
    grid_spec = pltpu.PrefetchScalarGridSpec(
        num_scalar_prefetch=2,
        grid=(n_rows // rb, nj),
        in_specs=[
            pl.BlockSpec((rb, d), row_blk),
            pl.BlockSpec((1, d, tf), lambda i, j, be, nv: (be[i], 0, jj(i, j, nv))),
            pl.BlockSpec((1, d, tf), lambda i, j, be, nv: (be[i], 0, jj(i, j, nv))),
            pl.BlockSpec((1, tf, d), lambda i, j, be, nv: (be[i], jj(i, j, nv), 0)),
        ],
        out_specs=pl.BlockSpec((rb, d), lambda i, j, be, nv: (i, 0)),
        scratch_shapes=[pltpu.VMEM((rb, d), F32), pltpu.VMEM((rb, d), BF16)],
    )
    return pl.pallas_call(
        _expert_kernel,
        grid_spec=grid_spec,
        out_shape=jax.ShapeDtypeStruct((n_rows, d), F32),
        compiler_params=_cparams("arbitrary", "arbitrary"),
        name="moe_experts",
    )(blk_expert, n_valid, xs, w_gate.astype(BF16), w_up.astype(BF16), w_down.astype(BF16))


def _combine_kernel(pos_ref, posn_ref, route_ref, x_ref, mod_ref, lg_ref, lb_ref, y_hbm, o_ref, buf, sem,
                    *, tb, alpha):
    i = pl.program_id(0)
    slot = i % 2
    d = o_ref.shape[-1]
    ntile = tb // SUBLANES

    def issue_tile(p_ref, s, r8):
        for sub in range(SUBLANES):
            for kk in range(2):
                src_row = p_ref[0, 0, 2 * SUBLANES * r8 + 2 * sub + kk]
                pltpu.make_async_copy(y_hbm.at[pl.ds(src_row, 1), :], _staged_row(buf.at[s, kk], r8, sub),
                                      sem.at[s]).start()

    def wait_slot(s):
        for _ in range(2):
            pltpu.make_async_copy(y_hbm.at[pl.ds(0, tb), :], o_ref, sem.at[s]).wait()

    def issue_all(p_ref, s):
        def body(r8, carry):
            issue_tile(p_ref, s, r8)
            return carry
        lax.fori_loop(0, ntile, body, 0)

    @pl.when(i == 0)
    def _():
        issue_all(pos_ref, 0)

    issue_all(posn_ref, 1 - slot)
    wait_slot(slot)
    rec = route_ref[...]
    y0 = buf[slot, 0].reshape(tb, d)
    y1 = buf[slot, 1].reshape(tb, d)
    f = rec[:, RT_W1:RT_W1 + 1] * y0 + rec[:, RT_W2:RT_W2 + 1] * y1
    gt = mod_ref[0, 5:6, :]
    o_ref[...] = _layer_norm(alpha * x_ref[...] + (1.0 + gt) * f, lg_ref[...], lb_ref[...])

    @pl.when(i == pl.num_programs(0) - 1)
    def _():
        wait_slot(1 - slot)


def _combine(ys, pos, route, x1, mod_l, ln_g, ln_b, alpha, tb, seq):
    n, d = x1.shape
    nb = n // tb
    per_b = seq // tb
    pos3 = pos.reshape(nb, 1, 2 * tb)
    return pl.pallas_call(
        functools.partial(_combine_kernel, tb=tb, alpha=alpha),
        grid=(nb,),
        in_specs=[
            pl.BlockSpec((1, 1, 2 * tb), lambda i: (i, 0, 0), memory_space=pltpu.SMEM),
            pl.BlockSpec((1, 1, 2 * tb), lambda i: (jnp.minimum(i + 1, nb - 1), 0, 0), memory_space=pltpu.SMEM),
            pl.BlockSpec((tb, ROUTER_LANES), lambda i: (i, 0)),
            pl.BlockSpec((tb, d), lambda i: (i, 0)),
            pl.BlockSpec((1, N_MOD, d), lambda i: (i // per_b, 0, 0)),
            pl.BlockSpec((1, d), lambda i: (0, 0)),
            pl.BlockSpec((1, d), lambda i: (0, 0)),
            pl.BlockSpec(memory_space=pl.ANY),
        ],
        out_specs=pl.BlockSpec((tb, d), lambda i: (i, 0)),
        out_shape=jax.ShapeDtypeStruct((n, d), F32),
        scratch_shapes=[pltpu.VMEM((2, 2, tb // SUBLANES, SUBLANES, d), F32), pltpu.SemaphoreType.DMA((2,))],
        compiler_params=_cparams("arbitrary"),
        name="moe_combine",
    )(pos3, pos3, route, x1, mod_l, ln_g.reshape(1, d), ln_b.reshape(1, d), ys)


def _moe_ffn(x1, route, count, mod_l, w_gate, w_up, w_down, ln_g, ln_b, alpha, rb, tf, tb):
    bsz, seq, d = x1.shape
    n = bsz * seq
    n_rows = 2 * n + N_EXPERTS * rb
    route2 = route.reshape(n, ROUTER_LANES)
    pos, blk_expert, n_valid, fill = _route_tables(route2, count, rb, n_rows // rb)
    xs = _dispatch(x1.reshape(n, d), mod_l, pos, fill, n_rows, rb, seq)
    ys = _experts(xs, blk_expert, n_valid, w_gate, w_up, w_down, rb, tf)
    out = _combine(ys, pos, route2, x1.reshape(n, d), mod_l, ln_g, ln_b, alpha, tb, seq)
    return out.reshape(bsz, seq, d)


def _tile(n, want):
    t = min(n, want)
    assert n % t == 0, (n, want)
    return t


def kernel(x, c, mod_w, mod_b, w_in, w_out, s5_lam_re, s5_lam_im, s5_log_dt, s5_b_re, s5_b_im, s5_c_re, s5_c_im, s5_d, s5_w_glu, s5_b_glu, gla_w_alpha_up, gla_b_alpha, gla_head_gain, ln_mix_g, ln_mix_b, ffn_w_gate, ffn_w_up, ffn_w_down, moe_w_router, moe_b_router, moe_w_gate, moe_w_up, moe_w_down, ln_ffn_g, ln_ffn_b):
    bsz, seq, d = x.shape
    depth = mod_w.shape[0]
    alpha = (2.0 * depth) ** 0.25
    tm = _tile(seq, 512)
    tt = _tile(seq, 16)
    tg = _tile(seq, 512)
    tm_ffn = _tile(seq, 1024)
    tf = _tile(D_FF, 512)
    tb = _tile(seq, 256)

    mod = _modulation(c, mod_w, mod_b).reshape(depth, bsz, N_MOD, d)
    for layer in range(depth):
        mod_l = mod[layer]
        w_gate_fold = _gate_fold(w_in[layer][:, IN_P[0]:], gla_w_alpha_up[layer])
        w_cat = jnp.concatenate([w_in[layer][:, :IN_P[0]], w_gate_fold], axis=1).astype(BF16)
        u_tm, q, k, v, g_out, pre = _in_projection(x, mod_l, w_cat, tm)

        s5w = _s5_weights(s5_lam_re[layer], s5_lam_im[layer], s5_log_dt[layer], s5_b_re[layer], s5_b_im[layer],
                          s5_c_re[layer], s5_c_im[layer])
        y_s5 = _s5_group(u_tm.reshape(seq, bsz, S5_WIDTH), *s5w, s5_d[layer], s5_w_glu[layer], s5_b_glu[layer], tt)
        y_s5 = y_s5.reshape(seq, bsz * S5_WIDTH)
        y_gla = _gla_group(q, k, v, g_out, pre, gla_b_alpha[layer], gla_head_gain[layer], tg)

        i = layer // 2
        if layer % 2 == 0:
            x = _mix_dense_ffn(y_s5, y_gla, x, mod_l, w_out[layer], ln_mix_g[layer], ln_mix_b[layer],
                               ffn_w_gate[i], ffn_w_up[i], ffn_w_down[i], ln_ffn_g[layer], ln_ffn_b[layer],
                               alpha, tm_ffn, tf)
        else:
            x1, route, count = _out_projection_router(y_s5, y_gla, x, mod_l, w_out[layer], ln_mix_g[layer],
                                                      ln_mix_b[layer], moe_w_router[i], moe_b_router[i], alpha, tm)
            x = _moe_ffn(x1, route, count, mod_l, moe_w_gate[i], moe_w_up[i], moe_w_down[i],
                         ln_ffn_g[layer], ln_ffn_b[layer], alpha, tm_ffn, tf, tb)
    return x
```

```python
import functools
import math

import jax
import jax.numpy as jnp
from jax import lax
from jax.experimental import pallas as pl
from jax.experimental.pallas import tpu as pltpu

F32 = jnp.float32
BF16 = jnp.bfloat16

D_MODEL = 1024
S5_WIDTH = 512
S5_GROUP_CH = 16
S5_GROUPS = 32
S5_STATE = 64
S5_LANES = S5_GROUPS * S5_STATE
GLA_WIDTH = 512
GLA_HEADS = 4
GLA_DK = 256
GLA_HEAD_K = 64
GLA_HEAD_V = 128
GLA_GATE_RANK = 16
GLA_GATE_TAU = 16.0
GLA_CHUNK = 64
D_FF = 3584
N_EXPERTS = 8
N_MOD = 6
LN_EPS = 1e-5
RMS_EPS = 1e-6
ROUTER_LANES = 128
NEG_BIG = -1e30

VMEM_LIMIT = 56 * 1024 * 1024


def _cparams(*sem):
    return pltpu.CompilerParams(dimension_semantics=sem, vmem_limit_bytes=VMEM_LIMIT)


def _dot(a, b):
    return jnp.dot(a, b, preferred_element_type=F32)


def _split2(a):
    a1 = a.astype(BF16)
    a2 = (a - a1.astype(F32)).astype(BF16)
    return a1, a2


def _split3(a):
    a1 = a.astype(BF16)
    r1 = a - a1.astype(F32)
    a2 = r1.astype(BF16)
    a3 = (r1 - a2.astype(F32)).astype(BF16)
    return a1, a2, a3


def _dot_f32(a, b):
    a1, a2, a3 = _split3(a)
    b1, b2, b3 = _split3(b)
    lo = _dot(a1, b3) + _dot(a2, b2) + _dot(a3, b1)
    mid = _dot(a1, b2) + _dot(a2, b1)
    return lo + mid + _dot(a1, b1)


def _dot_f32x3(a, b):
    a1, a2 = _split2(a)
    b1, b2 = _split2(b)
    return (_dot(a1, b2) + _dot(a2, b1)) + _dot(a1, b1)


def _silu(x):
    return x * jax.nn.sigmoid(x)


def _layer_norm(r, gain, bias):
    mu = jnp.mean(r, axis=-1, keepdims=True)
    d = r - mu
    var = jnp.mean(d * d, axis=-1, keepdims=True)
    return d * lax.rsqrt(var + LN_EPS) * gain + bias


def _mod_kernel(c_ref, w_ref, b_ref, o_ref):
    o_ref[0] = _dot_f32(_silu(c_ref[...]), w_ref[0]) + b_ref[0]


def _modulation(c, mod_w, mod_b):
    depth, d, nd = mod_w.shape
    bsz = c.shape[0]
    return pl.pallas_call(
        _mod_kernel,
        grid=(depth, nd // d),
        in_specs=[
            pl.BlockSpec((bsz, d), lambda l, n: (0, 0)),
            pl.BlockSpec((1, d, d), lambda l, n: (l, 0, n)),
            pl.BlockSpec((1, 1, d), lambda l, n: (l, 0, n)),
        ],
        out_specs=pl.BlockSpec((1, bsz, d), lambda l, n: (l, 0, n)),
        out_shape=jax.ShapeDtypeStruct((depth, bsz, nd), F32),
        compiler_params=_cparams("parallel", "parallel"),
        name="modulation",
    )(c, mod_w, mod_b.reshape(depth, 1, nd))


def _gate_fold_kernel(wl_ref, wu_ref, o_ref):
    o_ref[...] = _dot_f32(wl_ref[...], wu_ref[...])


def _gate_fold(w_low, w_up):
    d = w_low.shape[0]
    pad = 128 - GLA_GATE_RANK
    wl = jnp.pad(w_low, ((0, 0), (0, pad)))
    wu = jnp.pad(w_up, ((0, pad), (0, 0)))
    return pl.pallas_call(
        _gate_fold_kernel,
        out_shape=jax.ShapeDtypeStruct((d, GLA_DK), F32),
        name="gate_fold",
    )(wl, wu)


def _s5_disc_kernel(lre_ref, lim_ref, ldt_ref, bre_ref, bim_ref, lam_ref, bbre_ref, bbim_ref):
    lre = lre_ref[...]
    lim = lim_ref[...]
    dt = jnp.exp(ldt_ref[...])
    mag = jnp.exp(lre * dt)
    ang = lim * dt
    are = mag * jnp.cos(ang)
    aim = mag * jnp.sin(ang)
    lam_ref[0] = are
    lam_ref[1] = aim
    den = lre * lre + lim * lim
    nre = are - 1.0
    cre = (nre * lre + aim * lim) / den
    cim = (aim * lre - nre * lim) / den
    for h in range(S5_GROUP_CH):
        bre = bre_ref[h]
        bim = bim_ref[h]
        bbre_ref[h] = cre * bre - cim * bim
        bbim_ref[h] = cre * bim + cim * bre


def _s5_discretise(lam_re, lam_im, log_dt, b_re, b_im):
    g, p = lam_re.shape
    hch = b_re.shape[-1]
    bre_t = jnp.transpose(b_re, (2, 0, 1))
    bim_t = jnp.transpose(b_im, (2, 0, 1))
    lam, bbre, bbim = pl.pallas_call(
        _s5_disc_kernel,
        out_shape=(
            jax.ShapeDtypeStruct((2, g, p), F32),
            jax.ShapeDtypeStruct((hch, g, p), F32),
            jax.ShapeDtypeStruct((hch, g, p), F32),
        ),
        name="s5_discretise",
    )(lam_re, lam_im, log_dt.reshape(g, 1), bre_t, bim_t)
    return lam, bbre, bbim


def _s5_weights(lam_re, lam_im, log_dt, b_re, b_im, c_re, c_im):
    g, p, hch = S5_GROUPS, S5_STATE, S5_GROUP_CH
    lam, bbre, bbim = _s5_discretise(lam_re, lam_im, log_dt, b_re, b_im)
    lam_rows = lam.reshape(2, g * p)
    eye = jnp.eye(g, dtype=F32)

    def in_blocks(bb):
        dense = jnp.einsum('hgp,gk->ghkp', bb, eye).reshape(g * hch, g * p)
        blocks = [dense[128 * (n // 2):128 * (n // 2) + 128, 256 * n:256 * (n + 1)] for n in range(8)]
        return jnp.stack(blocks).astype(BF16)

    def out_blocks(cc, sign):
        dense = jnp.einsum('ghp,gk->gpkh', cc, eye).reshape(g * p, g * hch) * sign
        blocks = [dense[1024 * m:1024 * (m + 1), 256 * m:256 * (m + 1)] for m in range(2)]
        return jnp.stack(blocks).astype(BF16)

    return lam_rows, in_blocks(bbre), in_blocks(bbim), out_blocks(c_re, 1.0), out_blocks(c_im, -1.0)


IN_U = (0, 512)
IN_Q = (512, 768)
IN_K = (768, 1024)
IN_V = (1024, 1536)
IN_G = (1536, 2048)
IN_P = (2048, 2304)


def _inproj_kernel(x_ref, mod_ref, w_ref, u_ref, q_ref, k_ref, v_ref, g_ref, p_ref):
    sh = mod_ref[0, 0:1, :]
    sc = mod_ref[0, 1:2, :]
    h = (x_ref[0] * (1.0 + sc) + sh).astype(BF16)
    u_ref[...] = _dot(h, w_ref[:, IN_U[0]:IN_U[1]]).astype(BF16)
    q_ref[0] = _dot(h, w_ref[:, IN_Q[0]:IN_Q[1]]).astype(BF16)
    k_ref[0] = _dot(h, w_ref[:, IN_K[0]:IN_K[1]]).astype(BF16)
    v_ref[0] = _dot(h, w_ref[:, IN_V[0]:IN_V[1]]).astype(BF16)
    g_ref[0] = _dot(h, w_ref[:, IN_G[0]:IN_G[1]]).astype(BF16)
    p_ref[0] = _dot(h, w_ref[:, IN_P[0]:IN_P[1]])


def _in_projection(x, mod_l, w_cat, tm):
    bsz, seq, d = x.shape
    ncol = w_cat.shape[1]
    bspec = lambda w: pl.BlockSpec((1, tm, w), lambda b, t: (b, t, 0))
    return pl.pallas_call(
        _inproj_kernel,
        grid=(bsz, seq // tm),
        in_specs=[
            pl.BlockSpec((1, tm, d), lambda b, t: (b, t, 0)),
            pl.BlockSpec((1, N_MOD, d), lambda b, t: (b, 0, 0)),
            pl.BlockSpec((d, ncol), lambda b, t: (0, 0)),
        ],
        out_specs=[
            pl.BlockSpec((tm, S5_WIDTH), lambda b, t: (t, b)),
            bspec(GLA_DK), bspec(GLA_DK), bspec(GLA_WIDTH), bspec(GLA_WIDTH), bspec(GLA_DK),
        ],
        out_shape=[
            jax.ShapeDtypeStruct((seq, bsz * S5_WIDTH), BF16),
            jax.ShapeDtypeStruct((bsz, seq, GLA_DK), BF16),
            jax.ShapeDtypeStruct((bsz, seq, GLA_DK), BF16),
            jax.ShapeDtypeStruct((bsz, seq, GLA_WIDTH), BF16),
            jax.ShapeDtypeStruct((bsz, seq, GLA_WIDTH), BF16),
            jax.ShapeDtypeStruct((bsz, seq, GLA_DK), F32),
        ],
        compiler_params=_cparams("parallel", "parallel"),
        name="in_projection",
    )(x, mod_l, w_cat)


S5_SCAN_LANES = 256


def _s5_kernel(u_ref, lam_ref, wbre_ref, wbim_ref, wcre_ref, wcim_ref, d_ref, wglu_ref, bglu_ref,
               o_ref, hre, him, st_re, st_im, *, tt, nb):
    @pl.when(pl.program_id(0) == 0)
    def _():
        st_re[...] = jnp.zeros_like(st_re)
        st_im[...] = jnp.zeros_like(st_im)

    rows = tt * nb
    ub = u_ref[...].reshape(rows, S5_WIDTH)
    for n in range(8):
        lhs = ub[:, 128 * (n // 2):128 * (n // 2) + 128]
        hre[:, 256 * n:256 * (n + 1)] = _dot(lhs, wbre_ref[n])
        him[:, 256 * n:256 * (n + 1)] = _dot(lhs, wbim_ref[n])

    for c in range(S5_LANES // S5_SCAN_LANES):
        ls = slice(c * S5_SCAN_LANES, (c + 1) * S5_SCAN_LANES)
        lr = jnp.broadcast_to(lam_ref[0:1, ls], (nb, S5_SCAN_LANES))
        li = jnp.broadcast_to(lam_ref[1:2, ls], (nb, S5_SCAN_LANES))
        hr = st_re[:, ls]
        hi = st_im[:, ls]
        for t in range(tt):
            rs = slice(t * nb, (t + 1) * nb)
            nr = lr * hr - li * hi + hre[rs, ls]
            ni = lr * hi + li * hr + him[rs, ls]
            hre[rs, ls] = nr
            him[rs, ls] = ni
            hr, hi = nr, ni
        st_re[:, ls] = hr
        st_im[:, ls] = hi

    ys = []
    for m in range(2):
        ks = slice(1024 * m, 1024 * (m + 1))
        ys.append(_dot(hre[:, ks].astype(BF16), wcre_ref[m]) + _dot(him[:, ks].astype(BF16), wcim_ref[m]))
    y = jnp.concatenate(ys, axis=-1) + d_ref[...] * ub.astype(F32)
    y = 0.5 * y * (1.0 + jnp.tanh(math.sqrt(2.0 / math.pi) * (y + 0.044715 * (y * y * y))))
    z = y * jax.nn.sigmoid(_dot(y.astype(BF16), wglu_ref[...]) + bglu_ref[...])
    o_ref[...] = z.astype(BF16).reshape(tt, nb, S5_WIDTH)


def _s5_group(u_tm, lam_rows, wbre, wbim, wcre, wcim, d_skip, w_glu, b_glu, tt):
    seq, nb, width = u_tm.shape
    rows = tt * nb
    full = lambda a: pl.BlockSpec(a.shape, lambda t: (0,) * a.ndim)
    d2 = d_skip.reshape(1, width)
    b2 = b_glu.reshape(1, width)
    wg = w_glu.astype(BF16)
    return pl.pallas_call(
        functools.partial(_s5_kernel, tt=tt, nb=nb),
        grid=(seq // tt,),
        in_specs=[
            pl.BlockSpec((tt, nb, width), lambda t: (t, 0, 0)),
            full(lam_rows), full(wbre), full(wbim), full(wcre), full(wcim), full(d2), full(wg), full(b2),
        ],
        out_specs=pl.BlockSpec((tt, nb, width), lambda t: (t, 0, 0)),
        out_shape=jax.ShapeDtypeStruct((seq, nb, width), BF16),
        scratch_shapes=[
            pltpu.VMEM((rows, S5_LANES), F32),
            pltpu.VMEM((rows, S5_LANES), F32),
            pltpu.VMEM((nb, S5_LANES), F32),
            pltpu.VMEM((nb, S5_LANES), F32),
        ],
        compiler_params=_cparams("arbitrary"),
        name="s5_group",
    )(u_tm, lam_rows, wbre, wbim, wcre, wcim, d2, wg, b2)


def _gla_kernel(q_ref, k_ref, v_ref, g_ref, p_ref, ba_ref, gain_ref, o_ref, s_ref, oi_ref, upd_ref, *, nchunk):
    @pl.when(pl.program_id(1) == 0)
    def _():
        s_ref[...] = jnp.zeros_like(s_ref)

    c = GLA_CHUNK
    nh = GLA_HEADS
    srow = lax.broadcasted_iota(jnp.int32, (nh * c, c), 0)
    scol = lax.broadcasted_iota(jnp.int32, (nh * c, c), 1)
    causal = (srow & (c - 1)) >= scol
    lane = lax.broadcasted_iota(jnp.int32, (1, GLA_DK), 1)
    head_lanes = [(lane >= GLA_HEAD_K * h) & (lane < GLA_HEAD_K * (h + 1)) for h in range(nh)]
    vrow = lax.broadcasted_iota(jnp.int32, (2 * GLA_HEAD_V, 2 * GLA_HEAD_K), 0)
    kcol = lax.broadcasted_iota(jnp.int32, (2 * GLA_HEAD_V, 2 * GLA_HEAD_K), 1)
    same_head = (vrow < GLA_HEAD_V) == (kcol < GLA_HEAD_K)
    pair_k = [slice(2 * GLA_HEAD_K * pr, 2 * GLA_HEAD_K * (pr + 1)) for pr in range(nh // 2)]
    pair_v = [slice(2 * GLA_HEAD_V * pr, 2 * GLA_HEAD_V * (pr + 1)) for pr in range(nh // 2)]
    scale = GLA_HEAD_K ** -0.5
    mid = c // 2 - 1

    t = nchunk * c
    slab = math.gcd(t, 4 * c)
    trow = lax.broadcasted_iota(jnp.int32, (slab, slab), 0)
    tcol = lax.broadcasted_iota(jnp.int32, (slab, slab), 1)
    chunk_tri = jnp.where((tcol <= trow) & (tcol >= (trow & -c)), 1.0, 0.0).astype(BF16)
    pre = p_ref[0] + ba_ref[...]
    log_alpha = (jnp.minimum(pre, 0.0) - jnp.log(1.0 + jnp.exp(-jnp.abs(pre)))) * (1.0 / GLA_GATE_TAU)
    a1, a2, a3 = _split3(log_alpha)
    b = jnp.concatenate(
        [_dot(chunk_tri, a3[r0:r0 + slab]) + _dot(chunk_tri, a2[r0:r0 + slab]) + _dot(chunk_tri, a1[r0:r0 + slab])
         for r0 in range(0, t, slab)], axis=0).reshape(nchunk, c, GLA_DK)
    b_mid = b[:, mid:mid + 1, :]
    b_last = b[:, c - 1:c, :]
    q = (q_ref[0].astype(F32) * scale).reshape(nchunk, c, GLA_DK)
    k = k_ref[0].astype(F32).reshape(nchunk, c, GLA_DK)
    q_in = (q * jnp.exp(b)).astype(BF16)
    q_e = q * jnp.exp(b - b_mid)
    k_e = (k * jnp.exp(b_mid - b)).astype(BF16)
    k_d = (k * jnp.exp(b_last - b)).astype(BF16)
    decay = jnp.exp(b_last)
    for ci in range(nchunk):
        sl = slice(ci * c, (ci + 1) * c)
        q_stack = jnp.concatenate([jnp.where(head_lanes[h], q_e[ci], 0.0) for h in range(nh)], axis=0).astype(BF16)
        scores = lax.dot_general(q_stack, k_e[ci], (((1,), (1,)), ((), ())),
                                 preferred_element_type=F32)
        scores = jnp.where(causal, scores, 0.0).astype(BF16)
        v_all = v_ref[0, sl, :]
        pv = _dot(scores, v_all)
        for h in range(nh):
            os_ = slice(GLA_HEAD_V * h, GLA_HEAD_V * (h + 1))
            oi_ref[sl, os_] = pv[c * h:c * (h + 1), os_]
        for pr in range(nh // 2):
            upd = lax.dot_general(v_all[:, pair_v[pr]], k_d[ci][:, pair_k[pr]], (((0,), (0,)), ((), ())),
                                  preferred_element_type=F32)
            upd_ref[ci, pr] = jnp.where(same_head, upd, 0.0)

    for pr in range(nh // 2):
        s_t = s_ref[pr]
        for ci in range(nchunk):
            sl = slice(ci * c, (ci + 1) * c)
            oi_ref[sl, pair_v[pr]] += lax.dot_general(q_in[ci][:, pair_k[pr]], s_t.astype(BF16),
                                                      (((1,), (1,)), ((), ())), preferred_element_type=F32)
            s_t = decay[ci][:, pair_k[pr]] * s_t + upd_ref[ci, pr]
        s_ref[pr] = s_t

    for h in range(nh):
        os_ = slice(GLA_HEAD_V * h, GLA_HEAD_V * (h + 1))
        o_h = oi_ref[:, os_]
        ms = jnp.mean(o_h * o_h, axis=-1, keepdims=True)
        o_n = o_h * lax.rsqrt(ms + RMS_EPS) * gain_ref[:, os_]
        o_ref[0, :, os_] = (o_n * _silu(g_ref[0, :, os_].astype(F32))).astype(BF16)


def _gla_group(q, k, v, g_out, pre, b_alpha, head_gain, tg):
    bsz, seq, _ = q.shape
    blk = lambda w: pl.BlockSpec((1, tg, w), lambda b, t: (b, t, 0))
    ba = b_alpha.reshape(1, GLA_DK)
    gain = head_gain.reshape(1, GLA_WIDTH)
    return pl.pallas_call(
        functools.partial(_gla_kernel, nchunk=tg // GLA_CHUNK),
        grid=(bsz, seq // tg),
        in_specs=[
            blk(GLA_DK), blk(GLA_DK), blk(GLA_WIDTH), blk(GLA_WIDTH), blk(GLA_DK),
            pl.BlockSpec((1, GLA_DK), lambda b, t: (0, 0)),
            pl.BlockSpec((1, GLA_WIDTH), lambda b, t: (0, 0)),
        ],
        out_specs=blk(GLA_WIDTH),
        out_shape=jax.ShapeDtypeStruct((bsz, seq, GLA_WIDTH), BF16),
        scratch_shapes=[
            pltpu.VMEM((GLA_HEADS // 2, 2 * GLA_HEAD_V, 2 * GLA_HEAD_K), F32),
            pltpu.VMEM((tg, GLA_WIDTH), F32),
            pltpu.VMEM((tg // GLA_CHUNK, GLA_HEADS // 2, 2 * GLA_HEAD_V, 2 * GLA_HEAD_K), F32),
        ],
        compiler_params=_cparams("parallel", "arbitrary"),
        name="gla_group",
    )(q, k, v, g_out, pre, ba, gain)


def _mix_out(ys_ref, yg_ref, x_ref, mod_ref, w_ref, lg_ref, lb_ref, alpha):
    y = _dot(ys_ref[...], w_ref[0:S5_WIDTH, :]) + _dot(yg_ref[0], w_ref[S5_WIDTH:, :])
    gt = mod_ref[0, 2:3, :]
    x1 = _layer_norm(alpha * x_ref[0] + (1.0 + gt) * y, lg_ref[...], lb_ref[...])
    h2 = x1 * (1.0 + mod_ref[0, 4:5, :]) + mod_ref[0, 3:4, :]
    return x1, h2


RT_E1, RT_E2, RT_R1, RT_R2, RT_W1, RT_W2 = range(6)


def _top2_route(logits, count, earlier):
    lane =lax.broadcasted_iota(jnp.int32, logits.shape, 1).astype(F32)
    l0 = jnp.where(lane < N_EXPERTS, logits, NEG_BIG)
    m1 = jnp.max(l0, axis=-1, keepdims=True)
    i1 = jnp.min(jnp.where(l0 == m1, lane, float(ROUTER_LANES)), axis=-1, keepdims=True)
    sel1 = lane == i1
    l1 = jnp.where(sel1, NEG_BIG, l0)
    m2 = jnp.max(l1, axis=-1, keepdims=True)
    i2 = jnp.min(jnp.where(l1 == m2, lane, float(ROUTER_LANES)), axis=-1, keepdims=True)
    sel2 = lane == i2
    e2 = jnp.exp(m2 - m1)
    w1 = 1.0 / (1.0 + e2)
    w2 = e2 / (1.0 + e2)
    chosen = jnp.where(sel1, 1.0, 0.0) + jnp.where(sel2, 1.0, 0.0)
    before = _dot(earlier, chosen.astype(BF16)) + count
    r1 = jnp.sum(jnp.where(sel1, before, 0.0), axis=-1, keepdims=True)
    r2 = jnp.sum(jnp.where(sel2, before, 0.0), axis=-1, keepdims=True)
    rec = jnp.zeros_like(logits)
    for ln, val in ((RT_E1, i1), (RT_E2, i2), (RT_R1, r1), (RT_R2, r2), (RT_W1, w1), (RT_W2, w2)):
        rec = jnp.where(lane == float(ln), val, rec)
    return rec, count + jnp.sum(chosen, axis=0, keepdims=True)


def _outproj_router_kernel(ys_ref, yg_ref, x_ref, mod_ref, w_ref, lg_ref, lb_ref, wr_ref, br_ref, earlier_ref,
                           x1_ref, route_ref, count_ref, cnt, *, alpha):
    @pl.when((pl.program_id(0) == 0) & (pl.program_id(1) == 0))
    def _():
        cnt[...] = jnp.zeros_like(cnt)

    x1, h2 = _mix_out(ys_ref, yg_ref, x_ref, mod_ref, w_ref, lg_ref, lb_ref, alpha)
    x1_ref[0] = x1
    logits = _dot_f32x3(h2, wr_ref[...]) + br_ref[...]
    rec, new_count = _top2_route(logits, cnt[...], earlier_ref[...])
    route_ref[0] = rec
    cnt[...] = new_count
    count_ref[...] = jnp.broadcast_to(new_count, count_ref.shape)


def _out_projection_router(y_s5_tm, y_gla, x, mod_l, w_out, ln_g, ln_b, w_router, b_router, alpha, tm):
    bsz, seq, d = x.shape
    row = lambda a: a.reshape(1, -1)
    const = lambda shape: pl.BlockSpec(shape, lambda b, t: (0,) * len(shape))
    pad = ROUTER_LANES - N_EXPERTS
    return pl.pallas_call(
        functools.partial(_outproj_router_kernel, alpha=alpha),
        grid=(bsz, seq // tm),
        in_specs=[
            pl.BlockSpec((tm, S5_WIDTH), lambda b, t: (t, b)),
            pl.BlockSpec((1, tm, GLA_WIDTH), lambda b, t: (b, t, 0)),
            pl.BlockSpec((1, tm, d), lambda b, t: (b, t, 0)),
            pl.BlockSpec((1, N_MOD, d), lambda b, t: (b, 0, 0)),
            const((d, d)), const((1, d)), const((1, d)),
            const((d, ROUTER_LANES)), const((1, ROUTER_LANES)), const((tm, tm)),
        ],
        out_specs=[
            pl.BlockSpec((1, tm, d), lambda b, t: (b, t, 0)),
            pl.BlockSpec((1, tm, ROUTER_LANES), lambda b, t: (b, t, 0)),
            const((8, ROUTER_LANES)),
        ],
        out_shape=[
            jax.ShapeDtypeStruct((bsz, seq, d), F32),
            jax.ShapeDtypeStruct((bsz, seq, ROUTER_LANES), F32),
            jax.ShapeDtypeStruct((8, ROUTER_LANES), F32),
        ],
        scratch_shapes=[pltpu.VMEM((1, ROUTER_LANES), F32)],
        compiler_params=_cparams("arbitrary", "arbitrary"),
        name="out_projection",
    )(y_s5_tm, y_gla, x, mod_l, w_out.astype(BF16), row(ln_g), row(ln_b),
      jnp.pad(w_router, ((0, 0), (0, pad))), jnp.pad(row(b_router), ((0, 0), (0, pad))),
      jnp.tri(tm, k=-1, dtype=BF16))


def _mix_ffn_kernel(ys_ref, yg_ref, x_ref, mod_ref, wo_ref, lg1_ref, lb1_ref, wg_ref, wu_ref, wd_ref,
                    lg2_ref, lb2_ref, o_ref, acc_ref, x1_ref, hb_ref, *, alpha):
    j = pl.program_id(1)

    @pl.when(j == 0)
    def _():
        x1, h2 = _mix_out(ys_ref, yg_ref, x_ref, mod_ref, wo_ref, lg1_ref, lb1_ref, alpha)
        x1_ref[...] = x1
        hb_ref[...] = h2.astype(BF16)
        acc_ref[...] = jnp.zeros_like(acc_ref)

    h = hb_ref[...]
    a = _dot(h, wg_ref[...])
    mid = (_silu(a) * _dot(h, wu_ref[...])).astype(BF16)
    acc_ref[...] += _dot(mid, wd_ref[...])

    @pl.when(j == pl.num_programs(1) - 1)
    def _():
        gt = mod_ref[0, 5:6, :]
        o_ref[0] = _layer_norm(alpha * x1_ref[...] + (1.0 + gt) * acc_ref[...], lg2_ref[...], lb2_ref[...])


def _mix_dense_ffn(y_s5_tm, y_gla, x, mod_l, w_out, ln1_g, ln1_b, w_gate, w_up, w_down, ln2_g, ln2_b,
                   alpha, tm, tf):
    bsz, seq, d = x.shape
    dff = w_gate.shape[1]
    per_b = seq // tm
    row = lambda a: a.reshape(1, d)
    const = lambda shape: pl.BlockSpec(shape, lambda i, j: (0,) * len(shape))
    tok = lambda w: pl.BlockSpec((1, tm, w), lambda i, j: (i // per_b, i % per_b, 0))
    return pl.pallas_call(
        functools.partial(_mix_ffn_kernel, alpha=alpha),
        grid=(bsz * per_b, dff // tf),
        in_specs=[
            pl.BlockSpec((tm, S5_WIDTH), lambda i, j: (i % per_b, i // per_b)),
            tok(GLA_WIDTH),
            tok(d),
            pl.BlockSpec((1, N_MOD, d), lambda i, j: (i // per_b, 0, 0)),
            const((d, d)), const((1, d)), const((1, d)),
            pl.BlockSpec((d, tf), lambda i, j: (0, j)),
            pl.BlockSpec((d, tf), lambda i, j: (0, j)),
            pl.BlockSpec((tf, d), lambda i, j: (j, 0)),
            const((1, d)), const((1, d)),
        ],
        out_specs=tok(d),
        out_shape=jax.ShapeDtypeStruct((bsz, seq, d), F32),
        scratch_shapes=[pltpu.VMEM((tm, d), F32), pltpu.VMEM((tm, d), F32), pltpu.VMEM((tm, d), BF16)],
        compiler_params=_cparams("parallel", "arbitrary"),
        name="mix_dense_ffn",
    )(y_s5_tm, y_gla, x, mod_l, w_out.astype(BF16), row(ln1_g), row(ln1_b),
      w_gate.astype(BF16), w_up.astype(BF16), w_down.astype(BF16), row(ln2_g), row(ln2_b))


def _route_tables(route, count, rb, nblk):
    e = route[:, RT_E1:RT_E2 + 1].astype(jnp.int32)
    rank = route[:, RT_R1:RT_R2 + 1].astype(jnp.int32)
    cnt = count[0, :N_EXPERTS].astype(jnp.int32)
    padded = ((cnt + rb - 1) // rb) * rb
    ends = jnp.cumsum(padded)
    starts = ends - padded
    pos = jnp.sum(jnp.where(e[..., None] == jnp.arange(N_EXPERTS), starts, 0), axis=-1) + rank
    n_valid = ends[-1] // rb
    blk = jnp.minimum(jnp.arange(nblk, dtype=jnp.int32), n_valid - 1)
    blk_expert = jnp.sum((blk[:, None] * rb >= ends[None, :]).astype(jnp.int32), axis=-1)
    blk_expert = jnp.minimum(blk_expert, N_EXPERTS - 1)
    n_rows = jnp.full((1,), nblk * rb, jnp.int32)
    fill = jnp.stack([jnp.concatenate([starts + cnt, ends[-1:]]), jnp.concatenate([ends, n_rows])], axis=1)
    return (pos.astype(jnp.int32), blk_expert.astype(jnp.int32), n_valid.astype(jnp.int32).reshape(1),
            fill.reshape(-1).astype(jnp.int32))


SUBLANES = 8


def _staged_row(ref, tile, sub):
    return ref.at[tile, pl.ds(sub, 1), :]


def _dispatch_kernel(fill_ref, pos_ref, x_ref, mod_ref, xs_hbm, hs_ref, sem, *, tb):
    i = pl.program_id(0)
    h = x_ref[...] * (1.0 + mod_ref[0, 4:5, :]) + mod_ref[0, 3:4, :]
    hs_ref[...] = h.reshape(tb // SUBLANES, SUBLANES, h.shape[-1])

    def issue(r8, carry):
        for sub in range(SUBLANES):
            for kk in range(2):
                dst_row = pos_ref[0, 0, 2 * SUBLANES * r8 + 2 * sub + kk]
                pltpu.make_async_copy(_staged_row(hs_ref, r8, sub), xs_hbm.at[pl.ds(dst_row, 1), :],
                                      sem.at[0]).start()
        return carry

    lax.fori_loop(0, tb // SUBLANES, issue, 0)

    @pl.when(i == 0)
    def _():
        for e in range(N_EXPERTS + 1):
            lo = fill_ref[2 * e]
            hi = fill_ref[2 * e + 1]

            def fill(p, carry):
                pltpu.make_async_copy(_staged_row(hs_ref, 0, 0), xs_hbm.at[pl.ds(p, 1), :], sem.at[1]).start()
                return carry

            def drain(p, carry):
                pltpu.make_async_copy(_staged_row(hs_ref, 0, 0), xs_hbm.at[pl.ds(p, 1), :], sem.at[1]).wait()
                return carry

            lax.fori_loop(lo, hi, fill, 0)
            lax.fori_loop(lo, hi, drain, 0)

    for _ in range(2):
        pltpu.make_async_copy(x_ref, xs_hbm.at[pl.ds(0, tb), :], sem.at[0]).wait()


def _dispatch(x1, mod_l, pos, fill, n_rows, tb, seq):
    n, d = x1.shape
    per_b = seq // tb
    grid_spec = pltpu.PrefetchScalarGridSpec(
        num_scalar_prefetch=1,
        grid=(n // tb,),
        in_specs=[
            pl.BlockSpec((1, 1, 2 * tb), lambda i, f: (i, 0, 0), memory_space=pltpu.SMEM),
            pl.BlockSpec((tb, d), lambda i, f: (i, 0)),
            pl.BlockSpec((1, N_MOD, d), lambda i, f: (i // per_b, 0, 0)),
        ],
        out_specs=pl.BlockSpec(memory_space=pl.ANY),
        scratch_shapes=[pltpu.VMEM((tb // SUBLANES, SUBLANES, d), F32), pltpu.SemaphoreType.DMA((2,))],
    )
    return pl.pallas_call(
        functools.partial(_dispatch_kernel, tb=tb),
        grid_spec=grid_spec,
        out_shape=jax.ShapeDtypeStruct((n_rows, d), F32),
        compiler_params=_cparams("arbitrary"),
        name="moe_dispatch",
    )(fill, pos.reshape(n // tb, 1, 2 * tb), x1, mod_l)


def _expert_kernel(be_ref, nv_ref, xs_ref, wg_ref, wu_ref, wd_ref, y_ref, acc_ref, xb_ref):
    del be_ref
    i = pl.program_id(0)
    j = pl.program_id(1)

    @pl.when(i < nv_ref[0])
    def _():
        @pl.when(j == 0)
        def _():
            acc_ref[...] = jnp.zeros_like(acc_ref)
            xb_ref[...] = xs_ref[...].astype(BF16)

        h = xb_ref[...]
        a = _dot(h, wg_ref[0])
        mid = (_silu(a) * _dot(h, wu_ref[0])).astype(BF16)
        acc_ref[...] += _dot(mid, wd_ref[0])

        @pl.when(j == pl.num_programs(1) - 1)
        def _():
            y_ref[...] = acc_ref[...]

    @pl.when((i >= nv_ref[0]) & (j == 0))
    def _():
        y_ref[...] = jnp.zeros_like(y_ref)


def _experts(xs, blk_expert, n_valid, w_gate, w_up, w_down, rb, tf):
    n_rows, d = xs.shape
    dff = w_gate.shape[2]
    nj = dff // tf
    row_blk = lambda i, j, be, nv: (jnp.minimum(i, nv[0] - 1), 0)
    jj = lambda i, j, nv: jnp.where(i < nv[0], j, nj - 1)
    grid_spec = pltpu.PrefetchScalarGridSpec(
        num_scalar_prefetch=2,
        grid=(n_rows // rb, nj),
        in_specs=[
            pl.BlockSpec((rb, d), row_blk),
            pl.BlockSpec((1, d, tf), lambda i, j, be, nv: (be[i], 0, jj(i, j, nv))),
            pl.BlockSpec((1, d, tf), lambda i, j, be, nv: (be[i], 0, jj(i, j, nv))),
            pl.BlockSpec((1, tf, d), lambda i, j, be, nv: (be[i], jj(i, j, nv), 0)),
        ],
        out_specs=pl.BlockSpec((rb, d), lambda i, j, be, nv: (i, 0)),
        scratch_shapes=[pltpu.VMEM((rb, d), F32), pltpu.VMEM((rb, d), BF16)],
    )
    return pl.pallas_call(
        _expert_kernel,
        grid_spec=grid_spec,
        out_shape=jax.ShapeDtypeStruct((n_rows, d), F32),
        compiler_params=_cparams("arbitrary", "arbitrary"),
        name="moe_experts",
    )(blk_expert, n_valid, xs, w_gate.astype(BF16), w_up.astype(BF16), w_down.astype(BF16))


def _combine_kernel(pos_ref, posn_ref, route_ref, x_ref, mod_ref, lg_ref, lb_ref, y_hbm, o_ref, buf, sem,
                    *, tb, alpha):
    i = pl.program_id(0)
    slot = i % 2
    d = o_ref.shape[-1]
    ntile = tb // SUBLANES

    def issue_tile(p_ref, s, r8):
        for sub in range(SUBLANES):
            for kk in range(2):
                src_row = p_ref[0, 0, 2 * SUBLANES * r8 + 2 * sub + kk]
                pltpu.make_async_copy(y_hbm.at[pl.ds(src_row, 1), :], _staged_row(buf.at[s, kk], r8, sub),
                                      sem.at[s]).start()

    def wait_slot(s):
        for _ in range(2):
            pltpu.make_async_copy(y_hbm.at[pl.ds(0, tb), :], o_ref, sem.at[s]).wait()

    def issue_all(p_ref, s):
        def body(r8, carry):
            issue_tile(p_ref, s, r8)
            return carry
        lax.fori_loop(0, ntile, body, 0)

    @pl.when(i == 0)
    def _():
        issue_all(pos_ref, 0)

    issue_all(posn_ref, 1 - slot)
    wait_slot(slot)
    rec = route_ref[...]
    y0 = buf[slot, 0].reshape(tb, d)
    y1 = buf[slot, 1].reshape(tb, d)
    f = rec[:, RT_W1:RT_W1 + 1] * y0 + rec[:, RT_W2:RT_W2 + 1] * y1
    gt = mod_ref[0, 5:6, :]
    o_ref[...] = _layer_norm(alpha * x_ref[...] + (1.0 + gt) * f, lg_ref[...], lb_ref[...])

    @pl.when(i == pl.num_programs(0) - 1)
    def _():
        wait_slot(1 - slot)


def _combine(ys, pos, route, x1, mod_l, ln_g, ln_b, alpha, tb, seq):
    n, d = x1.shape
    nb = n // tb
    per_b = seq // tb
    pos3 = pos.reshape(nb, 1, 2 * tb)
    return pl.pallas_call(
        functools.partial(_combine_kernel, tb=tb, alpha=alpha),
        grid=(nb,),
        in_specs=[
            pl.BlockSpec((1, 1, 2 * tb), lambda i: (i, 0, 0), memory_space=pltpu.SMEM),
            pl.BlockSpec((1, 1, 2 * tb), lambda i: (jnp.minimum(i + 1, nb - 1), 0, 0), memory_space=pltpu.SMEM),
            pl.BlockSpec((tb, ROUTER_LANES), lambda i: (i, 0)),
            pl.BlockSpec((tb, d), lambda i: (i, 0)),
            pl.BlockSpec((1, N_MOD, d), lambda i: (i // per_b, 0, 0)),
            pl.BlockSpec((1, d), lambda i: (0, 0)),
            pl.BlockSpec((1, d), lambda i: (0, 0)),
            pl.BlockSpec(memory_space=pl.ANY),
        ],
        out_specs=pl.BlockSpec((tb, d), lambda i: (i, 0)),
        out_shape=jax.ShapeDtypeStruct((n, d), F32),
        scratch_shapes=[pltpu.VMEM((2, 2, tb // SUBLANES, SUBLANES, d), F32), pltpu.SemaphoreType.DMA((2,))],
        compiler_params=_cparams("arbitrary"),
        name="moe_combine",
    )(pos3, pos3, route, x1, mod_l, ln_g.reshape(1, d), ln_b.reshape(1, d), ys)


def _moe_ffn(x1, route, count, mod_l, w_gate, w_up, w_down, ln_g, ln_b, alpha, rb, tf, tb):
    bsz, seq, d = x1.shape
    n = bsz * seq
    n_rows = 2 * n + N_EXPERTS * rb
    route2 = route.reshape(n, ROUTER_LANES)
    pos, blk_expert, n_valid, fill = _route_tables(route2, count, rb, n_rows // rb)
    xs = _dispatch(x1.reshape(n, d), mod_l, pos, fill, n_rows, rb, seq)
    ys = _experts(xs, blk_expert, n_valid, w_gate, w_up, w_down, rb, tf)
    out = _combine(ys, pos, route2, x1.reshape(n, d), mod_l, ln_g, ln_b, alpha, tb, seq)
    return out.reshape(bsz, seq, d)


def _tile(n, want):
    t = min(n, want)
    assert n % t == 0, (n, want)
    return t


def kernel(x, c, mod_w, mod_b, w_in, w_out, s5_lam_re, s5_lam_im, s5_log_dt, s5_b_re, s5_b_im, s5_c_re, s5_c_im, s5_d, s5_w_glu, s5_b_glu, gla_w_alpha_up, gla_b_alpha, gla_head_gain, ln_mix_g, ln_mix_b, ffn_w_gate, ffn_w_up, ffn_w_down, moe_w_router, moe_b_router, moe_w_gate, moe_w_up, moe_w_down, ln_ffn_g, ln_ffn_b):
    bsz, seq, d = x.shape
    depth = mod_w.shape[0]
    alpha = (2.0 * depth) ** 0.25
    tm = _tile(seq, 512)
    tt = _tile(seq, 32)
    tg = _tile(seq, 1024)
    tm_ffn = _tile(seq, 1024)
    tf = _tile(D_FF, 512)
    tb = _tile(seq, 256)

    mod = _modulation(c, mod_w, mod_b).reshape(depth, bsz, N_MOD, d)
    for layer in range(depth):
        mod_l = mod[layer]
        w_gate_fold = _gate_fold(w_in[layer][:, IN_P[0]:], gla_w_alpha_up[layer])
        w_cat = jnp.concatenate([w_in[layer][:, :IN_P[0]], w_gate_fold], axis=1).astype(BF16)
        u_tm, q, k, v, g_out, pre = _in_projection(x, mod_l, w_cat, tm)

        s5w = _s5_weights(s5_lam_re[layer], s5_lam_im[layer], s5_log_dt[layer], s5_b_re[layer], s5_b_im[layer],
                          s5_c_re[layer], s5_c_im[layer])
        y_s5 = _s5_group(u_tm.reshape(seq, bsz, S5_WIDTH), *s5w, s5_d[layer], s5_w_glu[layer], s5_b_glu[layer], tt)
        y_s5 = y_s5.reshape(seq, bsz * S5_WIDTH)
        y_gla = _gla_group(q, k, v, g_out, pre, gla_b_alpha[layer], gla_head_gain[layer], tg)

        i = layer // 2
        if layer % 2 == 0:
            x = _mix_dense_ffn(y_s5, y_gla, x, mod_l, w_out[layer], ln_mix_g[layer], ln_mix_b[layer],
                               ffn_w_gate[i], ffn_w_up[i], ffn_w_down[i], ln_ffn_g[layer], ln_ffn_b[layer],
                               alpha, tm_ffn, tf)
        else:
            x1, route, count = _out_projection_router(y_s5, y_gla, x, mod_l, w_out[layer], ln_mix_g[layer],
                                                      ln_mix_b[layer], moe_w_router[i], moe_b_router[i], alpha, tm)
            x = _moe_ffn(x1, route, count, mod_l, moe_w_gate[i], moe_w_up[i], moe_w_down[i],
                         ln_ffn_g[layer], ln_ffn_b[layer], alpha, tm_ffn, tf, tb)
    return x
```

```python
import functools
import math

import jax
import jax.numpy as jnp
from jax import lax
from jax.experimental import pallas as pl
from jax.experimental.pallas import tpu as pltpu

F32 = jnp.float32
BF16 = jnp.bfloat16

D_MODEL = 1024
S5_WIDTH = 512
S5_GROUP_CH = 16
S5_GROUPS = 32
S5_STATE = 64
S5_LANES = S5_GROUPS * S5_STATE
GLA_WIDTH = 512
GLA_HEADS = 4
GLA_DK = 256
GLA_HEAD_K = 64
GLA_HEAD_V = 128
GLA_GATE_RANK = 16
GLA_GATE_TAU = 16.0
GLA_CHUNK = 64
D_FF = 3584
N_EXPERTS = 8
N_MOD = 6
LN_EPS = 1e-5
RMS_EPS = 1e-6
ROUTER_LANES = 128
NEG_BIG = -1e30

VMEM_LIMIT = 56 * 1024 * 1024


def _cparams(*sem):
    return pltpu.CompilerParams(dimension_semantics=sem, vmem_limit_bytes=VMEM_LIMIT)


def _dot(a, b):
    return jnp.dot(a, b, preferred_element_type=F32)


def _split2(a):
    a1 = a.astype(BF16)
    a2 = (a - a1.astype(F32)).astype(BF16)
    return a1, a2


def _split3(a):
    a1 = a.astype(BF16)
    r1 = a - a1.astype(F32)
    a2 = r1.astype(BF16)
    a3 = (r1 - a2.astype(F32)).astype(BF16)
    return a1, a2, a3


def _dot_f32(a, b):
    a1, a2, a3 = _split3(a)
    b1, b2, b3 = _split3(b)
    lo = _dot(a1, b3) + _dot(a2, b2) + _dot(a3, b1)
    mid = _dot(a1, b2) + _dot(a2, b1)
    return lo + mid + _dot(a1, b1)


def _dot_f32x3(a, b):
    a1, a2 = _split2(a)
    b1, b2 = _split2(b)
    return (_dot(a1, b2) + _dot(a2, b1)) + _dot(a1, b1)


def _silu(x):
    return x * jax.nn.sigmoid(x)


def _layer_norm(r, gain, bias):
    mu = jnp.mean(r, axis=-1, keepdims=True)
    d = r - mu
    var = jnp.mean(d * d, axis=-1, keepdims=True)
    return d * lax.rsqrt(var + LN_EPS) * gain + bias


def _mod_kernel(c_ref, w_ref, b_ref, o_ref):
    o_ref[0] = _dot_f32(_silu(c_ref[...]), w_ref[0]) + b_ref[0]


def _modulation(c, mod_w, mod_b):
    depth, d, nd = mod_w.shape
    bsz = c.shape[0]
    return pl.pallas_call(
        _mod_kernel,
        grid=(depth, nd // d),
        in_specs=[
            pl.BlockSpec((bsz, d), lambda l, n: (0, 0)),
            pl.BlockSpec((1, d, d), lambda l, n: (l, 0, n)),
            pl.BlockSpec((1, 1, d), lambda l, n: (l, 0, n)),
        ],
        out_specs=pl.BlockSpec((1, bsz, d), lambda l, n: (l, 0, n)),
        out_shape=jax.ShapeDtypeStruct((depth, bsz, nd), F32),
        compiler_params=_cparams("parallel", "parallel"),
        name="modulation",
    )(c, mod_w, mod_b.reshape(depth, 1, nd))


def _gate_fold_kernel(wl_ref, wu_ref, o_ref):
    o_ref[...] = _dot_f32(wl_ref[...], wu_ref[...])


def _gate_fold(w_low, w_up):
    d = w_low.shape[0]
    pad = 128 - GLA_GATE_RANK
    wl = jnp.pad(w_low, ((0, 0), (0, pad)))
    wu = jnp.pad(w_up, ((0, pad), (0, 0)))
    return pl.pallas_call(
        _gate_fold_kernel,
        out_shape=jax.ShapeDtypeStruct((d, GLA_DK), F32),
        name="gate_fold",
    )(wl, wu)


def _s5_disc_kernel(lre_ref, lim_ref, ldt_ref, bre_ref, bim_ref, lam_ref, bbre_ref, bbim_ref):
    lre = lre_ref[...]
    lim = lim_ref[...]
    dt = jnp.exp(ldt_ref[...])
    mag = jnp.exp(lre * dt)
    ang = lim * dt
    are = mag * jnp.cos(ang)
    aim = mag * jnp.sin(ang)
    lam_ref[0] = are
    lam_ref[1] = aim
    den = lre * lre + lim * lim
    nre = are - 1.0
    cre = (nre * lre + aim * lim) / den
    cim = (aim * lre - nre * lim) / den
    for h in range(S5_GROUP_CH):
        bre = bre_ref[h]
        bim = bim_ref[h]
        bbre_ref[h] = cre * bre - cim * bim
        bbim_ref[h] = cre * bim + cim * bre


def _s5_discretise(lam_re, lam_im, log_dt, b_re, b_im):
    g, p = lam_re.shape
    hch = b_re.shape[-1]
    bre_t = jnp.transpose(b_re, (2, 0, 1))
    bim_t = jnp.transpose(b_im, (2, 0, 1))
    lam, bbre, bbim = pl.pallas_call(
        _s5_disc_kernel,
        out_shape=(
            jax.ShapeDtypeStruct((2, g, p), F32),
            jax.ShapeDtypeStruct((hch, g, p), F32),
            jax.ShapeDtypeStruct((hch, g, p), F32),
        ),
        name="s5_discretise",
    )(lam_re, lam_im, log_dt.reshape(g, 1), bre_t, bim_t)
    return lam, bbre, bbim


def _s5_weights(lam_re, lam_im, log_dt, b_re, b_im, c_re, c_im):
    g, p, hch = S5_GROUPS, S5_STATE, S5_GROUP_CH
    lam, bbre, bbim = _s5_discretise(lam_re, lam_im, log_dt, b_re, b_im)
    lam_rows = lam.reshape(2, g * p)
    eye = jnp.eye(g, dtype=F32)

    def in_blocks(bb):
        dense = jnp.einsum('hgp,gk->ghkp', bb, eye).reshape(g * hch, g * p)
        blocks = [dense[128 * (n // 2):128 * (n // 2) + 128, 256 * n:256 * (n + 1)] for n in range(8)]
        return jnp.stack(blocks).astype(BF16)

    def out_blocks(cc, sign):
        dense = jnp.einsum('ghp,gk->gpkh', cc, eye).reshape(g * p, g * hch) * sign
        blocks = [dense[1024 * m:1024 * (m + 1), 256 * m:256 * (m + 1)] for m in range(2)]
        return jnp.stack(blocks).astype(BF16)

    return lam_rows, in_blocks(bbre), in_blocks(bbim), out_blocks(c_re, 1.0), out_blocks(c_im, -1.0)


IN_U = (0, 512)
IN_Q = (512, 768)
IN_K = (768, 1024)
IN_V = (1024, 1536)
IN_G = (1536, 2048)
IN_P = (2048, 2304)


def _inproj_kernel(x_ref, mod_ref, w_ref, u_ref, q_ref, k_ref, v_ref, g_ref, p_ref):
    sh = mod_ref[0, 0:1, :]
    sc = mod_ref[0, 1:2, :]
    h = (x_ref[0] * (1.0 + sc) + sh).astype(BF16)
    u_ref[...] = _dot(h, w_ref[:, IN_U[0]:IN_U[1]]).astype(BF16)
    q_ref[0] = _dot(h, w_ref[:, IN_Q[0]:IN_Q[1]]).astype(BF16)
    k_ref[0] = _dot(h, w_ref[:, IN_K[0]:IN_K[1]]).astype(BF16)
    v_ref[0] = _dot(h, w_ref[:, IN_V[0]:IN_V[1]]).astype(BF16)
    g_ref[0] = _dot(h, w_ref[:, IN_G[0]:IN_G[1]]).astype(BF16)
    p_ref[0] = _dot(h, w_ref[:, IN_P[0]:IN_P[1]])


def _in_projection(x, mod_l, w_cat, tm):
    bsz, seq, d = x.shape
    ncol = w_cat.shape[1]
    bspec = lambda w: pl.BlockSpec((1, tm, w), lambda b, t: (b, t, 0))
    return pl.pallas_call(
        _inproj_kernel,
        grid=(bsz, seq // tm),
        in_specs=[
            pl.BlockSpec((1, tm, d), lambda b, t: (b, t, 0)),
            pl.BlockSpec((1, N_MOD, d), lambda b, t: (b, 0, 0)),
            pl.BlockSpec((d, ncol), lambda b, t: (0, 0)),
        ],
        out_specs=[
            pl.BlockSpec((tm, S5_WIDTH), lambda b, t: (t, b)),
            bspec(GLA_DK), bspec(GLA_DK), bspec(GLA_WIDTH), bspec(GLA_WIDTH), bspec(GLA_DK),
        ],
        out_shape=[
            jax.ShapeDtypeStruct((seq, bsz * S5_WIDTH), BF16),
            jax.ShapeDtypeStruct((bsz, seq, GLA_DK), BF16),
            jax.ShapeDtypeStruct((bsz, seq, GLA_DK), BF16),
            jax.ShapeDtypeStruct((bsz, seq, GLA_WIDTH), BF16),
            jax.ShapeDtypeStruct((bsz, seq, GLA_WIDTH), BF16),
            jax.ShapeDtypeStruct((bsz, seq, GLA_DK), F32),
        ],
        compiler_params=_cparams("parallel", "parallel"),
        name="in_projection",
    )(x, mod_l, w_cat)


S5_SCAN_LANES = 256


def _s5_kernel(u_ref, lam_ref, wbre_ref, wbim_ref, wcre_ref, wcim_ref, d_ref, wglu_ref, bglu_ref,
               o_ref, hre, him, st_re, st_im, *, tt, nb):
    @pl.when(pl.program_id(0) == 0)
    def _():
        st_re[...] = jnp.zeros_like(st_re)
        st_im[...] = jnp.zeros_like(st_im)

    rows = tt * nb
    ub = u_ref[...].reshape(rows, S5_WIDTH)
    for n in range(8):
        lhs = ub[:, 128 * (n // 2):128 * (n // 2) + 128]
        hre[:, 256 * n:256 * (n + 1)] = _dot(lhs, wbre_ref[n])
        him[:, 256 * n:256 * (n + 1)] = _dot(lhs, wbim_ref[n])

    for c in range(S5_LANES // S5_SCAN_LANES):
        ls = slice(c * S5_SCAN_LANES, (c + 1) * S5_SCAN_LANES)
        lr = jnp.broadcast_to(lam_ref[0:1, ls], (nb, S5_SCAN_LANES))
        li = jnp.broadcast_to(lam_ref[1:2, ls], (nb, S5_SCAN_LANES))
        hr = st_re[:, ls]
        hi = st_im[:, ls]
        for t in range(tt):
            rs = slice(t * nb, (t + 1) * nb)
            nr = lr * hr - li * hi + hre[rs, ls]
            ni = lr * hi + li * hr + him[rs, ls]
            hre[rs, ls] = nr
            him[rs, ls] = ni
            hr, hi = nr, ni
        st_re[:, ls] = hr
        st_im[:, ls] = hi

    ys = []
    for m in range(2):
        ks = slice(1024 * m, 1024 * (m + 1))
        ys.append(_dot(hre[:, ks].astype(BF16), wcre_ref[m]) + _dot(him[:, ks].astype(BF16), wcim_ref[m]))
    y = jnp.concatenate(ys, axis=-1) + d_ref[...] * ub.astype(F32)
    y = 0.5 * y * (1.0 + jnp.tanh(math.sqrt(2.0 / math.pi) * (y + 0.044715 * (y * y * y))))
    z = y * jax.nn.sigmoid(_dot(y.astype(BF16), wglu_ref[...]) + bglu_ref[...])
    o_ref[...] = z.astype(BF16).reshape(tt, nb, S5_WIDTH)


def _s5_group(u_tm, lam_rows, wbre, wbim, wcre, wcim, d_skip, w_glu, b_glu, tt):
    seq, nb, width = u_tm.shape
    rows = tt * nb
    full = lambda a: pl.BlockSpec(a.shape, lambda t: (0,) * a.ndim)
    d2 = d_skip.reshape(1, width)
    b2 = b_glu.reshape(1, width)
    wg = w_glu.astype(BF16)
    return pl.pallas_call(
        functools.partial(_s5_kernel, tt=tt, nb=nb),
        grid=(seq // tt,),
        in_specs=[
            pl.BlockSpec((tt, nb, width), lambda t: (t, 0, 0)),
            full(lam_rows), full(wbre), full(wbim), full(wcre), full(wcim), full(d2), full(wg), full(b2),
        ],
        out_specs=pl.BlockSpec((tt, nb, width), lambda t: (t, 0, 0)),
        out_shape=jax.ShapeDtypeStruct((seq, nb, width), BF16),
        scratch_shapes=[
            pltpu.VMEM((rows, S5_LANES), F32),
            pltpu.VMEM((rows, S5_LANES), F32),
            pltpu.VMEM((nb, S5_LANES), F32),
            pltpu.VMEM((nb, S5_LANES), F32),
        ],
        compiler_params=_cparams("arbitrary"),
        name="s5_group",
    )(u_tm, lam_rows, wbre, wbim, wcre, wcim, d2, wg, b2)


def _gla_kernel(q_ref, k_ref, v_ref, g_ref, p_ref, ba_ref, gain_ref, o_ref, s_ref, oi_ref, upd_ref, *, nchunk):
    @pl.when(pl.program_id(1) == 0)
    def _():
        s_ref[...] = jnp.zeros_like(s_ref)

    c = GLA_CHUNK
    nh = GLA_HEADS
    srow = lax.broadcasted_iota(jnp.int32, (nh * c, c), 0)
    scol = lax.broadcasted_iota(jnp.int32, (nh * c, c), 1)
    causal = (srow & (c - 1)) >= scol
    lane = lax.broadcasted_iota(jnp.int32, (1, GLA_DK), 1)
    head_lanes = [(lane >= GLA_HEAD_K * h) & (lane < GLA_HEAD_K * (h + 1)) for h in range(nh)]
    vrow = lax.broadcasted_iota(jnp.int32, (2 * GLA_HEAD_V, 2 * GLA_HEAD_K), 0)
    kcol = lax.broadcasted_iota(jnp.int32, (2 * GLA_HEAD_V, 2 * GLA_HEAD_K), 1)
    same_head = (vrow < GLA_HEAD_V) == (kcol < GLA_HEAD_K)
    pair_k = [slice(2 * GLA_HEAD_K * pr, 2 * GLA_HEAD_K * (pr + 1)) for pr in range(nh // 2)]
    pair_v = [slice(2 * GLA_HEAD_V * pr, 2 * GLA_HEAD_V * (pr + 1)) for pr in range(nh // 2)]
    scale = GLA_HEAD_K ** -0.5
    mid = c // 2 - 1

    t = nchunk * c
    slab = math.gcd(t, 4 * c)
    trow = lax.broadcasted_iota(jnp.int32, (slab, slab), 0)
    tcol = lax.broadcasted_iota(jnp.int32, (slab, slab), 1)
    chunk_tri = jnp.where((tcol <= trow) & (tcol >= (trow & -c)), 1.0, 0.0).astype(BF16)
    pre = p_ref[0] + ba_ref[...]
    log_alpha = (jnp.minimum(pre, 0.0) - jnp.log(1.0 + jnp.exp(-jnp.abs(pre)))) * (1.0 / GLA_GATE_TAU)
    a1, a2, a3 = _split3(log_alpha)
    b = jnp.concatenate(
        [_dot(chunk_tri, a3[r0:r0 + slab]) + _dot(chunk_tri, a2[r0:r0 + slab]) + _dot(chunk_tri, a1[r0:r0 + slab])
         for r0 in range(0, t, slab)], axis=0).reshape(nchunk, c, GLA_DK)
    b_mid = b[:, mid:mid + 1, :]
    b_last = b[:, c - 1:c, :]
    q = (q_ref[0].astype(F32) * scale).reshape(nchunk, c, GLA_DK)
    k = k_ref[0].astype(F32).reshape(nchunk, c, GLA_DK)
    q_in = (q * jnp.exp(b)).astype(BF16)
    q_e = q * jnp.exp(b - b_mid)
    k_e = (k * jnp.exp(b_mid - b)).astype(BF16)
    k_d = (k * jnp.exp(b_last - b)).astype(BF16)
    decay = jnp.exp(b_last)
    for ci in range(nchunk):
        sl = slice(ci * c, (ci + 1) * c)
        q_stack = jnp.concatenate([jnp.where(head_lanes[h], q_e[ci], 0.0) for h in range(nh)], axis=0).astype(BF16)
        scores = lax.dot_general(q_stack, k_e[ci], (((1,), (1,)), ((), ())),
                                 preferred_element_type=F32)
        scores = jnp.where(causal, scores, 0.0).astype(BF16)
        v_all = v_ref[0, sl, :]
        pv = _dot(scores, v_all)
        for h in range(nh):
            os_ = slice(GLA_HEAD_V * h, GLA_HEAD_V * (h + 1))
            oi_ref[sl, os_] = pv[c * h:c * (h + 1), os_]
        for pr in range(nh // 2):
            upd = lax.dot_general(v_all[:, pair_v[pr]], k_d[ci][:, pair_k[pr]], (((0,), (0,)), ((), ())),
                                  preferred_element_type=F32)
            upd_ref[ci, pr] = jnp.where(same_head, upd, 0.0)

    for pr in range(nh // 2):
        s_t = s_ref[pr]
        for ci in range(nchunk):
            sl = slice(ci * c, (ci + 1) * c)
            oi_ref[sl, pair_v[pr]] += lax.dot_general(q_in[ci][:, pair_k[pr]], s_t.astype(BF16),
                                                      (((1,), (1,)), ((), ())), preferred_element_type=F32)
            s_t = decay[ci][:, pair_k[pr]] * s_t + upd_ref[ci, pr]
        s_ref[pr] = s_t

    for h in range(nh):
        os_ = slice(GLA_HEAD_V * h, GLA_HEAD_V * (h + 1))
        o_h = oi_ref[:, os_]
        ms = jnp.mean(o_h * o_h, axis=-1, keepdims=True)
        o_n = o_h * lax.rsqrt(ms + RMS_EPS) * gain_ref[:, os_]
        o_ref[0, :, os_] = (o_n * _silu(g_ref[0, :, os_].astype(F32))).astype(BF16)


def _gla_group(q, k, v, g_out, pre, b_alpha, head_gain, tg):
    bsz, seq, _ = q.shape
    blk = lambda w: pl.BlockSpec((1, tg, w), lambda b, t: (b, t, 0))
    ba = b_alpha.reshape(1, GLA_DK)
    gain = head_gain.reshape(1, GLA_WIDTH)
    return pl.pallas_call(
        functools.partial(_gla_kernel, nchunk=tg // GLA_CHUNK),
        grid=(bsz, seq // tg),
        in_specs=[
            blk(GLA_DK), blk(GLA_DK), blk(GLA_WIDTH), blk(GLA_WIDTH), blk(GLA_DK),
            pl.BlockSpec((1, GLA_DK), lambda b, t: (0, 0)),
            pl.BlockSpec((1, GLA_WIDTH), lambda b, t: (0, 0)),
        ],
        out_specs=blk(GLA_WIDTH),
        out_shape=jax.ShapeDtypeStruct((bsz, seq, GLA_WIDTH), BF16),
        scratch_shapes=[
            pltpu.VMEM((GLA_HEADS // 2, 2 * GLA_HEAD_V, 2 * GLA_HEAD_K), F32),
            pltpu.VMEM((tg, GLA_WIDTH), F32),
            pltpu.VMEM((tg // GLA_CHUNK, GLA_HEADS // 2, 2 * GLA_HEAD_V, 2 * GLA_HEAD_K), F32),
        ],
        compiler_params=_cparams("parallel", "arbitrary"),
        name="gla_group",
    )(q, k, v, g_out, pre, ba, gain)


def _mix_out(ys_ref, yg_ref, x_ref, mod_ref, w_ref, lg_ref, lb_ref, alpha):
    y = _dot(ys_ref[...], w_ref[0:S5_WIDTH, :]) + _dot(yg_ref[0], w_ref[S5_WIDTH:, :])
    gt = mod_ref[0, 2:3, :]
    x1 = _layer_norm(alpha * x_ref[0] + (1.0 + gt) * y, lg_ref[...], lb_ref[...])
    h2 = x1 * (1.0 + mod_ref[0, 4:5, :]) + mod_ref[0, 3:4, :]
    return x1, h2


RT_E1, RT_E2, RT_R1, RT_R2, RT_W1, RT_W2 = range(6)


def _top2_route(logits, count, earlier):
    lane =lax.broadcasted_iota(jnp.int32, logits.shape, 1).astype(F32)
    l0 = jnp.where(lane < N_EXPERTS, logits, NEG_BIG)
    m1 = jnp.max(l0, axis=-1, keepdims=True)
    i1 = jnp.min(jnp.where(l0 == m1, lane, float(ROUTER_LANES)), axis=-1, keepdims=True)
    sel1 = lane == i1
    l1 = jnp.where(sel1, NEG_BIG, l0)
    m2 = jnp.max(l1, axis=-1, keepdims=True)
    i2 = jnp.min(jnp.where(l1 == m2, lane, float(ROUTER_LANES)), axis=-1, keepdims=True)
    sel2 = lane == i2
    e2 = jnp.exp(m2 - m1)
    w1 = 1.0 / (1.0 + e2)
    w2 = e2 / (1.0 + e2)
    chosen = jnp.where(sel1, 1.0, 0.0) + jnp.where(sel2, 1.0, 0.0)
    before = _dot(earlier, chosen.astype(BF16)) + count
    r1 = jnp.sum(jnp.where(sel1, before, 0.0), axis=-1, keepdims=True)
    r2 = jnp.sum(jnp.where(sel2, before, 0.0), axis=-1, keepdims=True)
    rec = jnp.zeros_like(logits)
    for ln, val in ((RT_E1, i1), (RT_E2, i2), (RT_R1, r1), (RT_R2, r2), (RT_W1, w1), (RT_W2, w2)):
        rec = jnp.where(lane == float(ln), val, rec)
    return rec, count + jnp.sum(chosen, axis=0, keepdims=True)


def _outproj_router_kernel(ys_ref, yg_ref, x_ref, mod_ref, w_ref, lg_ref, lb_ref, wr_ref, br_ref, earlier_ref,
                           x1_ref, route_ref, count_ref, cnt, *, alpha):
    @pl.when((pl.program_id(0) == 0) & (pl.program_id(1) == 0))
    def _():
        cnt[...] = jnp.zeros_like(cnt)

    x1, h2 = _mix_out(ys_ref, yg_ref, x_ref, mod_ref, w_ref, lg_ref, lb_ref, alpha)
    x1_ref[0] = x1
    logits = _dot_f32x3(h2, wr_ref[...]) + br_ref[...]
    rec, new_count = _top2_route(logits, cnt[...], earlier_ref[...])
    route_ref[0] = rec
    cnt[...] = new_count
    count_ref[...] = jnp.broadcast_to(new_count, count_ref.shape)


def _out_projection_router(y_s5_tm, y_gla, x, mod_l, w_out, ln_g, ln_b, w_router, b_router, alpha, tm):
    bsz, seq, d = x.shape
    row = lambda a: a.reshape(1, -1)
    const = lambda shape: pl.BlockSpec(shape, lambda b, t: (0,) * len(shape))
    pad = ROUTER_LANES - N_EXPERTS
    return pl.pallas_call(
        functools.partial(_outproj_router_kernel, alpha=alpha),
        grid=(bsz, seq // tm),
        in_specs=[
            pl.BlockSpec((tm, S5_WIDTH), lambda b, t: (t, b)),
            pl.BlockSpec((1, tm, GLA_WIDTH), lambda b, t: (b, t, 0)),
            pl.BlockSpec((1, tm, d), lambda b, t: (b, t, 0)),
            pl.BlockSpec((1, N_MOD, d), lambda b, t: (b, 0, 0)),
            const((d, d)), const((1, d)), const((1, d)),
            const((d, ROUTER_LANES)), const((1, ROUTER_LANES)), const((tm, tm)),
        ],
        out_specs=[
            pl.BlockSpec((1, tm, d), lambda b, t: (b, t, 0)),
            pl.BlockSpec((1, tm, ROUTER_LANES), lambda b, t: (b, t, 0)),
            const((8, ROUTER_LANES)),
        ],
        out_shape=[
            jax.ShapeDtypeStruct((bsz, seq, d), F32),
            jax.ShapeDtypeStruct((bsz, seq, ROUTER_LANES), F32),
            jax.ShapeDtypeStruct((8, ROUTER_LANES), F32),
        ],
        scratch_shapes=[pltpu.VMEM((1, ROUTER_LANES), F32)],
        compiler_params=_cparams("arbitrary", "arbitrary"),
        name="out_projection",
    )(y_s5_tm, y_gla, x, mod_l, w_out.astype(BF16), row(ln_g), row(ln_b),
      jnp.pad(w_router, ((0, 0), (0, pad))), jnp.pad(row(b_router), ((0, 0), (0, pad))),
      jnp.tri(tm, k=-1, dtype=BF16))


def _mix_ffn_kernel(ys_ref, yg_ref, x_ref, mod_ref, wo_ref, lg1_ref, lb1_ref, wg_ref, wu_ref, wd_ref,
                    lg2_ref, lb2_ref, o_ref, acc_ref, x1_ref, hb_ref, *, alpha):
    j = pl.program_id(1)

    @pl.when(j == 0)
    def _():
        x1, h2 = _mix_out(ys_ref, yg_ref, x_ref, mod_ref, wo_ref, lg1_ref, lb1_ref, alpha)
        x1_ref[...] = x1
        hb_ref[...] = h2.astype(BF16)
        acc_ref[...] = jnp.zeros_like(acc_ref)

    h = hb_ref[...]
    a = _dot(h, wg_ref[...])
    mid = (_silu(a) * _dot(h, wu_ref[...])).astype(BF16)
    acc_ref[...] += _dot(mid, wd_ref[...])

    @pl.when(j == pl.num_programs(1) - 1)
    def _():
        gt = mod_ref[0, 5:6, :]
        o_ref[0] = _layer_norm(alpha * x1_ref[...] + (1.0 + gt) * acc_ref[...], lg2_ref[...], lb2_ref[...])


def _mix_dense_ffn(y_s5_tm, y_gla, x, mod_l, w_out, ln1_g, ln1_b, w_gate, w_up, w_down, ln2_g, ln2_b,
                   alpha, tm, tf):
    bsz, seq, d = x.shape
    dff = w_gate.shape[1]
    per_b = seq // tm
    row = lambda a: a.reshape(1, d)
    const = lambda shape: pl.BlockSpec(shape, lambda i, j: (0,) * len(shape))
    tok = lambda w: pl.BlockSpec((1, tm, w), lambda i, j: (i // per_b, i % per_b, 0))
    return pl.pallas_call(
        functools.partial(_mix_ffn_kernel, alpha=alpha),
        grid=(bsz * per_b, dff // tf),
        in_specs=[
            pl.BlockSpec((tm, S5_WIDTH), lambda i, j: (i % per_b, i // per_b)),
            tok(GLA_WIDTH),
            tok(d),
            pl.BlockSpec((1, N_MOD, d), lambda i, j: (i // per_b, 0, 0)),
            const((d, d)), const((1, d)), const((1, d)),
            pl.BlockSpec((d, tf), lambda i, j: (0, j)),
            pl.BlockSpec((d, tf), lambda i, j: (0, j)),
            pl.BlockSpec((tf, d), lambda i, j: (j, 0)),
            const((1, d)), const((1, d)),
        ],
        out_specs=tok(d),
        out_shape=jax.ShapeDtypeStruct((bsz, seq, d), F32),
        scratch_shapes=[pltpu.VMEM((tm, d), F32), pltpu.VMEM((tm, d), F32), pltpu.VMEM((tm, d), BF16)],
        compiler_params=_cparams("parallel", "arbitrary"),
        name="mix_dense_ffn",
    )(y_s5_tm, y_gla, x, mod_l, w_out.astype(BF16), row(ln1_g), row(ln1_b),
      w_gate.astype(BF16), w_up.astype(BF16), w_down.astype(BF16), row(ln2_g), row(ln2_b))


def _route_tables(route, count, rb, nblk):
    e = route[:, RT_E1:RT_E2 + 1].astype(jnp.int32)
    rank = route[:, RT_R1:RT_R2 + 1].astype(jnp.int32)
    cnt = count[0, :N_EXPERTS].astype(jnp.int32)
    padded = ((cnt + rb - 1) // rb) * rb
    ends = jnp.cumsum(padded)
    starts = ends - padded
    pos = jnp.sum(jnp.where(e[..., None] == jnp.arange(N_EXPERTS), starts, 0), axis=-1) + rank
    n_valid = ends[-1] // rb
    blk = jnp.minimum(jnp.arange(nblk, dtype=jnp.int32), n_valid - 1)
    blk_expert = jnp.sum((blk[:, None] * rb >= ends[None, :]).astype(jnp.int32), axis=-1)
    blk_expert = jnp.minimum(blk_expert, N_EXPERTS - 1)
    n_rows = jnp.full((1,), nblk * rb, jnp.int32)
    fill = jnp.stack([jnp.concatenate([starts + cnt, ends[-1:]]), jnp.concatenate([ends, n_rows])], axis=1)
    return (pos.astype(jnp.int32), blk_expert.astype(jnp.int32), n_valid.astype(jnp.int32).reshape(1),
            fill.reshape(-1).astype(jnp.int32))


SUBLANES = 8


def _staged_row(ref, tile, sub):
    return ref.at[tile, pl.ds(sub, 1), :]


def _dispatch_kernel(fill_ref, pos_ref, x_ref, mod_ref, xs_hbm, hs_ref, sem, *, tb):
    i = pl.program_id(0)
    h = x_ref[...] * (1.0 + mod_ref[0, 4:5, :]) + mod_ref[0, 3:4, :]
    hs_ref[...] = h.reshape(tb // SUBLANES, SUBLANES, h.shape[-1])

    def issue(r8, carry):
        for sub in range(SUBLANES):
            for kk in range(2):
                dst_row = pos_ref[0, 0, 2 * SUBLANES * r8 + 2 * sub + kk]
                pltpu.make_async_copy(_staged_row(hs_ref, r8, sub), xs_hbm.at[pl.ds(dst_row, 1), :],
                                      sem.at[0]).start()
        return carry

    lax.fori_loop(0, tb // SUBLANES, issue, 0)

    @pl.when(i == 0)
    def _():
        for e in range(N_EXPERTS + 1):
            lo = fill_ref[2 * e]
            hi = fill_ref[2 * e + 1]

            def fill(p, carry):
                pltpu.make_async_copy(_staged_row(hs_ref, 0, 0), xs_hbm.at[pl.ds(p, 1), :], sem.at[1]).start()
                return carry

            def drain(p, carry):
                pltpu.make_async_copy(_staged_row(hs_ref, 0, 0), xs_hbm.at[pl.ds(p, 1), :], sem.at[1]).wait()
                return carry

            lax.fori_loop(lo, hi, fill, 0)
            lax.fori_loop(lo, hi, drain, 0)

    for _ in range(2):
        pltpu.make_async_copy(x_ref, xs_hbm.at[pl.ds(0, tb), :], sem.at[0]).wait()


def _dispatch(x1, mod_l, pos, fill, n_rows, tb, seq):
    n, d = x1.shape
    per_b = seq // tb
    grid_spec = pltpu.PrefetchScalarGridSpec(
        num_scalar_prefetch=1,
        grid=(n // tb,),
        in_specs=[
            pl.BlockSpec((1, 1, 2 * tb), lambda i, f: (i, 0, 0), memory_space=pltpu.SMEM),
            pl.BlockSpec((tb, d), lambda i, f: (i, 0)),
            pl.BlockSpec((1, N_MOD, d), lambda i, f: (i // per_b, 0, 0)),
        ],
        out_specs=pl.BlockSpec(memory_space=pl.ANY),
        scratch_shapes=[pltpu.VMEM((tb // SUBLANES, SUBLANES, d), F32), pltpu.SemaphoreType.DMA((2,))],
    )
    return pl.pallas_call(
        functools.partial(_dispatch_kernel, tb=tb),
        grid_spec=grid_spec,
        out_shape=jax.ShapeDtypeStruct((n_rows, d), F32),
        compiler_params=_cparams("arbitrary"),
        name="moe_dispatch",
    )(fill, pos.reshape(n // tb, 1, 2 * tb), x1, mod_l)


def _expert_kernel(be_ref, nv_ref, xs_ref, wg_ref, wu_ref, wd_ref, y_ref, acc_ref, xb_ref):
    del be_ref
    i = pl.program_id(0)
    j = pl.program_id(1)

    @pl.when(i < nv_ref[0])
    def _():
        @pl.when(j == 0)
        def _():
            acc_ref[...] = jnp.zeros_like(acc_ref)
            xb_ref[...] = xs_ref[...].astype(BF16)

        h = xb_ref[...]
        a = _dot(h, wg_ref[0])
        mid = (_silu(a) * _dot(h, wu_ref[0])).astype(BF16)
        acc_ref[...] += _dot(mid, wd_ref[0])

        @pl.when(j == pl.num_programs(1) - 1)
        def _():
            y_ref[...] = acc_ref[...]

    @pl.when((i >= nv_ref[0]) & (j == 0))
    def _():
        y_ref[...] = jnp.zeros_like(y_ref)


def _experts(xs, blk_expert, n_valid, w_gate, w_up, w_down, rb, tf):
    n_rows, d = xs.shape
    dff = w_gate.shape[2]
    nj = dff // tf
    row_blk = lambda i, j, be, nv: (jnp.minimum(i, nv[0] - 1), 0)
    jj = lambda i, j, nv: jnp.where(i < nv[0], j, nj - 1)
    grid_spec = pltpu.PrefetchScalarGridSpec(
        num_scalar_prefetch=2,
        grid=(n_rows // rb, nj),
        in_specs=[
            pl.BlockSpec((rb, d), row_blk),
            pl.BlockSpec((1, d, tf), lambda i, j, be, nv: (be[i], 0, jj(i, j, nv))),
            pl.BlockSpec((1, d, tf), lambda i, j, be, nv: (be[i], 0, jj(i, j, nv))),
            pl.BlockSpec((1, tf, d), lambda i, j, be, nv: (be[i], jj(i, j, nv), 0)),
        ],
        out_specs=pl.BlockSpec((rb, d), lambda i, j, be, nv: (i, 0)),
        scratch_shapes=[pltpu.VMEM((rb, d), F32), pltpu.VMEM((rb, d), BF16)],
    )
    return pl.pallas_call(
        _expert_kernel,
        grid_spec=grid_spec,
        out_shape=jax.ShapeDtypeStruct((n_rows, d), F32),
        compiler_params=_cparams("arbitrary", "arbitrary"),
        name="moe_experts",
    )(blk_expert, n_valid, xs, w_gate.astype(BF16), w_up.astype(BF16), w_down.astype(BF16))


def _combine_kernel(pos_ref, posn_ref, route_ref, x_ref, mod_ref, lg_ref, lb_ref, y_hbm, o_ref, buf, sem,
                    *, tb, alpha):
    i = pl.program_id(0)
    slot = i % 2
    d = o_ref.shape[-1]
    ntile = tb // SUBLANES

    def issue_tile(p_ref, s, r8):
        for sub in range(SUBLANES):
            for kk in range(2):
                src_row = p_ref[0, 0, 2 * SUBLANES * r8 + 2 * sub + kk]
                pltpu.make_async_copy(y_hbm.at[pl.ds(src_row, 1), :], _staged_row(buf.at[s, kk], r8, sub),
                                      sem.at[s]).start()

    def wait_slot(s):
        for _ in range(2):
            pltpu.make_async_copy(y_hbm.at[pl.ds(0, tb), :], o_ref, sem.at[s]).wait()

    def issue_all(p_ref, s):
        def body(r8, carry):
            issue_tile(p_ref, s, r8)
            return carry
        lax.fori_loop(0, ntile, body, 0)

    @pl.when(i == 0)
    def _():
        issue_all(pos_ref, 0)

    issue_all(posn_ref, 1 - slot)
    wait_slot(slot)
    rec = route_ref[...]
    y0 = buf[slot, 0].reshape(tb, d)
    y1 = buf[slot, 1].reshape(tb, d)
    f = rec[:, RT_W1:RT_W1 + 1] * y0 + rec[:, RT_W2:RT_W2 + 1] * y1
    gt = mod_ref[0, 5:6, :]
    o_ref[...] = _layer_norm(alpha * x_ref[...] + (1.0 + gt) * f, lg_ref[...], lb_ref[...])

    @pl.when(i == pl.num_programs(0) - 1)
    def _():
        wait_slot(1 - slot)


def _combine(ys, pos, route, x1, mod_l, ln_g, ln_b, alpha, tb, seq):
    n, d = x1.shape
    nb = n // tb
    per_b = seq // tb
    pos3 = pos.reshape(nb, 1, 2 * tb)
    return pl.pallas_call(
        functools.partial(_combine_kernel, tb=tb, alpha=alpha),
        grid=(nb,),
        in_specs=[
            pl.BlockSpec((1, 1, 2 * tb), lambda i: (i, 0, 0), memory_space=pltpu.SMEM),
            pl.BlockSpec((1, 1, 2 * tb), lambda i: (jnp.minimum(i + 1, nb - 1), 0, 0), memory_space=pltpu.SMEM),
            pl.BlockSpec((tb, ROUTER_LANES), lambda i: (i, 0)),
            pl.BlockSpec((tb, d), lambda i: (i, 0)),
            pl.BlockSpec((1, N_MOD, d), lambda i: (i // per_b, 0, 0)),
            pl.BlockSpec((1, d), lambda i: (0, 0)),
            pl.BlockSpec((1, d), lambda i: (0, 0)),
            pl.BlockSpec(memory_space=pl.ANY),
        ],
        out_specs=pl.BlockSpec((tb, d), lambda i: (i, 0)),
        out_shape=jax.ShapeDtypeStruct((n, d), F32),
        scratch_shapes=[pltpu.VMEM((2, 2, tb // SUBLANES, SUBLANES, d), F32), pltpu.SemaphoreType.DMA((2,))],
        compiler_params=_cparams("arbitrary"),
        name="moe_combine",
    )(pos3, pos3, route, x1, mod_l, ln_g.reshape(1, d), ln_b.reshape(1, d), ys)


def _moe_ffn(x1, route, count, mod_l, w_gate, w_up, w_down, ln_g, ln_b, alpha, rb, tf, tb):
    bsz, seq, d = x1.shape
    n = bsz * seq
    n_rows = 2 * n + N_EXPERTS * rb
    route2 = route.reshape(n, ROUTER_LANES)
    pos, blk_expert, n_valid, fill = _route_tables(route2, count, rb, n_rows // rb)
    xs = _dispatch(x1.reshape(n, d), mod_l, pos, fill, n_rows, rb, seq)
    ys = _experts(xs, blk_expert, n_valid, w_gate, w_up, w_down, rb, tf)
    out = _combine(ys, pos, route2, x1.reshape(n, d), mod_l, ln_g, ln_b, alpha, tb, seq)
    return out.reshape(bsz, seq, d)


def _tile(n, want):
    t = min(n, want)
    assert n % t == 0, (n, want)
    return t


def kernel(x, c, mod_w, mod_b, w_in, w_out, s5_lam_re, s5_lam_im, s5_log_dt, s5_b_re, s5_b_im, s5_c_re, s5_c_im, s5_d, s5_w_glu, s5_b_glu, gla_w_alpha_up, gla_b_alpha, gla_head_gain, ln_mix_g, ln_mix_b, ffn_w_gate, ffn_w_up, ffn_w_down, moe_w_router, moe_b_router, moe_w_gate, moe_w_up, moe_w_down, ln_ffn_g, ln_ffn_b):
    bsz, seq, d = x.shape
    depth = mod_w.shape[0]
    alpha = (2.0 * depth) ** 0.25
    tm = _tile(seq, 512)
    tt = _tile(seq, 32)
    tg = _tile(seq, 1024)
    tm_ffn = _tile(seq, 1024)
    tf = _tile(D_FF, 512)
    tb = _tile(seq, 512)

    mod = _modulation(c, mod_w, mod_b).reshape(depth, bsz, N_MOD, d)
    for layer in range(depth):
        mod_l = mod[layer]
        w_gate_fold = _gate_fold(w_in[layer][:, IN_P[0]:], gla_w_alpha_up[layer])
        w_cat = jnp.concatenate([w_in[layer][:, :IN_P[0]], w_gate_fold], axis=1).astype(BF16)
        u_tm, q, k, v, g_out, pre = _in_projection(x, mod_l, w_cat, tm_ffn)

        s5w = _s5_weights(s5_lam_re[layer], s5_lam_im[layer], s5_log_dt[layer], s5_b_re[layer], s5_b_im[layer],
                          s5_c_re[layer], s5_c_im[layer])
        y_s5 = _s5_group(u_tm.reshape(seq, bsz, S5_WIDTH), *s5w, s5_d[layer], s5_w_glu[layer], s5_b_glu[layer], tt)
        y_s5 = y_s5.reshape(seq, bsz * S5_WIDTH)
        y_gla = _gla_group(q, k, v, g_out, pre, gla_b_alpha[layer], gla_head_gain[layer], tg)

        i = layer // 2
        if layer % 2 == 0:
            x = _mix_dense_ffn(y_s5, y_gla, x, mod_l, w_out[layer], ln_mix_g[layer], ln_mix_b[layer],
                               ffn_w_gate[i], ffn_w_up[i], ffn_w_down[i], ln_ffn_g[layer], ln_ffn_b[layer],
                               alpha, tm_ffn, tf)
        else:
            x1, route, count = _out_projection_router(y_s5, y_gla, x, mod_l, w_out[layer], ln_mix_g[layer],
                                                      ln_mix_b[layer], moe_w_router[i], moe_b_router[i], alpha, tm)
            x = _moe_ffn(x1, route, count, mod_l, moe_w_gate[i], moe_w_up[i], moe_w_down[i],
                         ln_ffn_g[layer], ln_ffn_b[layer], alpha, tm_ffn, tf, tb)
    return x
```

```python
import functools
import math

import jax
import jax.numpy as jnp
from jax import lax
from jax.experimental import pallas as pl
from jax.experimental.pallas import tpu as pltpu

F32 = jnp.float32
BF16 = jnp.bfloat16

D_MODEL = 1024
S5_WIDTH = 512
S5_GROUP_CH = 16
S5_GROUPS = 32
S5_STATE = 64
S5_LANES = S5_GROUPS * S5_STATE
GLA_WIDTH = 512
GLA_HEADS = 4
GLA_DK = 256
GLA_HEAD_K = 64
GLA_HEAD_V = 128
GLA_GATE_RANK = 16
GLA_GATE_TAU = 16.0
GLA_CHUNK = 64
D_FF = 3584
N_EXPERTS = 8
N_MOD = 6
LN_EPS = 1e-5
RMS_EPS = 1e-6
ROUTER_LANES = 128
NEG_BIG = -1e30

VMEM_LIMIT = 56 * 1024 * 1024


def _cparams(*sem):
    return pltpu.CompilerParams(dimension_semantics=sem, vmem_limit_bytes=VMEM_LIMIT)


def _dot(a, b):
    return jnp.dot(a, b, preferred_element_type=F32)


def _split2(a):
    a1 = a.astype(BF16)
    a2 = (a - a1.astype(F32)).astype(BF16)
    return a1, a2


def _split3(a):
    a1 = a.astype(BF16)
    r1 = a - a1.astype(F32)
    a2 = r1.astype(BF16)
    a3 = (r1 - a2.astype(F32)).astype(BF16)
    return a1, a2, a3


def _dot_f32(a, b):
    a1, a2, a3 = _split3(a)
    b1, b2, b3 = _split3(b)
    lo = _dot(a1, b3) + _dot(a2, b2) + _dot(a3, b1)
    mid = _dot(a1, b2) + _dot(a2, b1)
    return lo + mid + _dot(a1, b1)


def _dot_f32x3(a, b):
    a1, a2 = _split2(a)
    b1, b2 = _split2(b)
    return (_dot(a1, b2) + _dot(a2, b1)) + _dot(a1, b1)


def _silu(x):
    return x * jax.nn.sigmoid(x)


def _layer_norm(r, gain, bias):
    mu = jnp.mean(r, axis=-1, keepdims=True)
    d = r - mu
    var = jnp.mean(d * d, axis=-1, keepdims=True)
    return d * lax.rsqrt(var + LN_EPS) * gain + bias


def _mod_kernel(c_ref, w_ref, b_ref, o_ref):
    o_ref[0] = _dot_f32(_silu(c_ref[...]), w_ref[0]) + b_ref[0]


def _modulation(c, mod_w, mod_b):
    depth, d, nd = mod_w.shape
    bsz = c.shape[0]
    return pl.pallas_call(
        _mod_kernel,
        grid=(depth, nd // d),
        in_specs=[
            pl.BlockSpec((bsz, d), lambda l, n: (0, 0)),
            pl.BlockSpec((1, d, d), lambda l, n: (l, 0, n)),
            pl.BlockSpec((1, 1, d), lambda l, n: (l, 0, n)),
        ],
        out_specs=pl.BlockSpec((1, bsz, d), lambda l, n: (l, 0, n)),
        out_shape=jax.ShapeDtypeStruct((depth, bsz, nd), F32),
        compiler_params=_cparams("parallel", "parallel"),
        name="modulation",
    )(c, mod_w, mod_b.reshape(depth, 1, nd))


def _gate_fold_kernel(wl_ref, wu_ref, o_ref):
    o_ref[...] = _dot_f32(wl_ref[...], wu_ref[...])


def _gate_fold(w_low, w_up):
    d = w_low.shape[0]
    pad = 128 - GLA_GATE_RANK
    wl = jnp.pad(w_low, ((0, 0), (0, pad)))
    wu = jnp.pad(w_up, ((0, pad), (0, 0)))
    return pl.pallas_call(
        _gate_fold_kernel,
        out_shape=jax.ShapeDtypeStruct((d, GLA_DK), F32),
        name="gate_fold",
    )(wl, wu)


def _s5_disc_kernel(lre_ref, lim_ref, ldt_ref, bre_ref, bim_ref, lam_ref, bbre_ref, bbim_ref):
    lre = lre_ref[...]
    lim = lim_ref[...]
    dt = jnp.exp(ldt_ref[...])
    mag = jnp.exp(lre * dt)
    ang = lim * dt
    are = mag * jnp.cos(ang)
    aim = mag * jnp.sin(ang)
    lam_ref[0] = are
    lam_ref[1] = aim
    den = lre * lre + lim * lim
    nre = are - 1.0
    cre = (nre * lre + aim * lim) / den
    cim = (aim * lre - nre * lim) / den
    for h in range(S5_GROUP_CH):
        bre = bre_ref[h]
        bim = bim_ref[h]
        bbre_ref[h] = cre * bre - cim * bim
        bbim_ref[h] = cre * bim + cim * bre


def _s5_discretise(lam_re, lam_im, log_dt, b_re, b_im):
    g, p = lam_re.shape
    hch = b_re.shape[-1]
    bre_t = jnp.transpose(b_re, (2, 0, 1))
    bim_t = jnp.transpose(b_im, (2, 0, 1))
    lam, bbre, bbim = pl.pallas_call(
        _s5_disc_kernel,
        out_shape=(
            jax.ShapeDtypeStruct((2, g, p), F32),
            jax.ShapeDtypeStruct((hch, g, p), F32),
            jax.ShapeDtypeStruct((hch, g, p), F32),
        ),
        name="s5_discretise",
    )(lam_re, lam_im, log_dt.reshape(g, 1), bre_t, bim_t)
    return lam, bbre, bbim


def _s5_weights(lam_re, lam_im, log_dt, b_re, b_im, c_re, c_im):
    g, p, hch = S5_GROUPS, S5_STATE, S5_GROUP_CH
    lam, bbre, bbim = _s5_discretise(lam_re, lam_im, log_dt, b_re, b_im)
    lam_rows = lam.reshape(2, g * p)
    eye = jnp.eye(g, dtype=F32)

    def in_blocks(bb):
        dense = jnp.einsum('hgp,gk->ghkp', bb, eye).reshape(g * hch, g * p)
        blocks = [dense[128 * (n // 2):128 * (n // 2) + 128, 256 * n:256 * (n + 1)] for n in range(8)]
        return jnp.stack(blocks).astype(BF16)

    def out_blocks(cc, sign):
        dense = jnp.einsum('ghp,gk->gpkh', cc, eye).reshape(g * p, g * hch) * sign
        blocks = [dense[1024 * m:1024 * (m + 1), 256 * m:256 * (m + 1)] for m in range(2)]
        return jnp.stack(blocks).astype(BF16)

    return lam_rows, in_blocks(bbre), in_blocks(bbim), out_blocks(c_re, 1.0), out_blocks(c_im, -1.0)


IN_U = (0, 512)
IN_Q = (512, 768)
IN_K = (768, 1024)
IN_V = (1024, 1536)
IN_G = (1536, 2048)
IN_P = (2048, 2304)


def _inproj_kernel(x_ref, mod_ref, w_ref, u_ref, q_ref, k_ref, v_ref, g_ref, p_ref):
    sh = mod_ref[0, 0:1, :]
    sc = mod_ref[0, 1:2, :]
    h = (x_ref[0] * (1.0 + sc) + sh).astype(BF16)
    u_ref[...] = _dot(h, w_ref[:, IN_U[0]:IN_U[1]]).astype(BF16)
    q_ref[0] = _dot(h, w_ref[:, IN_Q[0]:IN_Q[1]]).astype(BF16)
    k_ref[0] = _dot(h, w_ref[:, IN_K[0]:IN_K[1]]).astype(BF16)
    v_ref[0] = _dot(h, w_ref[:, IN_V[0]:IN_V[1]]).astype(BF16)
    g_ref[0] = _dot(h, w_ref[:, IN_G[0]:IN_G[1]]).astype(BF16)
    p_ref[0] = _dot(h, w_ref[:, IN_P[0]:IN_P[1]])


def _in_projection(x, mod_l, w_cat, tm):
    bsz, seq, d = x.shape
    ncol = w_cat.shape[1]
    bspec = lambda w: pl.BlockSpec((1, tm, w), lambda b, t: (b, t, 0))
    return pl.pallas_call(
        _inproj_kernel,
        grid=(bsz, seq // tm),
        in_specs=[
            pl.BlockSpec((1, tm, d), lambda b, t: (b, t, 0)),
            pl.BlockSpec((1, N_MOD, d), lambda b, t: (b, 0, 0)),
            pl.BlockSpec((d, ncol), lambda b, t: (0, 0)),
        ],
        out_specs=[
            pl.BlockSpec((tm, S5_WIDTH), lambda b, t: (t, b)),
            bspec(GLA_DK), bspec(GLA_DK), bspec(GLA_WIDTH), bspec(GLA_WIDTH), bspec(GLA_DK),
        ],
        out_shape=[
            jax.ShapeDtypeStruct((seq, bsz * S5_WIDTH), BF16),
            jax.ShapeDtypeStruct((bsz, seq, GLA_DK), BF16),
            jax.ShapeDtypeStruct((bsz, seq, GLA_DK), BF16),
            jax.ShapeDtypeStruct((bsz, seq, GLA_WIDTH), BF16),
            jax.ShapeDtypeStruct((bsz, seq, GLA_WIDTH), BF16),
            jax.ShapeDtypeStruct((bsz, seq, GLA_DK), F32),
        ],
        compiler_params=_cparams("parallel", "parallel"),
        name="in_projection",
    )(x, mod_l, w_cat)


S5_SCAN_LANES = 256


def _s5_kernel(u_ref, lam_ref, wbre_ref, wbim_ref, wcre_ref, wcim_ref, d_ref, wglu_ref, bglu_ref,
               o_ref, hre, him, st_re, st_im, *, tt, nb):
    @pl.when(pl.program_id(0) == 0)
    def _():
        st_re[...] = jnp.zeros_like(st_re)
        st_im[...] = jnp.zeros_like(st_im)

    rows = tt * nb
    ub = u_ref[...].reshape(rows, S5_WIDTH)
    for n in range(8):
        lhs = ub[:, 128 * (n // 2):128 * (n // 2) + 128]
        hre[:, 256 * n:256 * (n + 1)] = _dot(lhs, wbre_ref[n])
        him[:, 256 * n:256 * (n + 1)] = _dot(lhs, wbim_ref[n])

    for c in range(S5_LANES // S5_SCAN_LANES):
        ls = slice(c * S5_SCAN_LANES, (c + 1) * S5_SCAN_LANES)
        lr = jnp.broadcast_to(lam_ref[0:1, ls], (nb, S5_SCAN_LANES))
        li = jnp.broadcast_to(lam_ref[1:2, ls], (nb, S5_SCAN_LANES))
        hr = st_re[:, ls]
        hi = st_im[:, ls]
        for t in range(tt):
            rs = slice(t * nb, (t + 1) * nb)
            nr = lr * hr - li * hi + hre[rs, ls]
            ni = lr * hi + li * hr + him[rs, ls]
            hre[rs, ls] = nr
            him[rs, ls] = ni
            hr, hi = nr, ni
        st_re[:, ls] = hr
        st_im[:, ls] = hi

    ys = []
    for m in range(2):
        ks = slice(1024 * m, 1024 * (m + 1))
        ys.append(_dot(hre[:, ks].astype(BF16), wcre_ref[m]) + _dot(him[:, ks].astype(BF16), wcim_ref[m]))
    y = jnp.concatenate(ys, axis=-1) + d_ref[...] * ub.astype(F32)
    y = 0.5 * y * (1.0 + jnp.tanh(math.sqrt(2.0 / math.pi) * (y + 0.044715 * (y * y * y))))
    z = y * jax.nn.sigmoid(_dot(y.astype(BF16), wglu_ref[...]) + bglu_ref[...])
    o_ref[...] = z.astype(BF16).reshape(tt, nb, S5_WIDTH)


def _s5_group(u_tm, lam_rows, wbre, wbim, wcre, wcim, d_skip, w_glu, b_glu, tt):
    seq, nb, width = u_tm.shape
    rows = tt * nb
    full = lambda a: pl.BlockSpec(a.shape, lambda t: (0,) * a.ndim)
    d2 = d_skip.reshape(1, width)
    b2 = b_glu.reshape(1, width)
    wg = w_glu.astype(BF16)
    return pl.pallas_call(
        functools.partial(_s5_kernel, tt=tt, nb=nb),
        grid=(seq // tt,),
        in_specs=[
            pl.BlockSpec((tt, nb, width), lambda t: (t, 0, 0)),
            full(lam_rows), full(wbre), full(wbim), full(wcre), full(wcim), full(d2), full(wg), full(b2),
        ],
        out_specs=pl.BlockSpec((tt, nb, width), lambda t: (t, 0, 0)),
        out_shape=jax.ShapeDtypeStruct((seq, nb, width), BF16),
        scratch_shapes=[
            pltpu.VMEM((rows, S5_LANES), F32),
            pltpu.VMEM((rows, S5_LANES), F32),
            pltpu.VMEM((nb, S5_LANES), F32),
            pltpu.VMEM((nb, S5_LANES), F32),
        ],
        compiler_params=_cparams("arbitrary"),
        name="s5_group",
    )(u_tm, lam_rows, wbre, wbim, wcre, wcim, d2, wg, b2)


def _gla_kernel(q_ref, k_ref, v_ref, g_ref, p_ref, ba_ref, gain_ref, o_ref, s_ref, oi_ref, upd_ref, *, nchunk):
    @pl.when(pl.program_id(1) == 0)
    def _():
        s_ref[...] = jnp.zeros_like(s_ref)

    c = GLA_CHUNK
    nh = GLA_HEADS
    srow = lax.broadcasted_iota(jnp.int32, (nh * c, c), 0)
    scol = lax.broadcasted_iota(jnp.int32, (nh * c, c), 1)
    causal = (srow & (c - 1)) >= scol
    lane = lax.broadcasted_iota(jnp.int32, (1, GLA_DK), 1)
    head_lanes = [(lane >= GLA_HEAD_K * h) & (lane < GLA_HEAD_K * (h + 1)) for h in range(nh)]
    vrow = lax.broadcasted_iota(jnp.int32, (2 * GLA_HEAD_V, 2 * GLA_HEAD_K), 0)
    kcol = lax.broadcasted_iota(jnp.int32, (2 * GLA_HEAD_V, 2 * GLA_HEAD_K), 1)
    same_head = (vrow < GLA_HEAD_V) == (kcol < GLA_HEAD_K)
    pair_k = [slice(2 * GLA_HEAD_K * pr, 2 * GLA_HEAD_K * (pr + 1)) for pr in range(nh // 2)]
    pair_v = [slice(2 * GLA_HEAD_V * pr, 2 * GLA_HEAD_V * (pr + 1)) for pr in range(nh // 2)]
    scale = GLA_HEAD_K ** -0.5
    mid = c // 2 - 1

    t = nchunk * c
    slab = math.gcd(t, 4 * c)
    trow = lax.broadcasted_iota(jnp.int32, (slab, slab), 0)
    tcol = lax.broadcasted_iota(jnp.int32, (slab, slab), 1)
    chunk_tri = jnp.where((tcol <= trow) & (tcol >= (trow & -c)), 1.0, 0.0).astype(BF16)
    pre = p_ref[0] + ba_ref[...]
    log_alpha = (jnp.minimum(pre, 0.0) - jnp.log(1.0 + jnp.exp(-jnp.abs(pre)))) * (1.0 / GLA_GATE_TAU)
    a1, a2, a3 = _split3(log_alpha)
    b = jnp.concatenate(
        [_dot(chunk_tri, a3[r0:r0 + slab]) + _dot(chunk_tri, a2[r0:r0 + slab]) + _dot(chunk_tri, a1[r0:r0 + slab])
         for r0 in range(0, t, slab)], axis=0).reshape(nchunk, c, GLA_DK)
    b_mid = b[:, mid:mid + 1, :]
    b_last = b[:, c - 1:c, :]
    q = (q_ref[0].astype(F32) * scale).reshape(nchunk, c, GLA_DK)
    k = k_ref[0].astype(F32).reshape(nchunk, c, GLA_DK)
    q_in = (q * jnp.exp(b)).astype(BF16)
    q_e = q * jnp.exp(b - b_mid)
    k_e = (k * jnp.exp(b_mid - b)).astype(BF16)
    k_d = (k * jnp.exp(b_last - b)).astype(BF16)
    decay = jnp.exp(b_last)
    for ci in range(nchunk):
        sl = slice(ci * c, (ci + 1) * c)
        q_stack = jnp.concatenate([jnp.where(head_lanes[h], q_e[ci], 0.0) for h in range(nh)], axis=0).astype(BF16)
        scores = lax.dot_general(q_stack, k_e[ci], (((1,), (1,)), ((), ())),
                                 preferred_element_type=F32)
        scores = jnp.where(causal, scores, 0.0).astype(BF16)
        v_all = v_ref[0, sl, :]
        pv = _dot(scores, v_all)
        for h in range(nh):
            os_ = slice(GLA_HEAD_V * h, GLA_HEAD_V * (h + 1))
            oi_ref[sl, os_] = pv[c * h:c * (h + 1), os_]
        for pr in range(nh // 2):
            upd = lax.dot_general(v_all[:, pair_v[pr]], k_d[ci][:, pair_k[pr]], (((0,), (0,)), ((), ())),
                                  preferred_element_type=F32)
            upd_ref[ci, pr] = jnp.where(same_head, upd, 0.0)

    for pr in range(nh // 2):
        s_t = s_ref[pr]
        for ci in range(nchunk):
            sl = slice(ci * c, (ci + 1) * c)
            oi_ref[sl, pair_v[pr]] += lax.dot_general(q_in[ci][:, pair_k[pr]], s_t.astype(BF16),
                                                      (((1,), (1,)), ((), ())), preferred_element_type=F32)
            s_t = decay[ci][:, pair_k[pr]] * s_t + upd_ref[ci, pr]
        s_ref[pr] = s_t

    for h in range(nh):
        os_ = slice(GLA_HEAD_V * h, GLA_HEAD_V * (h + 1))
        o_h = oi_ref[:, os_]
        ms = jnp.mean(o_h * o_h, axis=-1, keepdims=True)
        o_n = o_h * lax.rsqrt(ms + RMS_EPS) * gain_ref[:, os_]
        o_ref[0, :, os_] = (o_n * _silu(g_ref[0, :, os_].astype(F32))).astype(BF16)


def _gla_group(q, k, v, g_out, pre, b_alpha, head_gain, tg):
    bsz, seq, _ = q.shape
    blk = lambda w: pl.BlockSpec((1, tg, w), lambda b, t: (b, t, 0))
    ba = b_alpha.reshape(1, GLA_DK)
    gain = head_gain.reshape(1, GLA_WIDTH)
    return pl.pallas_call(
        functools.partial(_gla_kernel, nchunk=tg // GLA_CHUNK),
        grid=(bsz, seq // tg),
        in_specs=[
            blk(GLA_DK), blk(GLA_DK), blk(GLA_WIDTH), blk(GLA_WIDTH), blk(GLA_DK),
            pl.BlockSpec((1, GLA_DK), lambda b, t: (0, 0)),
            pl.BlockSpec((1, GLA_WIDTH), lambda b, t: (0, 0)),
        ],
        out_specs=blk(GLA_WIDTH),
        out_shape=jax.ShapeDtypeStruct((bsz, seq, GLA_WIDTH), BF16),
        scratch_shapes=[
            pltpu.VMEM((GLA_HEADS // 2, 2 * GLA_HEAD_V, 2 * GLA_HEAD_K), F32),
            pltpu.VMEM((tg, GLA_WIDTH), F32),
            pltpu.VMEM((tg // GLA_CHUNK, GLA_HEADS // 2, 2 * GLA_HEAD_V, 2 * GLA_HEAD_K), F32),
        ],
        compiler_params=_cparams("parallel", "arbitrary"),
        name="gla_group",
    )(q, k, v, g_out, pre, ba, gain)


def _mix_out(ys_ref, yg_ref, x_ref, mod_ref, w_ref, lg_ref, lb_ref, alpha):
    y = _dot(ys_ref[...], w_ref[0:S5_WIDTH, :]) + _dot(yg_ref[0], w_ref[S5_WIDTH:, :])
    gt = mod_ref[0, 2:3, :]
    x1 = _layer_norm(alpha * x_ref[0] + (1.0 + gt) * y, lg_ref[...], lb_ref[...])
    h2 = x1 * (1.0 + mod_ref[0, 4:5, :]) + mod_ref[0, 3:4, :]
    return x1, h2


RT_E1, RT_E2, RT_R1, RT_R2, RT_W1, RT_W2 = range(6)


def _top2_route(logits, count, earlier):
    lane =lax.broadcasted_iota(jnp.int32, logits.shape, 1).astype(F32)
    l0 = jnp.where(lane < N_EXPERTS, logits, NEG_BIG)
    m1 = jnp.max(l0, axis=-1, keepdims=True)
    i1 = jnp.min(jnp.where(l0 == m1, lane, float(ROUTER_LANES)), axis=-1, keepdims=True)
    sel1 = lane == i1
    l1 = jnp.where(sel1, NEG_BIG, l0)
    m2 = jnp.max(l1, axis=-1, keepdims=True)
    i2 = jnp.min(jnp.where(l1 == m2, lane, float(ROUTER_LANES)), axis=-1, keepdims=True)
    sel2 = lane == i2
    e2 = jnp.exp(m2 - m1)
    w1 = 1.0 / (1.0 + e2)
    w2 = e2 / (1.0 + e2)
    chosen = jnp.where(sel1, 1.0, 0.0) + jnp.where(sel2, 1.0, 0.0)
    before = _dot(earlier, chosen.astype(BF16)) + count
    r1 = jnp.sum(jnp.where(sel1, before, 0.0), axis=-1, keepdims=True)
    r2 = jnp.sum(jnp.where(sel2, before, 0.0), axis=-1, keepdims=True)
    rec = jnp.zeros_like(logits)
    for ln, val in ((RT_E1, i1), (RT_E2, i2), (RT_R1, r1), (RT_R2, r2), (RT_W1, w1), (RT_W2, w2)):
        rec = jnp.where(lane == float(ln), val, rec)
    return rec, count + jnp.sum(chosen, axis=0, keepdims=True)


def _outproj_router_kernel(ys_ref, yg_ref, x_ref, mod_ref, w_ref, lg_ref, lb_ref, wr_ref, br_ref, earlier_ref,
                           x1_ref, route_ref, count_ref, cnt, *, alpha):
    @pl.when((pl.program_id(0) == 0) & (pl.program_id(1) == 0))
    def _():
        cnt[...] = jnp.zeros_like(cnt)

    x1, h2 = _mix_out(ys_ref, yg_ref, x_ref, mod_ref, w_ref, lg_ref, lb_ref, alpha)
    x1_ref[0] = x1
    logits = _dot_f32x3(h2, wr_ref[...]) + br_ref[...]
    rec, new_count = _top2_route(logits, cnt[...], earlier_ref[...])
    route_ref[0] = rec
    cnt[...] = new_count
    count_ref[...] = jnp.broadcast_to(new_count, count_ref.shape)


def _out_projection_router(y_s5_tm, y_gla, x, mod_l, w_out, ln_g, ln_b, w_router, b_router, alpha, tm):
    bsz, seq, d = x.shape
    row = lambda a: a.reshape(1, -1)
    const = lambda shape: pl.BlockSpec(shape, lambda b, t: (0,) * len(shape))
    pad = ROUTER_LANES - N_EXPERTS
    return pl.pallas_call(
        functools.partial(_outproj_router_kernel, alpha=alpha),
        grid=(bsz, seq // tm),
        in_specs=[
            pl.BlockSpec((tm, S5_WIDTH), lambda b, t: (t, b)),
            pl.BlockSpec((1, tm, GLA_WIDTH), lambda b, t: (b, t, 0)),
            pl.BlockSpec((1, tm, d), lambda b, t: (b, t, 0)),
            pl.BlockSpec((1, N_MOD, d), lambda b, t: (b, 0, 0)),
            const((d, d)), const((1, d)), const((1, d)),
            const((d, ROUTER_LANES)), const((1, ROUTER_LANES)), const((tm, tm)),
        ],
        out_specs=[
            pl.BlockSpec((1, tm, d), lambda b, t: (b, t, 0)),
            pl.BlockSpec((1, tm, ROUTER_LANES), lambda b, t: (b, t, 0)),
            const((8, ROUTER_LANES)),
        ],
        out_shape=[
            jax.ShapeDtypeStruct((bsz, seq, d), F32),
            jax.ShapeDtypeStruct((bsz, seq, ROUTER_LANES), F32),
            jax.ShapeDtypeStruct((8, ROUTER_LANES), F32),
        ],
        scratch_shapes=[pltpu.VMEM((1, ROUTER_LANES), F32)],
        compiler_params=_cparams("arbitrary", "arbitrary"),
        name="out_projection",
    )(y_s5_tm, y_gla, x, mod_l, w_out.astype(BF16), row(ln_g), row(ln_b),
      jnp.pad(w_router, ((0, 0), (0, pad))), jnp.pad(row(b_router), ((0, 0), (0, pad))),
      jnp.tri(tm, k=-1, dtype=BF16))


def _mix_ffn_kernel(ys_ref, yg_ref, x_ref, mod_ref, wo_ref, lg1_ref, lb1_ref, wg_ref, wu_ref, wd_ref,
                    lg2_ref, lb2_ref, o_ref, acc_ref, x1_ref, hb_ref, *, alpha):
    j = pl.program_id(1)

    @pl.when(j == 0)
    def _():
        x1, h2 = _mix_out(ys_ref, yg_ref, x_ref, mod_ref, wo_ref, lg1_ref, lb1_ref, alpha)
        x1_ref[...] = x1
        hb_ref[...] = h2.astype(BF16)
        acc_ref[...] = jnp.zeros_like(acc_ref)

    h = hb_ref[...]
    a = _dot(h, wg_ref[...])
    mid = (_silu(a) * _dot(h, wu_ref[...])).astype(BF16)
    acc_ref[...] += _dot(mid, wd_ref[...])

    @pl.when(j == pl.num_programs(1) - 1)
    def _():
        gt = mod_ref[0, 5:6, :]
        o_ref[0] = _layer_norm(alpha * x1_ref[...] + (1.0 + gt) * acc_ref[...], lg2_ref[...], lb2_ref[...])


def _mix_dense_ffn(y_s5_tm, y_gla, x, mod_l, w_out, ln1_g, ln1_b, w_gate, w_up, w_down, ln2_g, ln2_b,
                   alpha, tm, tf):
    bsz, seq, d = x.shape
    dff = w_gate.shape[1]
    per_b = seq // tm
    row = lambda a: a.reshape(1, d)
    const = lambda shape: pl.BlockSpec(shape, lambda i, j: (0,) * len(shape))
    tok = lambda w: pl.BlockSpec((1, tm, w), lambda i, j: (i // per_b, i % per_b, 0))
    return pl.pallas_call(
        functools.partial(_mix_ffn_kernel, alpha=alpha),
        grid=(bsz * per_b, dff // tf),
        in_specs=[
            pl.BlockSpec((tm, S5_WIDTH), lambda i, j: (i % per_b, i // per_b)),
            tok(GLA_WIDTH),
            tok(d),
            pl.BlockSpec((1, N_MOD, d), lambda i, j: (i // per_b, 0, 0)),
            const((d, d)), const((1, d)), const((1, d)),
            pl.BlockSpec((d, tf), lambda i, j: (0, j)),
            pl.BlockSpec((d, tf), lambda i, j: (0, j)),
            pl.BlockSpec((tf, d), lambda i, j: (j, 0)),
            const((1, d)), const((1, d)),
        ],
        out_specs=tok(d),
        out_shape=jax.ShapeDtypeStruct((bsz, seq, d), F32),
        scratch_shapes=[pltpu.VMEM((tm, d), F32), pltpu.VMEM((tm, d), F32), pltpu.VMEM((tm, d), BF16)],
        compiler_params=_cparams("parallel", "arbitrary"),
        name="mix_dense_ffn",
    )(y_s5_tm, y_gla, x, mod_l, w_out.astype(BF16), row(ln1_g), row(ln1_b),
      w_gate.astype(BF16), w_up.astype(BF16), w_down.astype(BF16), row(ln2_g), row(ln2_b))


def _route_tables(route, count, rb, nblk):
    e = route[:, RT_E1:RT_E2 + 1].astype(jnp.int32)
    rank = route[:, RT_R1:RT_R2 + 1].astype(jnp.int32)
    cnt = count[0, :N_EXPERTS].astype(jnp.int32)
    padded = ((cnt + rb - 1) // rb) * rb
    ends = jnp.cumsum(padded)
    starts = ends - padded
    pos = jnp.sum(jnp.where(e[..., None] == jnp.arange(N_EXPERTS), starts, 0), axis=-1) + rank
    n_valid = ends[-1] // rb
    blk = jnp.minimum(jnp.arange(nblk, dtype=jnp.int32), n_valid - 1)
    blk_expert = jnp.sum((blk[:, None] * rb >= ends[None, :]).astype(jnp.int32), axis=-1)
    blk_expert = jnp.minimum(blk_expert, N_EXPERTS - 1)
    n_rows = jnp.full((1,), nblk * rb, jnp.int32)
    fill = jnp.stack([jnp.concatenate([starts + cnt, ends[-1:]]), jnp.concatenate([ends, n_rows])], axis=1)
    return (pos.astype(jnp.int32), blk_expert.astype(jnp.int32), n_valid.astype(jnp.int32).reshape(1),
            fill.reshape(-1).astype(jnp.int32))


SUBLANES = 8


def _staged_row(ref, tile, sub):
    return ref.at[tile, pl.ds(sub, 1), :]


def _dispatch_kernel(fill_ref, pos_ref, x_ref, mod_ref, xs_hbm, hs_ref, sem, *, tb):
    i = pl.program_id(0)
    h = x_ref[...] * (1.0 + mod_ref[0, 4:5, :]) + mod_ref[0, 3:4, :]
    hs_ref[...] = h.reshape(tb // SUBLANES, SUBLANES, h.shape[-1])

    def issue(r8, carry):
        for sub in range(SUBLANES):
            for kk in range(2):
                dst_row = pos_ref[0, 0, 2 * SUBLANES * r8 + 2 * sub + kk]
                pltpu.make_async_copy(_staged_row(hs_ref, r8, sub), xs_hbm.at[pl.ds(dst_row, 1), :],
                                      sem.at[0]).start(priority=kk)
        return carry

    lax.fori_loop(0, tb // SUBLANES, issue, 0)

    @pl.when(i == 0)
    def _():
        for e in range(N_EXPERTS + 1):
            lo = fill_ref[2 * e]
            hi = fill_ref[2 * e + 1]

            def fill(p, carry):
                pltpu.make_async_copy(_staged_row(hs_ref, 0, 0), xs_hbm.at[pl.ds(p, 1), :], sem.at[1]).start()
                return carry

            def drain(p, carry):
                pltpu.make_async_copy(_staged_row(hs_ref, 0, 0), xs_hbm.at[pl.ds(p, 1), :], sem.at[1]).wait()
                return carry

            lax.fori_loop(lo, hi, fill, 0)
            lax.fori_loop(lo, hi, drain, 0)

    for _ in range(2):
        pltpu.make_async_copy(x_ref, xs_hbm.at[pl.ds(0, tb), :], sem.at[0]).wait()


def _dispatch(x1, mod_l, pos, fill, n_rows, tb, seq):
    n, d = x1.shape
    per_b = seq // tb
    grid_spec = pltpu.PrefetchScalarGridSpec(
        num_scalar_prefetch=1,
        grid=(n // tb,),
        in_specs=[
            pl.BlockSpec((1, 1, 2 * tb), lambda i, f: (i, 0, 0), memory_space=pltpu.SMEM),
            pl.BlockSpec((tb, d), lambda i, f: (i, 0)),
            pl.BlockSpec((1, N_MOD, d), lambda i, f: (i // per_b, 0, 0)),
        ],
        out_specs=pl.BlockSpec(memory_space=pl.ANY),
        scratch_shapes=[pltpu.VMEM((tb // SUBLANES, SUBLANES, d), F32), pltpu.SemaphoreType.DMA((2,))],
    )
    return pl.pallas_call(
        functools.partial(_dispatch_kernel, tb=tb),
        grid_spec=grid_spec,
        out_shape=jax.ShapeDtypeStruct((n_rows, d), F32),
        compiler_params=_cparams("arbitrary"),
        name="moe_dispatch",
    )(fill, pos.reshape(n // tb, 1, 2 * tb), x1, mod_l)


def _expert_kernel(be_ref, nv_ref, xs_ref, wg_ref, wu_ref, wd_ref, y_ref, acc_ref, xb_ref):
    del be_ref
    i = pl.program_id(0)
    j = pl.program_id(1)

    @pl.when(i < nv_ref[0])
    def _():
        @pl.when(j == 0)
        def _():
            acc_ref[...] = jnp.zeros_like(acc_ref)
            xb_ref[...] = xs_ref[...].astype(BF16)

        h = xb_ref[...]
        a = _dot(h, wg_ref[0])
        mid = (_silu(a) * _dot(h, wu_ref[0])).astype(BF16)
        acc_ref[...] += _dot(mid, wd_ref[0])

        @pl.when(j == pl.num_programs(1) - 1)
        def _():
            y_ref[...] = acc_ref[...]

    @pl.when((i >= nv_ref[0]) & (j == 0))
    def _():
        y_ref[...] = jnp.zeros_like(y_ref)


def _experts(xs, blk_expert, n_valid, w_gate, w_up, w_down, rb, tf):
    n_rows, d = xs.shape
    dff = w_gate.shape[2]
    nj = dff // tf
    row_blk = lambda i, j, be, nv: (jnp.minimum(i, nv[0] - 1), 0)
    jj = lambda i, j, nv: jnp.where(i < nv[0], j, nj - 1)
    grid_spec = pltpu.PrefetchScalarGridSpec(
        num_scalar_prefetch=2,
        grid=(n_rows // rb, nj),
        in_specs=[
            pl.BlockSpec((rb, d), row_blk),
            pl.BlockSpec((1, d, tf), lambda i, j, be, nv: (be[i], 0, jj(i, j, nv))),
            pl.BlockSpec((1, d, tf), lambda i, j, be, nv: (be[i], 0, jj(i, j, nv))),
            pl.BlockSpec((1, tf, d), lambda i, j, be, nv: (be[i], jj(i, j, nv), 0)),
        ],
        out_specs=pl.BlockSpec((rb, d), lambda i, j, be, nv: (i, 0)),
        scratch_shapes=[pltpu.VMEM((rb, d), F32), pltpu.VMEM((rb, d), BF16)],
    )
    return pl.pallas_call(
        _expert_kernel,
        grid_spec=grid_spec,
        out_shape=jax.ShapeDtypeStruct((n_rows, d), F32),
        compiler_params=_cparams("arbitrary", "arbitrary"),
        name="moe_experts",
    )(blk_expert, n_valid, xs, w_gate.astype(BF16), w_up.astype(BF16), w_down.astype(BF16))


def _combine_kernel(pos_ref, posn_ref, route_ref, x_ref, mod_ref, lg_ref, lb_ref, y_hbm, o_ref, buf, sem,
                    *, tb, alpha):
    i = pl.program_id(0)
    slot = i % 2
    d = o_ref.shape[-1]
    ntile = tb // SUBLANES

    def issue_tile(p_ref, s, r8):
        for sub in range(SUBLANES):
            for kk in range(2):
                src_row = p_ref[0, 0, 2 * SUBLANES * r8 + 2 * sub + kk]
                pltpu.make_async_copy(y_hbm.at[pl.ds(src_row, 1), :], _staged_row(buf.at[s, kk], r8, sub),
                                      sem.at[s]).start(priority=kk)

    def wait_slot(s):
        for _ in range(2):
            pltpu.make_async_copy(y_hbm.at[pl.ds(0, tb), :], o_ref, sem.at[s]).wait()

    def issue_all(p_ref, s):
        def body(r8, carry):
            issue_tile(p_ref, s, r8)
            return carry
        lax.fori_loop(0, ntile, body, 0)

    @pl.when(i == 0)
    def _():
        issue_all(pos_ref, 0)

    issue_all(posn_ref, 1 - slot)
    wait_slot(slot)
    rec = route_ref[...]
    y0 = buf[slot, 0].reshape(tb, d)
    y1 = buf[slot, 1].reshape(tb, d)
    f = rec[:, RT_W1:RT_W1 + 1] * y0 + rec[:, RT_W2:RT_W2 + 1] * y1
    gt = mod_ref[0, 5:6, :]
    o_ref[...] = _layer_norm(alpha * x_ref[...] + (1.0 + gt) * f, lg_ref[...], lb_ref[...])

    @pl.when(i == pl.num_programs(0) - 1)
    def _():
        wait_slot(1 - slot)


def _combine(ys, pos, route, x1, mod_l, ln_g, ln_b, alpha, tb, seq):
    n, d = x1.shape
    nb = n // tb
    per_b = seq // tb
    pos3 = pos.reshape(nb, 1, 2 * tb)
    return pl.pallas_call(
        functools.partial(_combine_kernel, tb=tb, alpha=alpha),
        grid=(nb,),
        in_specs=[
            pl.BlockSpec((1, 1, 2 * tb), lambda i: (i, 0, 0), memory_space=pltpu.SMEM),
            pl.BlockSpec((1, 1, 2 * tb), lambda i: (jnp.minimum(i + 1, nb - 1), 0, 0), memory_space=pltpu.SMEM),
            pl.BlockSpec((tb, ROUTER_LANES), lambda i: (i, 0)),
            pl.BlockSpec((tb, d), lambda i: (i, 0)),
            pl.BlockSpec((1, N_MOD, d), lambda i: (i // per_b, 0, 0)),
            pl.BlockSpec((1, d), lambda i: (0, 0)),
            pl.BlockSpec((1, d), lambda i: (0, 0)),
            pl.BlockSpec(memory_space=pl.ANY),
        ],
        out_specs=pl.BlockSpec((tb, d), lambda i: (i, 0)),
        out_shape=jax.ShapeDtypeStruct((n, d), F32),
        scratch_shapes=[pltpu.VMEM((2, 2, tb // SUBLANES, SUBLANES, d), F32), pltpu.SemaphoreType.DMA((2,))],
        compiler_params=_cparams("arbitrary"),
        name="moe_combine",
    )(pos3, pos3, route, x1, mod_l, ln_g.reshape(1, d), ln_b.reshape(1, d), ys)


def _moe_ffn(x1, route, count, mod_l, w_gate, w_up, w_down, ln_g, ln_b, alpha, rb, tf, tb):
    bsz, seq, d = x1.shape
    n = bsz * seq
    n_rows = 2 * n + N_EXPERTS * rb
    route2 = route.reshape(n, ROUTER_LANES)
    pos, blk_expert, n_valid, fill = _route_tables(route2, count, rb, n_rows // rb)
    xs = _dispatch(x1.reshape(n, d), mod_l, pos, fill, n_rows, rb, seq)
    ys = _experts(xs, blk_expert, n_valid, w_gate, w_up, w_down, rb, tf)
    out = _combine(ys, pos, route2, x1.reshape(n, d), mod_l, ln_g, ln_b, alpha, tb, seq)
    return out.reshape(bsz, seq, d)


def _tile(n, want):
    t = min(n, want)
    assert n % t == 0, (n, want)
    return t


def kernel(x, c, mod_w, mod_b, w_in, w_out, s5_lam_re, s5_lam_im, s5_log_dt, s5_b_re, s5_b_im, s5_c_re, s5_c_im, s5_d, s5_w_glu, s5_b_glu, gla_w_alpha_up, gla_b_alpha, gla_head_gain, ln_mix_g, ln_mix_b, ffn_w_gate, ffn_w_up, ffn_w_down, moe_w_router, moe_b_router, moe_w_gate, moe_w_up, moe_w_down, ln_ffn_g, ln_ffn_b):
    bsz, seq, d = x.shape
    depth = mod_w.shape[0]
    alpha = (2.0 * depth) ** 0.25
    tm = _tile(seq, 512)
    tt = _tile(seq, 32)
    tg = _tile(seq, 1024)
    tm_ffn = _tile(seq, 1024)
    tf = _tile(D_FF, 512)
    tb = _tile(seq, 512)

    mod = _modulation(c, mod_w, mod_b).reshape(depth, bsz, N_MOD, d)
    for layer in range(depth):
        mod_l = mod[layer]
        w_gate_fold = _gate_fold(w_in[layer][:, IN_P[0]:], gla_w_alpha_up[layer])
        w_cat = jnp.concatenate([w_in[layer][:, :IN_P[0]], w_gate_fold], axis=1).astype(BF16)
        u_tm, q, k, v, g_out, pre = _in_projection(x, mod_l, w_cat, tm_ffn)

        s5w = _s5_weights(s5_lam_re[layer], s5_lam_im[layer], s5_log_dt[layer], s5_b_re[layer], s5_b_im[layer],
                          s5_c_re[layer], s5_c_im[layer])
        y_s5 = _s5_group(u_tm.reshape(seq, bsz, S5_WIDTH), *s5w, s5_d[layer], s5_w_glu[layer], s5_b_glu[layer], tt)
        y_s5 = y_s5.reshape(seq, bsz * S5_WIDTH)
        y_gla = _gla_group(q, k, v, g_out, pre, gla_b_alpha[layer], gla_head_gain[layer], tg)

        i = layer // 2
        if layer % 2 == 0:
            x = _mix_dense_ffn(y_s5, y_gla, x, mod_l, w_out[layer], ln_mix_g[layer], ln_mix_b[layer],
                               ffn_w_gate[i], ffn_w_up[i], ffn_w_down[i], ln_ffn_g[layer], ln_ffn_b[layer],
                               alpha, tm_ffn, tf)
        else:
            x1, route, count = _out_projection_router(y_s5, y_gla, x, mod_l, w_out[layer], ln_mix_g[layer],
                                                      ln_mix_b[layer], moe_w_router[i], moe_b_router[i], alpha, tm)
            x = _moe_ffn(x1, route, count, mod_l, moe_w_gate[i], moe_w_up[i], moe_w_down[i],
                         ln_ffn_g[layer], ln_ffn_b[layer], alpha, tm_ffn, tf, tb)
    return x
```
